```python
import math
import jax, jax.numpy as jnp
from jax import lax
import numpy as np

D_MODEL = 1024
BATCH = 16
SEQ = 4096
DEPTH = 1
DEC_BATCH = 32
DEC_SEQ = 64
PAST_LEN = 2048

CHUNK = 64
MIX_WIDTH = 2 * D_MODEL
HEAD_DIM = 64
N_Q_HEADS = 8
N_KV_HEADS = 2
GQA = N_Q_HEADS // N_KV_HEADS
ATTN_WIDTH = N_Q_HEADS * HEAD_DIM
KV_WIDTH = N_KV_HEADS * HEAD_DIM
WINDOW = 128
WIN_CHUNKS = WINDOW // CHUNK
ROPE_THETA = 10000.0
SSD_WIDTH = D_MODEL
SSD_HEAD_DIM = 64
SSD_HEADS = SSD_WIDTH // SSD_HEAD_DIM
SSD_GROUPS = 2
SSD_HPG = SSD_HEADS // SSD_GROUPS
SSD_STATE = 128
CONV_WIDTH = 4
CONV_CH = SSD_WIDTH + 2 * SSD_GROUPS * SSD_STATE
MEM_TOKENS = 256
MEM_HEADS = 4
MEM_HEAD_DIM = 128
MEM_WIDTH = MEM_HEADS * MEM_HEAD_DIM
EPS = 1e-6
IN_SIZES = (ATTN_WIDTH, KV_WIDTH, KV_WIDTH, ATTN_WIDTH, SSD_WIDTH, CONV_CH, SSD_HEADS, MEM_WIDTH, MEM_WIDTH)
IN_COLS = ATTN_WIDTH + 2 * KV_WIDTH + ATTN_WIDTH + SSD_WIDTH + CONV_CH + SSD_HEADS + 2 * MEM_WIDTH

kernel_name = "hymba_swa_ssd_memory_stream_step"


def split_cols(a, sizes):
    out, start = [], 0
    for s in sizes:
        out.append(a[..., start:start + s])
        start += s
    return out


def rmsnorm(x, g):
    xf = x.astype(jnp.float32)
    y = xf * lax.rsqrt(jnp.mean(xf * xf, axis=-1, keepdims=True) + EPS)
    return (y * g.astype(jnp.float32)).astype(x.dtype)


def rope(x, positions):
    half = x.shape[-1] // 2
    freqs = ROPE_THETA ** (-jnp.arange(half, dtype=jnp.float32) / half)
    ang = positions.astype(jnp.float32)[:, None] * freqs[None, :]
    cos = jnp.cos(ang)[:, None, :]
    sin = jnp.sin(ang)[:, None, :]
    x1 = x[..., :half].astype(jnp.float32)
    x2 = x[..., half:].astype(jnp.float32)
    return jnp.concatenate([x1 * cos - x2 * sin, x2 * cos + x1 * sin], axis=-1).astype(x.dtype)


def sink_softmax(s, sinks):
    sink = sinks.astype(jnp.float32).reshape(N_KV_HEADS, GQA, 1, 1)
    m = jnp.maximum(jnp.max(s, axis=-1, keepdims=True), sink)
    e = jnp.exp(s - m)
    return e / (jnp.sum(e, axis=-1, keepdims=True) + jnp.exp(sink - m))


def window_attn_prompt(q, k, v, sinks):
    b, t = q.shape[:2]
    nc = t // CHUNK
    qc = q.reshape(b, nc, CHUNK, N_KV_HEADS, GQA, HEAD_DIM)
    kc = k.reshape(b, nc, CHUNK, N_KV_HEADS, HEAD_DIM)
    vc = v.reshape(b, nc, CHUNK, N_KV_HEADS, HEAD_DIM)
    pad = ((0, 0), (WIN_CHUNKS, 0), (0, 0), (0, 0), (0, 0))
    kp = jnp.pad(kc, pad)
    vp = jnp.pad(vc, pad)
    kb = jnp.concatenate([kp[:, j:j + nc] for j in range(WIN_CHUNKS + 1)], axis=2)
    vb = jnp.concatenate([vp[:, j:j + nc] for j in range(WIN_CHUNKS + 1)], axis=2)
    chunk_idx = jnp.arange(nc)[:, None] - WIN_CHUNKS + jnp.arange(WIN_CHUNKS + 1)[None, :]
    valid = jnp.repeat(chunk_idx >= 0, CHUNK, axis=1)
    s = jnp.einsum('bcqhgd,bckhd->bchgqk', qc, kb, preferred_element_type=jnp.float32) / math.sqrt(HEAD_DIM)
    s = jnp.where(valid[None, :, None, None, None, :], s, -jnp.inf)
    p = sink_softmax(s, sinks)
    out = jnp.einsum('bchgqk,bckhd->bcqhgd', p.astype(v.dtype), vb)
    return out.reshape(b, t, ATTN_WIDTH)


def window_attn_sample(q, k_all, v_all, sinks):
    b, t = q.shape[:2]
    qg = q.reshape(b, t, N_KV_HEADS, GQA, HEAD_DIM)
    s = jnp.einsum('bqhgd,bkhd->bhgqk', qg, k_all, preferred_element_type=jnp.float32) / math.sqrt(HEAD_DIM)
    p = sink_softmax(s, sinks)
    out = jnp.einsum('bhgqk,bkhd->bqhgd', p.astype(v_all.dtype), v_all)
    return out.reshape(b, t, ATTN_WIDTH)


def causal_conv(xbc, conv_init, w, bias):
    t = xbc.shape[1]
    xp = jnp.concatenate([conv_init.astype(xbc.dtype), xbc], axis=1)
    out = bias
    for j in range(CONV_WIDTH):
        out = out + xp[:, j:j + t] * w[j]
    return jax.nn.silu(out), xp[:, -(CONV_WIDTH - 1):]


def ssd_scan(x, dt, a, bm, cm, init_state, chunk):
    b, t = x.shape[:2]
    nc = t // chunk
    f32 = jnp.float32
    xc = x.reshape(b, nc, chunk, SSD_GROUPS, SSD_HPG, SSD_HEAD_DIM).astype(f32)
    dtc = dt.reshape(b, nc, chunk, SSD_GROUPS, SSD_HPG)
    bc = bm.reshape(b, nc, chunk, SSD_GROUPS, SSD_STATE).astype(f32)
    cc = cm.reshape(b, nc, chunk, SSD_GROUPS, SSD_STATE).astype(f32)
    acum = jnp.cumsum(dtc * a.reshape(SSD_GROUPS, SSD_HPG), axis=2)
    diff = acum[:, :, :, None] - acum[:, :, None, :]
    causal = jnp.tril(jnp.ones((chunk, chunk), bool))[:, :, None, None]
    decay = jnp.exp(jnp.where(causal, diff, -jnp.inf))
    cb = jnp.einsum('bclgn,bcsgn->bclsg', cc, bc)
    w_ls = cb[..., None] * decay * dtc[:, :, None]
    y_diag = jnp.einsum('bclsgr,bcsgrp->bclgrp', w_ls, xc)
    decay_to_end = jnp.exp(acum[:, :, -1:] - acum) * dtc
    states = jnp.einsum('bclgn,bclgr,bclgrp->bcgrpn', bc, decay_to_end, xc)
    chunk_decay = jnp.exp(acum[:, :, -1])

    def step(s, inp):
        st, dec = inp
        return s * dec[..., None, None] + st, s

    init = init_state.reshape(b, SSD_GROUPS, SSD_HPG, SSD_HEAD_DIM, SSD_STATE).astype(f32)
    final, s_in = lax.scan(step, init, (jnp.moveaxis(states, 1, 0), jnp.moveaxis(chunk_decay, 1, 0)))
    s_in = jnp.moveaxis(s_in, 0, 1)
    y_off = jnp.einsum('bclgn,bcgrpn->bclgrp', cc, s_in) * jnp.exp(acum)[..., None]
    y = (y_diag + y_off).reshape(b, t, SSD_HEADS, SSD_HEAD_DIM)
    return y, final.reshape(b, SSD_HEADS, SSD_HEAD_DIM, SSD_STATE)


def memory_kv(mem, g_mem, w_mem_k, w_mem_v, g_mk):
    b = mem.shape[0]
    h = rmsnorm(mem, g_mem)
    k = rmsnorm((h @ w_mem_k).reshape(b, MEM_TOKENS, MEM_HEADS, MEM_HEAD_DIM), g_mk)
    v = (h @ w_mem_v).reshape(b, MEM_TOKENS, MEM_HEADS, MEM_HEAD_DIM)
    return k, v


def memory_attn(q, mk, mv):
    s = jnp.einsum('bqhd,bkhd->bhqk', q, mk, preferred_element_type=jnp.float32) / math.sqrt(MEM_HEAD_DIM)
    p = jax.nn.softmax(s, axis=-1)
    return jnp.einsum('bhqk,bkhd->bqhd', p.astype(mv.dtype), mv)


def parallel_layer(x, positions, win_k_past, win_v_past, ssm_init, conv_init, mem_k, mem_v, scan_chunk,
                   g_norm, w_in, g_q, g_k, sinks, conv_w, conv_b, dt_bias, a_log, d_skip, g_ssd, g_mq, w_out):
    b, t, _ = x.shape
    h = rmsnorm(x, g_norm)
    q, k, v, gate_a, z, xbc, dt_raw, mq, gate_m = split_cols(h @ w_in, IN_SIZES)

    q = rope(rmsnorm(q.reshape(b, t, N_Q_HEADS, HEAD_DIM), g_q), positions)
    k = rope(rmsnorm(k.reshape(b, t, N_KV_HEADS, HEAD_DIM), g_k), positions)
    v = v.reshape(b, t, N_KV_HEADS, HEAD_DIM)
    if win_k_past is None:
        a_out = window_attn_prompt(q, k, v, sinks)
        k_all, v_all = k, v
    else:
        k_all = jnp.concatenate([win_k_past.astype(k.dtype), k], axis=1)
        v_all = jnp.concatenate([win_v_past.astype(v.dtype), v], axis=1)
        a_out = window_attn_sample(q, k_all, v_all, sinks)
    new_k = k_all[:, -WINDOW:]
    new_v = v_all[:, -WINDOW:]
    a_out = a_out * jax.nn.silu(gate_a)

    xbc, conv_state = causal_conv(xbc, conv_init, conv_w, conv_b)
    xs, bm, cm = split_cols(xbc, (SSD_WIDTH, SSD_GROUPS * SSD_STATE, SSD_GROUPS * SSD_STATE))
    dt = jax.nn.softplus(dt_raw.astype(jnp.float32) + dt_bias.astype(jnp.float32))
    a = -jnp.exp(a_log.astype(jnp.float32))
    xs = xs.reshape(b, t, SSD_HEADS, SSD_HEAD_DIM)
    y, ssm_final = ssd_scan(xs, dt, a, bm.reshape(b, t, SSD_GROUPS, SSD_STATE),
                            cm.reshape(b, t, SSD_GROUPS, SSD_STATE), ssm_init, scan_chunk)
    y = y + d_skip.astype(jnp.float32)[:, None] * xs.astype(jnp.float32)
    s_out = rmsnorm(y.reshape(b, t, SSD_WIDTH) * jax.nn.silu(z.astype(jnp.float32)), g_ssd).astype(x.dtype)

    mq = rmsnorm(mq.reshape(b, t, MEM_HEADS, MEM_HEAD_DIM), g_mq)
    m_out = memory_attn(mq, mem_k.astype(mq.dtype), mem_v.astype(mq.dtype)).reshape(b, t, MEM_WIDTH)
    m_out = m_out * jax.nn.silu(gate_m)

    y_out = x + jnp.concatenate([a_out, s_out, m_out], axis=-1) @ w_out
    return y_out, new_k, new_v, ssm_final.astype(ssm_init.dtype), conv_state


def setup_inputs(seed: int = 0) -> dict:
    key = jax.random.key(seed)
    ks = jax.random.split(key, 32)
    f32 = jnp.float32

    def nrm(k, shape, scale=1.0):
        return jax.random.normal(k, shape, f32) * scale

    dt0 = jnp.exp(jax.random.uniform(ks[16], (DEPTH, SSD_HEADS), f32, math.log(1e-3), math.log(1e-1)))
    return {
        "x_prompt": nrm(ks[0], (BATCH, SEQ, D_MODEL)),
        "x_sample": nrm(ks[1], (DEC_BATCH, DEC_SEQ, D_MODEL)),
        "cache_win_k": nrm(ks[2], (DEPTH, DEC_BATCH, WINDOW, N_KV_HEADS, HEAD_DIM)),
        "cache_win_v": nrm(ks[3], (DEPTH, DEC_BATCH, WINDOW, N_KV_HEADS, HEAD_DIM)),
        "state_ssm": nrm(ks[4], (DEPTH, DEC_BATCH, SSD_HEADS, SSD_HEAD_DIM, SSD_STATE), 0.1),
        "state_conv": nrm(ks[5], (DEPTH, DEC_BATCH, CONV_WIDTH - 1, CONV_CH)),
        "cache_mem_k": nrm(ks[6], (DEPTH, DEC_BATCH, MEM_TOKENS, MEM_HEADS, MEM_HEAD_DIM)),
        "cache_mem_v": nrm(ks[7], (DEPTH, DEC_BATCH, MEM_TOKENS, MEM_HEADS, MEM_HEAD_DIM)),
        "mem_prompt": nrm(ks[8], (BATCH, MEM_TOKENS, D_MODEL)),
        "g_norm": 1.0 + nrm(ks[9], (DEPTH, D_MODEL), 0.01),
        "w_in": nrm(ks[10], (DEPTH, D_MODEL, IN_COLS), D_MODEL ** -0.5),
        "g_q": 1.0 + nrm(ks[11], (DEPTH, HEAD_DIM), 0.01),
        "g_k": 1.0 + nrm(ks[12], (DEPTH, HEAD_DIM), 0.01),
        "sinks": nrm(ks[13], (DEPTH, N_Q_HEADS), 0.5),
        "conv_w": nrm(ks[14], (DEPTH, CONV_WIDTH, CONV_CH), CONV_WIDTH ** -0.5),
        "conv_b": nrm(ks[15], (DEPTH, CONV_CH), 0.01),
        "dt_bias": dt0 + jnp.log(-jnp.expm1(-dt0)),
        "a_log": jnp.log(jax.random.uniform(ks[17], (DEPTH, SSD_HEADS), f32, 1.0, 16.0)),
        "d_skip": 1.0 + nrm(ks[18], (DEPTH, SSD_HEADS), 0.01),
        "g_ssd": 1.0 + nrm(ks[19], (DEPTH, SSD_WIDTH), 0.01),
        "g_mem": 1.0 + nrm(ks[20], (DEPTH, D_MODEL), 0.01),
        "w_mem_k": nrm(ks[21], (DEPTH, D_MODEL, MEM_WIDTH), D_MODEL ** -0.5),
        "w_mem_v": nrm(ks[22], (DEPTH, D_MODEL, MEM_WIDTH), D_MODEL ** -0.5),
        "g_mk": 1.0 + nrm(ks[23], (DEPTH, MEM_HEAD_DIM), 0.01),
        "g_mq": 1.0 + nrm(ks[24], (DEPTH, MEM_HEAD_DIM), 0.01),
        "w_out": nrm(ks[25], (DEPTH, MIX_WIDTH, D_MODEL), MIX_WIDTH ** -0.5),
    }


def reference(x_prompt, x_sample, cache_win_k, cache_win_v, state_ssm, state_conv, cache_mem_k, cache_mem_v,
              mem_prompt, g_norm, w_in, g_q, g_k, sinks, conv_w, conv_b, dt_bias, a_log, d_skip, g_ssd,
              g_mem, w_mem_k, w_mem_v, g_mk, g_mq, w_out):
    bp, t_p = x_prompt.shape[:2]
    t_s = x_sample.shape[1]
    pos_p = jnp.arange(t_p, dtype=jnp.int32)
    pos_s = PAST_LEN + jnp.arange(t_s, dtype=jnp.int32)
    xp, xs = x_prompt, x_sample
    wk_p, wv_p, ssm_p, conv_p, mk_p_l, mv_p_l = [], [], [], [], [], []
    wk_s, wv_s, ssm_s, conv_s = [], [], [], []
    for l in range(DEPTH):
        mk_p, mv_p = memory_kv(mem_prompt, g_mem[l], w_mem_k[l], w_mem_v[l], g_mk[l])
        lw = (g_norm[l], w_in[l], g_q[l], g_k[l], sinks[l], conv_w[l], conv_b[l], dt_bias[l], a_log[l],
              d_skip[l], g_ssd[l], g_mq[l], w_out[l])
        zeros_ssm = jnp.zeros((bp, SSD_HEADS, SSD_HEAD_DIM, SSD_STATE), x_prompt.dtype)
        zeros_conv = jnp.zeros((bp, CONV_WIDTH - 1, CONV_CH), x_prompt.dtype)
        xp, kp, vp, sp, cp = parallel_layer(xp, pos_p, None, None, zeros_ssm, zeros_conv, mk_p, mv_p, CHUNK, *lw)
        xs, ks_, vs_, ss_, cs_ = parallel_layer(xs, pos_s, cache_win_k[l], cache_win_v[l], state_ssm[l],
                                                state_conv[l], cache_mem_k[l], cache_mem_v[l], t_s, *lw)
        wk_p.append(kp); wv_p.append(vp); ssm_p.append(sp); conv_p.append(cp)
        mk_p_l.append(mk_p); mv_p_l.append(mv_p)
        wk_s.append(ks_); wv_s.append(vs_); ssm_s.append(ss_); conv_s.append(cs_)
    return (xp, xs,
            jnp.stack(wk_p), jnp.stack(wv_p), jnp.stack(ssm_p), jnp.stack(conv_p),
            jnp.stack(mk_p_l), jnp.stack(mv_p_l),
            jnp.stack(wk_s), jnp.stack(wv_s), jnp.stack(ssm_s), jnp.stack(conv_s))
```

```python
import functools
import math

import numpy as np
import jax
import jax.numpy as jnp
from jax import lax
from jax.experimental import pallas as pl
from jax.experimental.pallas import tpu as pltpu

D_MODEL = 1024
CHUNK = 64
HEAD_DIM = 64
N_Q_HEADS = 8
N_KV_HEADS = 2
ATTN_WIDTH = N_Q_HEADS * HEAD_DIM
KV_WIDTH = N_KV_HEADS * HEAD_DIM
WINDOW = 128
ROPE_THETA = 10000.0
SSD_WIDTH = D_MODEL
SSD_HEAD_DIM = 64
SSD_HEADS = SSD_WIDTH // SSD_HEAD_DIM
SSD_GROUPS = 2
SSD_STATE = 128
CONV_WIDTH = 4
CONV_CH = SSD_WIDTH + 2 * SSD_GROUPS * SSD_STATE
MEM_TOKENS = 256
MEM_HEADS = 4
MEM_HEAD_DIM = 128
MEM_WIDTH = MEM_HEADS * MEM_HEAD_DIM
MIX_WIDTH = 2 * D_MODEL
EPS = 1e-6
PAST_LEN = 2048

LANES = 128
SUBLANES = 8
VMEM_LIMIT_BYTES = 56 * 1024 * 1024

C_Q = 0
C_K = C_Q + ATTN_WIDTH
C_V = C_K + KV_WIDTH
C_GA = C_V + KV_WIDTH
C_Z = C_GA + ATTN_WIDTH
C_XBC = C_Z + SSD_WIDTH
C_MQ = C_XBC + CONV_CH
C_GM = C_MQ + MEM_WIDTH
C_DT = C_GM + MEM_WIDTH
P_COLS = C_DT + LANES
ORIG_DT = C_XBC + CONV_CH
PROJ_BLOCK = 512

F32 = jnp.float32
BF16 = jnp.bfloat16
NEG_BIG = -1e30


def _dot(a, b):
    return jnp.dot(a, b, preferred_element_type=F32)


def _dot_nt(a, b):
    return lax.dot_general(a, b, (((1,), (1,)), ((), ())), preferred_element_type=F32)


def _dot_tn(a, b):
    return lax.dot_general(a, b, (((0,), (0,)), ((), ())), preferred_element_type=F32)


def _split3(x):
    h1 = x.astype(BF16)
    r1 = x - h1.astype(F32)
    h2 = r1.astype(BF16)
    r2 = r1 - h2.astype(F32)
    return h1, h2, r2.astype(BF16)


def _dot_sel_rhs(x, sel):
    a, b, c = _split3(x)
    return (_dot(c, sel) + _dot(b, sel)) + _dot(a, sel)


def _dot_sel_lhs(sel, x):
    a, b, c = _split3(x)
    return (_dot(sel, c) + _dot(sel, b)) + _dot(sel, a)


def _seg_sumsq(x, seg):
    x2 = x * x
    hi = x2.astype(BF16)
    lo = (x2 - hi.astype(F32)).astype(BF16)
    return _dot(lo, seg) + _dot(hi, seg)


def _silu(x):
    return x / (1.0 + jnp.exp(-x))


def _softplus(x):
    return jnp.maximum(x, 0.0) + jnp.log1p(jnp.exp(-jnp.abs(x)))


def _rope(x, cos, sin_signed, first_half):
    partner = jnp.where(first_half, pltpu.roll(x, LANES - HEAD_DIM // 2, 1), pltpu.roll(x, HEAD_DIM // 2, 1))
    return x * cos + partner * sin_signed


def _memory_kv_kernel(mem_ref, g_mem_ref, wk_ref, wv_ref, g_mk_ref, k_out, v_out):
    x = mem_ref[0]
    h = x * lax.rsqrt(jnp.mean(x * x, axis=-1, keepdims=True) + EPS) * g_mem_ref[...]
    hb = h.astype(BF16)
    k = _dot(hb, wk_ref[...])
    v_out[0] = _dot(hb, wv_ref[...])
    for hh in range(MEM_HEADS):
        sl = slice(hh * MEM_HEAD_DIM, (hh + 1) * MEM_HEAD_DIM)
        kh = k[:, sl]
        r = lax.rsqrt(jnp.mean(kh * kh, axis=-1, keepdims=True) + EPS)
        k_out[0, :, sl] = kh * r * g_mk_ref[...]


def _memory_kv(mem, g_mem, w_k, w_v, g_mk):
    b = mem.shape[0]
    full = lambda shape: pl.BlockSpec(shape, lambda i: (0,) * len(shape))
    return pl.pallas_call(
        _memory_kv_kernel,
        grid=(b,),
        in_specs=[
            pl.BlockSpec((1, MEM_TOKENS, D_MODEL), lambda i: (i, 0, 0)),
            full((1, D_MODEL)),
            full((D_MODEL, MEM_WIDTH)),
            full((D_MODEL, MEM_WIDTH)),
            full((1, MEM_HEAD_DIM)),
        ],
        out_specs=[
            pl.BlockSpec((1, MEM_TOKENS, MEM_WIDTH), lambda i: (i, 0, 0)),
            pl.BlockSpec((1, MEM_TOKENS, MEM_WIDTH), lambda i: (i, 0, 0)),
        ],
        out_shape=[jax.ShapeDtypeStruct((b, MEM_TOKENS, MEM_WIDTH), F32)] * 2,
        compiler_params=pltpu.CompilerParams(dimension_semantics=("arbitrary",)),
        name="memory_kv",
    )(mem, g_mem.reshape(1, D_MODEL), w_k.astype(BF16), w_v.astype(BF16), g_mk.reshape(1, MEM_HEAD_DIM))


def _layer_kernel(
        x_ref, cos_ref, sin_ref, memk_ref, memv_ref,
        wink_ref, winv_ref, ssm_ref, conv_ref,
        w_in_ref, wdt_ref, w_out_ref,
        g_norm_ref, gq_ref, gk_ref, sinks_ref, convw_ref, convb_ref,
        dtb_row_ref, dtb_col_ref, alog_exp_ref, alog_col_ref, dskip_ref, g_ssd_ref, g_mq_ref,
        segq_ref, segk_ref, expand_ref, tri_ref, duptri_ref,
        y_ref, newk_ref, newv_ref, ssm_out_ref, conv_out_ref,
        proj, qs, kwin, vwin, kr, vr, xbuf, xconv, dtexp, actd, state, yssd, mix, memk_bf, memv_bf,
        *, tile, n_tiles, has_past):
    T = tile
    n_chunks = T // CHUNK
    t = pl.program_id(1)

    @pl.when(t == 0)
    def _init():
        if has_past:
            kwin[0:WINDOW, :] = wink_ref[0]
            vwin[0:WINDOW, :] = winv_ref[0]
            state[...] = ssm_ref[0]
            xbuf[0:SUBLANES, :] = jnp.zeros((SUBLANES, CONV_CH), F32)
            xbuf[SUBLANES - (CONV_WIDTH - 1):SUBLANES, :] = conv_ref[0]
        else:
            kwin[0:WINDOW, :] = jnp.zeros((WINDOW, KV_WIDTH), F32)
            vwin[0:WINDOW, :] = jnp.zeros((WINDOW, KV_WIDTH), F32)
            state[...] = jnp.zeros((SSD_WIDTH, SSD_STATE), F32)
            xbuf[0:SUBLANES, :] = jnp.zeros((SUBLANES, CONV_CH), F32)
        memk_bf[...] = memk_ref[0].astype(BF16)
        memv_bf[...] = memv_ref[0].astype(BF16)

    if n_tiles > 1:
        @pl.when(t > 0)
        def _shift():
            kwin[0:WINDOW, :] = kwin[T:T + WINDOW, :]
            vwin[0:WINDOW, :] = vwin[T:T + WINDOW, :]
            xbuf[0:SUBLANES, :] = xbuf[T:T + SUBLANES, :]

    x = x_ref[0]
    h = x * lax.rsqrt(jnp.mean(x * x, axis=-1, keepdims=True) + EPS) * g_norm_ref[...]
    hb = h.astype(BF16)
    for c0 in range(0, P_COLS, PROJ_BLOCK):
        c1 = min(c0 + PROJ_BLOCK, P_COLS)
        proj[:, c0:c1] = _dot(hb, w_in_ref[:, c0:c1])
    dt_t = _dot_nt(wdt_ref[...], hb)

    lane = lax.broadcasted_iota(jnp.int32, (1, LANES), 1)
    first_half = (lane % HEAD_DIM) < (HEAD_DIM // 2)
    lo_half = lane < HEAD_DIM
    cos = cos_ref[...]
    sin_s = sin_ref[...]

    q = proj[:, C_Q:C_Q + ATTN_WIDTH]
    qn = q * lax.rsqrt(_seg_sumsq(q, segq_ref[...]) * (1.0 / HEAD_DIM) + EPS) * gq_ref[...]
    for j in range(ATTN_WIDTH // LANES):
        sl = slice(j * LANES, (j + 1) * LANES)
        qs[:, sl] = _rope(qn[:, sl], cos, sin_s, first_half).astype(BF16)
    k = proj[:, C_K:C_K + KV_WIDTH]
    kn = k * lax.rsqrt(_seg_sumsq(k, segk_ref[...]) * (1.0 / HEAD_DIM) + EPS) * gk_ref[...]
    kwin[WINDOW:WINDOW + T, :] = _rope(kn, cos, sin_s, first_half)
    vwin[WINDOW:WINDOW + T, :] = proj[:, C_V:C_V + KV_WIDTH]

    for src, dst in ((kwin, kr), (vwin, vr)):
        full = src[...]
        swapped = pltpu.roll(full, HEAD_DIM, 1)
        zero = jnp.zeros_like(full)
        dst[0] = jnp.where(lo_half, full, zero).astype(BF16)
        dst[1] = jnp.where(lo_half, zero, swapped).astype(BF16)
        dst[2] = jnp.where(lo_half, swapped, zero).astype(BF16)
        dst[3] = jnp.where(lo_half, zero, full).astype(BF16)

    xbuf[SUBLANES:SUBLANES + T, :] = proj[:, C_XBC:C_XBC + CONV_CH]
    acc = convb_ref[...]
    for j in range(CONV_WIDTH):
        off = SUBLANES - (CONV_WIDTH - 1) + j
        acc = acc + xbuf[off:off + T, :] * convw_ref[j:j + 1, :]
    xconv[...] = _silu(acc)

    dt_row = _softplus(proj[:, C_DT:C_DT + LANES] + dtb_row_ref[...])
    dtexp[...] = _dot_sel_rhs(dt_row, expand_ref[...])
    a_exp = -jnp.exp(alog_exp_ref[...])
    da_t = _softplus(dt_t + dtb_col_ref[...]) * (-jnp.exp(alog_col_ref[...]))
    acum_t = _dot_sel_rhs(da_t, duptri_ref[...])
    for cc in range(n_chunks):
        actd[cc] = acum_t[:, cc * LANES:(cc + 1) * LANES]

    row64 = lax.broadcasted_iota(jnp.int32, (CHUNK, LANES), 0)
    lane64 = lax.broadcasted_iota(jnp.int32, (CHUNK, LANES), 1)
    causal_dup = row64 >= (lane64 % CHUNK)
    lo_half64 = lane64 < CHUNK
    row128 = lax.broadcasted_iota(jnp.int32, (2 * CHUNK, 1), 0)
    key_idx = lax.broadcasted_iota(jnp.int32, (1, WINDOW + CHUNK), 1)
    tri = tri_ref[...]
    d_skip = dskip_ref[...]

    def chunk_body(c, carry):
        r0 = c * CHUNK if isinstance(c, int) else pl.multiple_of(c * CHUNK, CHUNK)
        rows = pl.ds(r0, CHUNK)
        krows = pl.ds(r0, WINDOW + CHUNK)

        if not has_past:
            first_valid = jnp.where(t == 0, WINDOW - r0, 0)
            key_ok = key_idx >= first_valid
        for g in range(N_KV_HEADS):
            base = g * 2 * LANES
            qg = jnp.concatenate([qs[rows, base:base + LANES], qs[rows, base + LANES:base + 2 * LANES]], axis=0)
            o_acc = jnp.zeros((2 * CHUNK, LANES), F32)
            for pos in range(2):
                hd0 = 4 * g + pos
                sink = jnp.where(row128 < CHUNK, sinks_ref[hd0], sinks_ref[hd0 + 2])
                s = _dot_nt(qg, kr[2 * g + pos, krows, :])
                if not has_past:
                    s = jnp.where(key_ok, s, NEG_BIG)
                m = jnp.maximum(jnp.max(s, axis=-1, keepdims=True), sink)
                e = jnp.exp(s - m)
                den = jnp.sum(e, axis=-1, keepdims=True) + jnp.exp(sink - m)
                p = (e * (1.0 / den)).astype(BF16)
                o_acc = o_acc + _dot(p, vr[2 * g + pos, krows, :])
            ga = proj[rows, C_GA + base:C_GA + base + 2 * LANES]
            mix[rows, base:base + LANES] = (o_acc[0:CHUNK] * _silu(ga[:, 0:LANES])).astype(BF16)
            mix[rows, base + LANES:base + 2 * LANES] = (o_acc[CHUNK:] * _silu(ga[:, LANES:])).astype(BF16)

        dtx = dtexp[rows, :]
        xs = xconv[rows, 0:SSD_WIDTH]
        xdt = xs * dtx
        acum = _dot_sel_lhs(tri, dtx * a_exp)
        a_last = acum[CHUNK - 1:CHUNK, :]
        x_end = (xdt * jnp.exp(a_last - acum)).astype(BF16)
        exp_acum = jnp.exp(acum)
        act = actd[c]
        dec_end = jnp.exp(jnp.broadcast_to(act[:, CHUNK - 1:CHUNK], (SSD_HEADS, LANES)))

        for g in range(SSD_GROUPS):
            bg = xconv[rows, SSD_WIDTH + g * SSD_STATE:SSD_WIDTH + (g + 1) * SSD_STATE].astype(BF16)
            cg = xconv[rows, SSD_WIDTH + SSD_GROUPS * SSD_STATE + g * SSD_STATE:
                       SSD_WIDTH + SSD_GROUPS * SSD_STATE + (g + 1) * SSD_STATE].astype(BF16)
            cb = _dot_nt(cg, jnp.concatenate([bg, bg], axis=0))
            gw = SSD_WIDTH // SSD_GROUPS
            st_g = state[g * gw:(g + 1) * gw, :]
            y_off = _dot_nt(cg, st_g.astype(BF16))
            new_st = _dot_tn(x_end[:, g * gw:(g + 1) * gw], bg)
            for hp in range(SSD_HEADS // SSD_GROUPS // 2):
                pair = g * (SSD_HEADS // SSD_GROUPS // 2) + hp
                cols = slice(pair * LANES, (pair + 1) * LANES)
                rowb = jnp.where(lane < CHUNK, act[2 * pair:2 * pair + 1, :], act[2 * pair + 1:2 * pair + 2, :])
                dec = jnp.exp(jnp.where(causal_dup, acum[:, cols] - rowb, -jnp.inf))
                w = (cb * dec).astype(BF16)
                xp = xdt[:, cols]
                zero = jnp.zeros_like(xp)
                xbd = jnp.concatenate([jnp.where(lo_half64, xp, zero), jnp.where(lo_half64, zero, xp)],
                                      axis=0).astype(BF16)
                lc = slice(hp * LANES, (hp + 1) * LANES)
                y = _dot(w, xbd) + y_off[:, lc] * exp_acum[:, cols] + d_skip[:, cols] * xs[:, cols]
                yssd[rows, cols] = y
                for hh in range(2):
                    hd = 2 * pair + hh
                    sr = slice(hd * SSD_HEAD_DIM, (hd + 1) * SSD_HEAD_DIM)
                    lr = slice((2 * hp + hh) * SSD_HEAD_DIM, (2 * hp + hh + 1) * SSD_HEAD_DIM)
                    state[sr, :] = state[sr, :] * dec_end[hd:hd + 1, :] + new_st[lr, :]
        return carry

    if n_chunks == 1:
        chunk_body(0, 0)
    else:
        lax.fori_loop(0, n_chunks, chunk_body, 0)

    z = proj[:, C_Z:C_Z + SSD_WIDTH]
    yg = yssd[...] * _silu(z)
    mix[:, ATTN_WIDTH:ATTN_WIDTH + SSD_WIDTH] = (
        yg * lax.rsqrt(jnp.mean(yg * yg, axis=-1, keepdims=True) + EPS) * g_ssd_ref[...]).astype(BF16)

    scale = 1.0 / math.sqrt(MEM_HEAD_DIM)
    for hh in range(MEM_HEADS):
        sl = slice(hh * MEM_HEAD_DIM, (hh + 1) * MEM_HEAD_DIM)
        mqh = proj[:, C_MQ + hh * MEM_HEAD_DIM:C_MQ + (hh + 1) * MEM_HEAD_DIM]
        qh = (mqh * lax.rsqrt(jnp.mean(mqh * mqh, axis=-1, keepdims=True) + EPS) * g_mq_ref[...]).astype(BF16)
        s = _dot_nt(qh, memk_bf[:, sl]) * scale
        m = jnp.max(s, axis=-1, keepdims=True)
        e = jnp.exp(s - m)
        p = (e * (1.0 / jnp.sum(e, axis=-1, keepdims=True))).astype(BF16)
        o = _dot(p, memv_bf[:, sl])
        gm = proj[:, C_GM + hh * MEM_HEAD_DIM:C_GM + (hh + 1) * MEM_HEAD_DIM]
        c0 = ATTN_WIDTH + SSD_WIDTH + hh * MEM_HEAD_DIM
        mix[:, c0:c0 + MEM_HEAD_DIM] = (o * _silu(gm)).astype(BF16)

    y_ref[0] = x + _dot(mix[...], w_out_ref[...])

    @pl.when(t == n_tiles - 1)
    def _final():
        newk_ref[0] = kwin[T:T + WINDOW, :]
        newv_ref[0] = vwin[T:T + WINDOW, :]
        ssm_out_ref[0] = state[...]
        conv_out_ref[0] = xbuf[T + SUBLANES - (CONV_WIDTH - 1):T + SUBLANES, :]


def _constants(tile):
    n_chunks = tile // CHUNK
    idx = np.arange
    segq = (idx(ATTN_WIDTH)[:, None] // HEAD_DIM == idx(ATTN_WIDTH)[None, :] // HEAD_DIM)
    segk = (idx(KV_WIDTH)[:, None] // HEAD_DIM == idx(KV_WIDTH)[None, :] // HEAD_DIM)
    expand = (idx(LANES)[:, None] == idx(SSD_WIDTH)[None, :] // SSD_HEAD_DIM)
    tri = idx(CHUNK)[:, None] >= idx(CHUNK)[None, :]
    src = idx(tile)[:, None]
    dst = idx(n_chunks * LANES)[None, :]
    duptri = (src // CHUNK == dst // LANES) & (src % CHUNK <= dst % CHUNK)
    return [jnp.asarray(m.astype(np.float32), dtype=BF16) for m in (segq, segk, expand, tri, duptri)]


def _layer(x, positions, mem_k, mem_v, win_k, win_v, ssm0, conv0, params, *, tile, has_past):
    b, s, _ = x.shape
    n_t = s // tile
    assert s % tile == 0 and tile % CHUNK == 0
    assert has_past or tile >= WINDOW
    assert not has_past or n_t == 1

    half = HEAD_DIM // 2
    freqs = ROPE_THETA ** (-jnp.arange(half, dtype=F32) / half)
    ang = positions.astype(F32)[:, None] * freqs[None, :]
    cos, sin = jnp.cos(ang), jnp.sin(ang)
    cos_t = jnp.tile(cos, (1, LANES // half))
    sin_t = jnp.tile(jnp.concatenate([-sin, sin], axis=1), (1, LANES // HEAD_DIM))

    consts = _constants(tile)

    def full(a):
        nd = a.ndim
        return pl.BlockSpec(a.shape, lambda i, j, _n=nd: (0,) * _n)

    def per_batch(a):
        nd = a.ndim
        return pl.BlockSpec((1,) + a.shape[1:], lambda i, j, _n=nd: (i,) + (0,) * (_n - 1))

    vec_params = [params[k] for k in ("g_norm", "gq", "gk")]
    tail_params = [params[k] for k in ("conv_w", "conv_b", "dtb_row", "dtb_col", "alog_exp", "alog_col",
                                       "d_skip", "g_ssd", "g_mq")]
    inputs = [x, cos_t, sin_t, mem_k, mem_v, win_k, win_v, ssm0, conv0,
              params["w_in"], params["w_dt"], params["w_out"],
              *vec_params, params["sinks"], *tail_params, *consts]
    in_specs = [
        pl.BlockSpec((1, tile, D_MODEL), lambda i, j: (i, j, 0)),
        pl.BlockSpec((tile, LANES), lambda i, j: (j, 0)),
        pl.BlockSpec((tile, LANES), lambda i, j: (j, 0)),
        per_batch(mem_k), per_batch(mem_v),
        per_batch(win_k), per_batch(win_v), per_batch(ssm0), per_batch(conv0),
        full(params["w_in"]), full(params["w_dt"]), full(params["w_out"]),
        *[full(p) for p in vec_params],
        pl.BlockSpec(memory_space=pltpu.SMEM),
        *[full(p) for p in tail_params],
        *[full(c) for c in consts],
    ]
    out_shape = [
        jax.ShapeDtypeStruct((b, s, D_MODEL), F32),
        jax.ShapeDtypeStruct((b, WINDOW, KV_WIDTH), F32),
        jax.ShapeDtypeStruct((b, WINDOW, KV_WIDTH), F32),
        jax.ShapeDtypeStruct((b, SSD_WIDTH, SSD_STATE), F32),
        jax.ShapeDtypeStruct((b, CONV_WIDTH - 1, CONV_CH), F32),
    ]
    out_specs = [
        pl.BlockSpec((1, tile, D_MODEL), lambda i, j: (i, j, 0)),
        pl.BlockSpec((1, WINDOW, KV_WIDTH), lambda i, j: (i, 0, 0)),
        pl.BlockSpec((1, WINDOW, KV_WIDTH), lambda i, j: (i, 0, 0)),
        pl.BlockSpec((1, SSD_WIDTH, SSD_STATE), lambda i, j: (i, 0, 0)),
        pl.BlockSpec((1, CONV_WIDTH - 1, CONV_CH), lambda i, j: (i, 0, 0)),
    ]
    n_chunks = tile // CHUNK
    scratch = [
        pltpu.VMEM((tile, P_COLS), F32),
        pltpu.VMEM((tile, ATTN_WIDTH), BF16),
        pltpu.VMEM((WINDOW + tile, KV_WIDTH), F32),
        pltpu.VMEM((WINDOW + tile, KV_WIDTH), F32),
        pltpu.VMEM((4, WINDOW + tile, KV_WIDTH), BF16),
        pltpu.VMEM((4, WINDOW + tile, KV_WIDTH), BF16),
        pltpu.VMEM((tile + SUBLANES, CONV_CH), F32),
        pltpu.VMEM((tile, CONV_CH), F32),
        pltpu.VMEM((tile, SSD_WIDTH), F32),
        pltpu.VMEM((n_chunks, SSD_HEADS, LANES), F32),
        pltpu.VMEM((SSD_WIDTH, SSD_STATE), F32),
        pltpu.VMEM((tile, SSD_WIDTH), F32),
        pltpu.VMEM((tile, MIX_WIDTH), BF16),
        pltpu.VMEM((MEM_TOKENS, MEM_WIDTH), BF16),
        pltpu.VMEM((MEM_TOKENS, MEM_WIDTH), BF16),
    ]
    return pl.pallas_call(
        functools.partial(_layer_kernel, tile=tile, n_tiles=n_t, has_past=has_past),
        grid=(b, n_t),
        in_specs=in_specs,
        out_specs=out_specs,
        out_shape=out_shape,
        scratch_shapes=scratch,
        compiler_params=pltpu.CompilerParams(
            dimension_semantics=("arbitrary", "arbitrary"),
            vmem_limit_bytes=VMEM_LIMIT_BYTES),
        name="layer_sample" if has_past else "layer_prompt",
    )(*inputs)


def _prep_params(g_norm, w_in, g_q, g_k, sinks, conv_w, conv_b, dt_bias, a_log, d_skip, g_ssd, g_mq, w_out):
    w_dt = w_in[:, ORIG_DT:ORIG_DT + SSD_HEADS]
    w_packed = jnp.concatenate(
        [w_in[:, :ORIG_DT], w_in[:, ORIG_DT + SSD_HEADS:], w_dt,
         jnp.zeros((D_MODEL, LANES - SSD_HEADS), w_in.dtype)], axis=1).astype(BF16)
    pad = LANES - SSD_HEADS
    return {
        "w_in": w_packed,
        "w_dt": w_dt.T.astype(BF16),
        "w_out": w_out.astype(BF16),
        "g_norm": g_norm.reshape(1, D_MODEL),
        "gq": (jnp.tile(g_q, N_Q_HEADS) * (1.0 / math.sqrt(HEAD_DIM))).reshape(1, ATTN_WIDTH),
        "gk": jnp.tile(g_k, N_KV_HEADS).reshape(1, KV_WIDTH),
        "sinks": sinks,
        "conv_w": conv_w,
        "conv_b": conv_b.reshape(1, CONV_CH),
        "dtb_row": jnp.pad(dt_bias, (0, pad)).reshape(1, LANES),
        "dtb_col": dt_bias.reshape(SSD_HEADS, 1),
        "alog_exp": jnp.repeat(a_log, SSD_HEAD_DIM).reshape(1, SSD_WIDTH),
        "alog_col": a_log.reshape(SSD_HEADS, 1),
        "d_skip": jnp.repeat(d_skip, SSD_HEAD_DIM).reshape(1, SSD_WIDTH),
        "g_ssd": g_ssd.reshape(1, SSD_WIDTH),
        "g_mq": g_mq.reshape(1, MEM_HEAD_DIM),
    }


PROMPT_TILE = 256


def kernel(x_prompt, x_sample, cache_win_k, cache_win_v, state_ssm, state_conv, cache_mem_k, cache_mem_v,
           mem_prompt, g_norm, w_in, g_q, g_k, sinks, conv_w, conv_b, dt_bias, a_log, d_skip, g_ssd,
           g_mem, w_mem_k, w_mem_v, g_mk, g_mq, w_out):
    depth = w_in.shape[0]
    assert depth == 1
    l = 0
    bp, t_p, _ = x_prompt.shape
    bs, t_s, _ = x_sample.shape
    params = _prep_params(g_norm[l], w_in[l], g_q[l], g_k[l], sinks[l], conv_w[l], conv_b[l], dt_bias[l],
                          a_log[l], d_skip[l], g_ssd[l], g_mq[l], w_out[l])

    mk_p, mv_p = _memory_kv(mem_prompt, g_mem[l], w_mem_k[l], w_mem_v[l], g_mk[l])

    zeros_k = jnp.zeros((bp, WINDOW, KV_WIDTH), F32)
    zeros_ssm = jnp.zeros((bp, SSD_WIDTH, SSD_STATE), F32)
    zeros_conv = jnp.zeros((bp, CONV_WIDTH - 1, CONV_CH), F32)
    tile_p = min(PROMPT_TILE, t_p)
    yp, kp, vp, sp, cp = _layer(
        x_prompt, jnp.arange(t_p, dtype=jnp.int32), mk_p, mv_p, zeros_k, zeros_k, zeros_ssm, zeros_conv,
        params, tile=tile_p, has_past=False)

    ys, ks, vs, ss, cs = _layer(
        x_sample, PAST_LEN + jnp.arange(t_s, dtype=jnp.int32),
        cache_mem_k[l].reshape(bs, MEM_TOKENS, MEM_WIDTH), cache_mem_v[l].reshape(bs, MEM_TOKENS, MEM_WIDTH),
        cache_win_k[l].reshape(bs, WINDOW, KV_WIDTH), cache_win_v[l].reshape(bs, WINDOW, KV_WIDTH),
        state_ssm[l].reshape(bs, SSD_WIDTH, SSD_STATE), state_conv[l],
        params, tile=t_s, has_past=True)

    kv5 = lambda a, b: a.reshape(1, b, WINDOW, N_KV_HEADS, HEAD_DIM)
    ssm5 = lambda a, b: a.reshape(1, b, SSD_HEADS, SSD_HEAD_DIM, SSD_STATE)
    mem5 = lambda a: a.reshape(1, bp, MEM_TOKENS, MEM_HEADS, MEM_HEAD_DIM)
    return (yp, ys,
            kv5(kp, bp), kv5(vp, bp), ssm5(sp, bp), cp[None],
            mem5(mk_p), mem5(mv_p),
            kv5(ks, bs), kv5(vs, bs), ssm5(ss, bs), cs[None])
```

```python
import functools
import math

import numpy as np
import jax
import jax.numpy as jnp
from jax import lax
from jax.experimental import pallas as pl
from jax.experimental.pallas import tpu as pltpu

D_MODEL = 1024
CHUNK = 64
HEAD_DIM = 64
N_Q_HEADS = 8
N_KV_HEADS = 2
ATTN_WIDTH = N_Q_HEADS * HEAD_DIM
KV_WIDTH = N_KV_HEADS * HEAD_DIM
WINDOW = 128
ROPE_THETA = 10000.0
SSD_WIDTH = D_MODEL
SSD_HEAD_DIM = 64
SSD_HEADS = SSD_WIDTH // SSD_HEAD_DIM
SSD_GROUPS = 2
SSD_STATE = 128
CONV_WIDTH = 4
CONV_CH = SSD_WIDTH + 2 * SSD_GROUPS * SSD_STATE
MEM_TOKENS = 256
MEM_HEADS = 4
MEM_HEAD_DIM = 128
MEM_WIDTH = MEM_HEADS * MEM_HEAD_DIM
MIX_WIDTH = 2 * D_MODEL
EPS = 1e-6
PAST_LEN = 2048

LANES = 128
SUBLANES = 8
VMEM_LIMIT_BYTES = 56 * 1024 * 1024

N_GROUPS = 4
GROUP_W = 1280
G0_Q, G0_K, G0_V, G0_GA = 0, ATTN_WIDTH, ATTN_WIDTH + KV_WIDTH, ATTN_WIDTH + 2 * KV_WIDTH
G1_Z, G1_DT = 0, SSD_WIDTH
G2_X, G2_B = 0, SSD_WIDTH
G3_C, G3_MQ, G3_GM = 0, SSD_GROUPS * SSD_STATE, SSD_GROUPS * SSD_STATE + MEM_WIDTH
PIECE_W = 256
N_PIECES = GROUP_W // PIECE_W
HEAD_ROWS = SUBLANES

F32 = jnp.float32
BF16 = jnp.bfloat16
NEG_BIG = -1e30


def _dot(a, b):
    return jnp.dot(a, b, preferred_element_type=F32)


def _dot_nt(a, b):
    return lax.dot_general(a, b, (((1,), (1,)), ((), ())), preferred_element_type=F32)


def _dot_tn(a, b):
    return lax.dot_general(a, b, (((0,), (0,)), ((), ())), preferred_element_type=F32)


def _split3(x):
    h1 = x.astype(BF16)
    r1 = x - h1.astype(F32)
    h2 = r1.astype(BF16)
    r2 = r1 - h2.astype(F32)
    return h1, h2, r2.astype(BF16)


def _dot_sel_rhs(x, sel):
    a, b, c = _split3(x)
    return (_dot(c, sel) + _dot(b, sel)) + _dot(a, sel)


def _seg_sumsq(x, seg):
    x2 = x * x
    hi = x2.astype(BF16)
    lo = (x2 - hi.astype(F32)).astype(BF16)
    return _dot(lo, seg) + _dot(hi, seg)


def _silu(x):
    return x / (1.0 + jnp.exp(-x))


def _softplus(x):
    return jnp.maximum(x, 0.0) + jnp.log1p(jnp.exp(-jnp.abs(x)))


def _rms_scale(x):
    return lax.rsqrt(jnp.mean(x * x, axis=-1, keepdims=True) + EPS)


def _rope(x, cos, sin_signed, first_half):
    partner = jnp.where(first_half, pltpu.roll(x, LANES - HEAD_DIM // 2, 1), pltpu.roll(x, HEAD_DIM // 2, 1))
    return x * cos + partner * sin_signed


def _memory_kv_kernel(mem_ref, g_mem_ref, wk_ref, wv_ref, g_mk_ref, k_out, v_out):
    x = mem_ref[0]
    hb = (x * _rms_scale(x) * g_mem_ref[...]).astype(BF16)
    k = _dot(hb, wk_ref[...])
    v_out[0] = _dot(hb, wv_ref[...])
    for hh in range(MEM_HEADS):
        sl = slice(hh * MEM_HEAD_DIM, (hh + 1) * MEM_HEAD_DIM)
        kh = k[:, sl]
        k_out[0, :, sl] = kh * _rms_scale(kh) * g_mk_ref[...]


def _memory_kv(mem, g_mem, w_k, w_v, g_mk):
    b = mem.shape[0]
    full = lambda shape: pl.BlockSpec(shape, lambda i: (0,) * len(shape))
    return pl.pallas_call(
        _memory_kv_kernel,
        grid=(b,),
        in_specs=[
            pl.BlockSpec((1, MEM_TOKENS, D_MODEL), lambda i: (i, 0, 0)),
            full((1, D_MODEL)),
            full((D_MODEL, MEM_WIDTH)),
            full((D_MODEL, MEM_WIDTH)),
            full((1, MEM_HEAD_DIM)),
        ],
        out_specs=[
            pl.BlockSpec((1, MEM_TOKENS, MEM_WIDTH), lambda i: (i, 0, 0)),
            pl.BlockSpec((1, MEM_TOKENS, MEM_WIDTH), lambda i: (i, 0, 0)),
        ],
        out_shape=[jax.ShapeDtypeStruct((b, MEM_TOKENS, MEM_WIDTH), F32)] * 2,
        compiler_params=pltpu.CompilerParams(dimension_semantics=("arbitrary",)),
        name="memory_kv",
    )(mem, g_mem.reshape(1, D_MODEL), w_k.astype(BF16), w_v.astype(BF16), g_mk.reshape(1, MEM_HEAD_DIM))


def _layer_kernel(*refs, tile, n_tiles, has_past):
    if has_past:
        (x_first_ref, x_next_ref, cos_ref, sin_ref, memk_ref, memv_ref,
         wink_ref, winv_ref, ssm_ref, conv_ref, *rest) = refs
    else:
        x_first_ref, x_next_ref, cos_ref, sin_ref, memk_ref, memv_ref, *rest = refs
        wink_ref = winv_ref = ssm_ref = conv_ref = None
    (w_in_ref, wdt_ref, w_out_ref,
     g_norm_ref, gq_ref, gk_ref, sinks_ref, convw_ref, convb_ref,
     dtb_row_ref, dtb_col_ref, alog_exp_ref, alog_col_ref, dskip_ref, g_ssd_ref, g_mq_ref,
     segq_ref, segk_ref, expand_ref, tri_ref, duptri_ref,
     y_ref, newk_ref, newv_ref, ssm_out_ref, conv_out_ref,
     proj_a, proj_b, hb_a, hb_b, xres_a, xres_b,
     qs, kwin, vwin, kr, vr, actd, state_t, mix, memk_bf, memv_bf) = rest

    T = tile
    n_chunks = T // CHUNK
    s = pl.program_id(0)
    t = s % n_tiles
    body_rows = pl.ds(HEAD_ROWS, T)

    def normed_input(x_ref, hb, xres):
        x = x_ref[0]
        xres[...] = x
        hb[...] = (x * _rms_scale(x) * g_norm_ref[...]).astype(BF16)

    def project_piece(hb, proj, g, i):
        cols = slice(i * PIECE_W, (i + 1) * PIECE_W)
        proj[g, body_rows, cols] = _dot(hb[...], w_in_ref[g, :, cols])

    def project_group(hb, proj, g):
        for i in range(N_PIECES):
            project_piece(hb, proj, g, i)

    @pl.when(s == 0)
    def _prologue():
        normed_input(x_first_ref, hb_a, xres_a)
        for g in range(N_GROUPS):
            project_group(hb_a, proj_a, g)

    lane = lax.broadcasted_iota(jnp.int32, (1, LANES), 1)
    first_half = (lane % HEAD_DIM) < (HEAD_DIM // 2)
    lo_half = lane < HEAD_DIM
    row64 = lax.broadcasted_iota(jnp.int32, (CHUNK, LANES), 0)
    lane64 = lax.broadcasted_iota(jnp.int32, (CHUNK, LANES), 1)
    causal_dup = row64 >= (lane64 % CHUNK)
    lo_half64 = lane64 < CHUNK
    row128 = lax.broadcasted_iota(jnp.int32, (2 * CHUNK, 1), 0)
    key_idx = lax.broadcasted_iota(jnp.int32, (1, WINDOW + CHUNK), 1)
    gw = SSD_WIDTH // SSD_GROUPS
    n_pairs = SSD_HEADS // 2
    pairs_per_group = n_pairs // SSD_GROUPS
    mem_scale = 1.0 / math.sqrt(MEM_HEAD_DIM)

    def build_masked(src, dst, rows):
        full = src[rows, :]
        swapped = pltpu.roll(full, HEAD_DIM, 1)
        zero = jnp.zeros_like(full)
        dst[0, rows, :] = jnp.where(lo_half, full, zero).astype(BF16)
        dst[1, rows, :] = jnp.where(lo_half, zero, swapped).astype(BF16)
        dst[2, rows, :] = jnp.where(lo_half, swapped, zero).astype(BF16)
        dst[3, rows, :] = jnp.where(lo_half, zero, full).astype(BF16)

    def step(proj, hb, xres, proj_nxt, hb_nxt, xres_nxt):
        @pl.when(t == 0)
        def _init():
            zero_head = jnp.zeros((HEAD_ROWS, GROUP_W), F32)
            proj[2, 0:HEAD_ROWS, :] = zero_head
            proj[3, 0:HEAD_ROWS, :] = zero_head
            if has_past:
                kwin[0:WINDOW, :] = wink_ref[0]
                vwin[0:WINDOW, :] = winv_ref[0]
                state_t[...] = ssm_ref[0].T
                tail = conv_ref[0]
                r_lo = HEAD_ROWS - (CONV_WIDTH - 1)
                proj[2, r_lo:HEAD_ROWS, :] = tail[:, 0:GROUP_W]
                proj[3, r_lo:HEAD_ROWS, 0:CONV_CH - GROUP_W] = tail[:, GROUP_W:CONV_CH]
                for src, dst in ((kwin, kr), (vwin, vr)):
                    build_masked(src, dst, slice(0, WINDOW))
            else:
                state_t[...] = jnp.zeros((SSD_STATE, SSD_WIDTH), F32)
                kwin[0:WINDOW, :] = jnp.zeros((WINDOW, KV_WIDTH), F32)
                vwin[0:WINDOW, :] = jnp.zeros((WINDOW, KV_WIDTH), F32)
                for v in range(4):
                    kr[v, 0:WINDOW, :] = jnp.zeros((WINDOW, KV_WIDTH), BF16)
                    vr[v, 0:WINDOW, :] = jnp.zeros((WINDOW, KV_WIDTH), BF16)
            memk_bf[...] = memk_ref[0].astype(BF16)
            memv_bf[...] = memv_ref[0].astype(BF16)

        if n_tiles > 1:
            @pl.when(t > 0)
            def _shift():
                kwin[0:WINDOW, :] = kwin[T:T + WINDOW, :]
                vwin[0:WINDOW, :] = vwin[T:T + WINDOW, :]
                for v in range(4):
                    kr[v, 0:WINDOW, :] = kr[v, T:T + WINDOW, :]
                    vr[v, 0:WINDOW, :] = vr[v, T:T + WINDOW, :]
                proj[2, 0:HEAD_ROWS, :] = proj_nxt[2, T:T + HEAD_ROWS, :]
                proj[3, 0:HEAD_ROWS, :] = proj_nxt[3, T:T + HEAD_ROWS, :]

        normed_input(x_next_ref, hb_nxt, xres_nxt)

        dt_t = _dot_nt(wdt_ref[...], hb[...])
        da_t = _softplus(dt_t + dtb_col_ref[...]) * (-jnp.exp(alog_col_ref[...]))
        acum_t = _dot_sel_rhs(da_t, duptri_ref[...])
        for cc in range(n_chunks):
            actd[cc] = acum_t[:, cc * LANES:(cc + 1) * LANES]

        a_exp = -jnp.exp(alog_exp_ref[...])
        tri = tri_ref[...]
        d_skip = dskip_ref[...]

        def chunk_body(c, carry):
            static = isinstance(c, int)
            r0 = c * CHUNK if static else pl.multiple_of(c * CHUNK, CHUNK)
            rows = pl.ds(r0, CHUNK)
            prow = pl.ds(r0 + HEAD_ROWS, CHUNK)
            wrow = pl.ds(r0 + WINDOW, CHUNK)
            krows = pl.ds(r0, WINDOW + CHUNK)

            project_piece(hb_nxt, proj_nxt, c, 0)

            q = proj[0, prow, G0_Q:G0_Q + ATTN_WIDTH]
            k = proj[0, prow, G0_K:G0_K + KV_WIDTH]
            dt_row = _softplus(proj[1, prow, G1_DT:G1_DT + LANES] + dtb_row_ref[...])
            mem_q = []
            for hh in range(MEM_HEADS):
                mqh = proj[3, prow, G3_MQ + hh * MEM_HEAD_DIM:G3_MQ + (hh + 1) * MEM_HEAD_DIM]
                mem_q.append((mqh * _rms_scale(mqh) * g_mq_ref[...]).astype(BF16))

            seg = segq_ref[...]
            q_blocks = [q[:, j * 2 * LANES:(j + 1) * 2 * LANES] for j in range(ATTN_WIDTH // (2 * LANES))]
            q_ss = [_seg_sumsq(qj, seg) for qj in q_blocks]
            k_ss = _seg_sumsq(k, segk_ref[...])
            dtx = _dot_sel_rhs(dt_row, expand_ref[...])
            mem_sc = [_dot_nt(mem_q[hh], memk_bf[:, hh * MEM_HEAD_DIM:(hh + 1) * MEM_HEAD_DIM])
                      for hh in range(MEM_HEADS)]

            project_piece(hb_nxt, proj_nxt, c, 1)

            def conv(group, c0, width, ch0):
                acc = convb_ref[:, ch0:ch0 + width]
                blk = proj[group, pl.ds(r0, HEAD_ROWS + CHUNK), c0:c0 + width]
                for j in range(CONV_WIDTH):
                    off = HEAD_ROWS - (CONV_WIDTH - 1) + j
                    acc = acc + blk[off:off + CHUNK, :] * convw_ref[j:j + 1, ch0:ch0 + width]
                return _silu(acc)

            b_all = conv(2, G2_B, SSD_GROUPS * SSD_STATE, SSD_WIDTH).astype(BF16)
            c_all = conv(3, G3_C, SSD_GROUPS * SSD_STATE, SSD_WIDTH + SSD_GROUPS * SSD_STATE).astype(BF16)
            xs = conv(2, G2_X, SSD_WIDTH, 0)

            cos = cos_ref[rows, :]
            sin_s = sin_ref[rows, :]
            for j, (qj, ss) in enumerate(zip(q_blocks, q_ss)):
                qn = qj * lax.rsqrt(ss * (1.0 / HEAD_DIM) + EPS) * gq_ref[:, j * 2 * LANES:(j + 1) * 2 * LANES]
                for jj in range(2):
                    c0 = j * 2 * LANES + jj * LANES
                    qs[rows, c0:c0 + LANES] = _rope(qn[:, jj * LANES:(jj + 1) * LANES], cos, sin_s,
                                                    first_half).astype(BF16)
            kn = k * lax.rsqrt(k_ss * (1.0 / HEAD_DIM) + EPS) * gk_ref[...]
            kwin[wrow, :] = _rope(kn, cos, sin_s, first_half)
            vwin[wrow, :] = proj[0, prow, G0_V:G0_V + KV_WIDTH]
            build_masked(kwin, kr, wrow)
            build_masked(vwin, vr, wrow)

            mem_p = []
            for hh in range(MEM_HEADS):
                sc = mem_sc[hh] * mem_scale
                m = jnp.max(sc, axis=-1, keepdims=True)
                e = jnp.exp(sc - m)
                mem_p.append((e * (1.0 / jnp.sum(e, axis=-1, keepdims=True))).astype(BF16))

            xdt = xs * dtx
            da3 = _split3(dtx * a_exp)

            cbs, y_offs = [], []
            for g in range(SSD_GROUPS):
                bg = b_all[:, g * SSD_STATE:(g + 1) * SSD_STATE]
                cg = c_all[:, g * SSD_STATE:(g + 1) * SSD_STATE]
                cbs.append(_dot_nt(cg, jnp.concatenate([bg, bg], axis=0)))
                y_offs.append(_dot(cg, state_t[:, g * gw:(g + 1) * gw].astype(BF16)))
            acum = (_dot(tri, da3[2]) + _dot(tri, da3[1])) + _dot(tri, da3[0])
            att_sc = []
            for g in range(N_KV_HEADS):
                base = g * 2 * LANES
                qg = jnp.concatenate([qs[rows, base:base + LANES], qs[rows, base + LANES:base + 2 * LANES]], axis=0)
                for pos in range(2):
                    att_sc.append(_dot_nt(qg, kr[2 * g + pos, krows, :]))
            mem_o = [_dot(mem_p[hh], memv_bf[:, hh * MEM_HEAD_DIM:(hh + 1) * MEM_HEAD_DIM])
                     for hh in range(MEM_HEADS)]

            project_piece(hb_nxt, proj_nxt, c, 2)

            if not has_past:
                first_valid = jnp.where(t == 0, WINDOW - r0, 0)
                key_ok = key_idx >= first_valid
            att_p = []
            for g in range(N_KV_HEADS):
                for pos in range(2):
                    hd0 = 4 * g + pos
                    sink = jnp.where(row128 < CHUNK, sinks_ref[hd0], sinks_ref[hd0 + 2])
                    sc = att_sc[2 * g + pos]
                    if not has_past:
                        sc = jnp.where(key_ok, sc, NEG_BIG)
                    m = jnp.maximum(jnp.max(sc, axis=-1, keepdims=True), sink)
                    e = jnp.exp(sc - m)
                    den = jnp.sum(e, axis=-1, keepdims=True) + jnp.exp(sink - m)
                    att_p.append((e * (1.0 / den)).astype(BF16))

            a_last = acum[CHUNK - 1:CHUNK, :]
            x_end = (xdt * jnp.exp(a_last - acum)).astype(BF16)
            exp_acum = jnp.exp(acum)
            dec_end = jnp.exp(a_last)
            act = actd[c]
            ws, xbds = [], []
            for pair in range(n_pairs):
                g = pair // pairs_per_group
                cols = slice(pair * LANES, (pair + 1) * LANES)
                rowb = jnp.where(lane < CHUNK, act[2 * pair:2 * pair + 1, :], act[2 * pair + 1:2 * pair + 2, :])
                dec = jnp.exp(jnp.where(causal_dup, acum[:, cols] - rowb, -jnp.inf))
                ws.append((cbs[g] * dec).astype(BF16))
                xp = xdt[:, cols]
                zero = jnp.zeros_like(xp)
                xbds.append(jnp.concatenate([jnp.where(lo_half64, xp, zero), jnp.where(lo_half64, zero, xp)],
                                            axis=0).astype(BF16))

            for hh in range(MEM_HEADS):
                gm = proj[3, prow, G3_GM + hh * MEM_HEAD_DIM:G3_GM + (hh + 1) * MEM_HEAD_DIM]
                c0 = ATTN_WIDTH + SSD_WIDTH + hh * MEM_HEAD_DIM
                mix[rows, c0:c0 + MEM_HEAD_DIM] = (mem_o[hh] * _silu(gm)).astype(BF16)

            att_o = []
            for g in range(N_KV_HEADS):
                att_o.append(_dot(att_p[2 * g], vr[2 * g, krows, :]) + _dot(att_p[2 * g + 1], vr[2 * g + 1, krows, :]))
            for g in range(SSD_GROUPS):
                gcols = slice(g * gw, (g + 1) * gw)
                bg = b_all[:, g * SSD_STATE:(g + 1) * SSD_STATE]
                state_t[:, gcols] = state_t[:, gcols] * dec_end[:, gcols] + _dot_tn(bg, x_end[:, gcols])
            y_diag = [_dot(ws[pair], xbds[pair]) for pair in range(n_pairs)]

            project_piece(hb_nxt, proj_nxt, c, 3)
            project_piece(hb_nxt, proj_nxt, c, 4)

            for g in range(N_KV_HEADS):
                base = g * 2 * LANES
                ga = proj[0, prow, G0_GA + base:G0_GA + base + 2 * LANES]
                mix[rows, base:base + LANES] = (att_o[g][0:CHUNK] * _silu(ga[:, 0:LANES])).astype(BF16)
                mix[rows, base + LANES:base + 2 * LANES] = (att_o[g][CHUNK:] * _silu(ga[:, LANES:])).astype(BF16)

            ys = []
            for pair in range(n_pairs):
                g, hp = divmod(pair, pairs_per_group)
                cols = slice(pair * LANES, (pair + 1) * LANES)
                lc = slice(hp * LANES, (hp + 1) * LANES)
                ys.append(y_diag[pair] + y_offs[g][:, lc] * exp_acum[:, cols] + d_skip[:, cols] * xs[:, cols])
            z = proj[1, prow, G1_Z:G1_Z + SSD_WIDTH]
            yg = jnp.concatenate(ys, axis=1) * _silu(z)
            mix[rows, ATTN_WIDTH:ATTN_WIDTH + SSD_WIDTH] = (yg * _rms_scale(yg) * g_ssd_ref[...]).astype(BF16)
            return carry

        if n_chunks == N_GROUPS:
            lax.fori_loop(0, n_chunks, chunk_body, 0)
        else:
            for cc in range(n_chunks):
                chunk_body(cc, 0)
            for g in range(n_chunks, N_GROUPS):
                project_group(hb_nxt, proj_nxt, g)

        y_ref[0] = xres[...] + _dot(mix[...], w_out_ref[...])

        @pl.when(t == n_tiles - 1)
        def _final():
            newk_ref[0] = kwin[T:T + WINDOW, :]
            newv_ref[0] = vwin[T:T + WINDOW, :]
            ssm_out_ref[0] = state_t[...].T
            r_lo = T + HEAD_ROWS - (CONV_WIDTH - 1)
            conv_out_ref[0, :, 0:GROUP_W] = proj[2, r_lo:T + HEAD_ROWS, :]
            conv_out_ref[0, :, GROUP_W:CONV_CH] = proj[3, r_lo:T + HEAD_ROWS, 0:CONV_CH - GROUP_W]

    parity = s % 2

    @pl.when(parity == 0)
    def _even():
        step(proj_a, hb_a, xres_a, proj_b, hb_b, xres_b)

    @pl.when(parity == 1)
    def _odd():
        step(proj_b, hb_b, xres_b, proj_a, hb_a, xres_a)


def _constants(tile):
    n_chunks = tile // CHUNK
    idx = np.arange
    segq = (idx(2 * LANES)[:, None] // HEAD_DIM == idx(2 * LANES)[None, :] // HEAD_DIM)
    segk = (idx(KV_WIDTH)[:, None] // HEAD_DIM == idx(KV_WIDTH)[None, :] // HEAD_DIM)
    expand = (idx(LANES)[:, None] == idx(SSD_WIDTH)[None, :] // SSD_HEAD_DIM)
    tri = idx(CHUNK)[:, None] >= idx(CHUNK)[None, :]
    src = idx(tile)[:, None]
    dst = idx(n_chunks * LANES)[None, :]
    duptri = (src // CHUNK == dst // LANES) & (src % CHUNK <= dst % CHUNK)
    return [jnp.asarray(m.astype(np.float32), dtype=BF16) for m in (segq, segk, expand, tri, duptri)]


def _layer(x, positions, mem_k, mem_v, past, params, *, tile):
    b, seq, _ = x.shape
    n_t = seq // tile
    n_steps = b * n_t
    has_past = past is not None
    assert seq % tile == 0 and tile % CHUNK == 0 and tile // CHUNK <= N_GROUPS
    assert n_t == 1 or tile >= WINDOW

    half = HEAD_DIM // 2
    freqs = ROPE_THETA ** (-jnp.arange(half, dtype=F32) / half)
    ang = positions.astype(F32)[:, None] * freqs[None, :]
    cos, sin = jnp.cos(ang), jnp.sin(ang)
    cos_t = jnp.tile(cos, (1, LANES // half))
    sin_t = jnp.tile(jnp.concatenate([-sin, sin], axis=1), (1, LANES // HEAD_DIM))

    consts = _constants(tile)

    def full(a):
        nd = a.ndim
        return pl.BlockSpec(a.shape, lambda i, _n=nd: (0,) * _n, pipeline_mode=pl.Buffered(1))

    def per_row(a):
        nd = a.ndim
        return pl.BlockSpec((1,) + a.shape[1:], lambda i, _n=nd: (i // n_t,) + (0,) * (_n - 1))

    def next_tile(i):
        j = jnp.minimum(i + 1, n_steps - 1)
        return (j // n_t, j % n_t, 0)

    vec_params = [params[k] for k in ("g_norm", "gq", "gk")]
    tail_params = [params[k] for k in ("conv_w", "conv_b", "dtb_row", "dtb_col", "alog_exp", "alog_col",
                                       "d_skip", "g_ssd", "g_mq")]
    past_inputs = list(past) if has_past else []
    inputs = [x, x, cos_t, sin_t, mem_k, mem_v, *past_inputs,
              params["w_in"], params["w_dt"], params["w_out"],
              *vec_params, params["sinks"], *tail_params, *consts]
    in_specs = [
        pl.BlockSpec((1, tile, D_MODEL), lambda i: (0, 0, 0)),
        pl.BlockSpec((1, tile, D_MODEL), next_tile),
        pl.BlockSpec((tile, LANES), lambda i: (i % n_t, 0)),
        pl.BlockSpec((tile, LANES), lambda i: (i % n_t, 0)),
        per_row(mem_k), per_row(mem_v),
        *[per_row(p) for p in past_inputs],
        full(params["w_in"]), full(params["w_dt"]), full(params["w_out"]),
        *[full(p) for p in vec_params],
        pl.BlockSpec(memory_space=pltpu.SMEM),
        *[full(p) for p in tail_params],
        *[full(c) for c in consts],
    ]
    out_shape = [
        jax.ShapeDtypeStruct((b, seq, D_MODEL), F32),
        jax.ShapeDtypeStruct((b, WINDOW, KV_WIDTH), F32),
        jax.ShapeDtypeStruct((b, WINDOW, KV_WIDTH), F32),
        jax.ShapeDtypeStruct((b, SSD_WIDTH, SSD_STATE), F32),
        jax.ShapeDtypeStruct((b, CONV_WIDTH - 1, CONV_CH), F32),
    ]
    out_specs = [
        pl.BlockSpec((1, tile, D_MODEL), lambda i: (i // n_t, i % n_t, 0)),
        pl.BlockSpec((1, WINDOW, KV_WIDTH), lambda i: (i // n_t, 0, 0)),
        pl.BlockSpec((1, WINDOW, KV_WIDTH), lambda i: (i // n_t, 0, 0)),
        pl.BlockSpec((1, SSD_WIDTH, SSD_STATE), lambda i: (i // n_t, 0, 0)),
        pl.BlockSpec((1, CONV_WIDTH - 1, CONV_CH), lambda i: (i // n_t, 0, 0)),
    ]
    n_chunks = tile // CHUNK
    proj_shape = (N_GROUPS, HEAD_ROWS + tile, GROUP_W)
    scratch = [
        pltpu.VMEM(proj_shape, F32), pltpu.VMEM(proj_shape, F32),
        pltpu.VMEM((tile, D_MODEL), BF16), pltpu.VMEM((tile, D_MODEL), BF16),
        pltpu.VMEM((tile, D_MODEL), F32), pltpu.VMEM((tile, D_MODEL), F32),
        pltpu.VMEM((tile, ATTN_WIDTH), BF16),
        pltpu.VMEM((WINDOW + tile, KV_WIDTH), F32),
        pltpu.VMEM((WINDOW + tile, KV_WIDTH), F32),
        pltpu.VMEM((4, WINDOW + tile, KV_WIDTH), BF16),
        pltpu.VMEM((4, WINDOW + tile, KV_WIDTH), BF16),
        pltpu.VMEM((n_chunks, SSD_HEADS, LANES), F32),
        pltpu.VMEM((SSD_STATE, SSD_WIDTH), F32),
        pltpu.VMEM((tile, MIX_WIDTH), BF16),
        pltpu.VMEM((MEM_TOKENS, MEM_WIDTH), BF16),
        pltpu.VMEM((MEM_TOKENS, MEM_WIDTH), BF16),
    ]
    return pl.pallas_call(
        functools.partial(_layer_kernel, tile=tile, n_tiles=n_t, has_past=has_past),
        grid=(n_steps,),
        in_specs=in_specs,
        out_specs=out_specs,
        out_shape=out_shape,
        scratch_shapes=scratch,
        compiler_params=pltpu.CompilerParams(
            dimension_semantics=("arbitrary",),
            vmem_limit_bytes=VMEM_LIMIT_BYTES),
        name="layer_sample" if has_past else "layer_prompt",
    )(*inputs)


def _prep_params(g_norm, w_in, g_q, g_k, sinks, conv_w, conv_b, dt_bias, a_log, d_skip, g_ssd, g_mq, w_out):
    o_z = ATTN_WIDTH + 2 * KV_WIDTH + ATTN_WIDTH
    o_x = o_z + SSD_WIDTH
    o_b = o_x + SSD_WIDTH
    o_c = o_b + SSD_GROUPS * SSD_STATE
    o_dt = o_c + SSD_GROUPS * SSD_STATE
    o_mq = o_dt + SSD_HEADS
    w_dt = w_in[:, o_dt:o_mq]
    zeros = lambda n: jnp.zeros((D_MODEL, n), w_in.dtype)
    groups = [
        w_in[:, 0:o_z],
        jnp.concatenate([w_in[:, o_z:o_x], w_dt, zeros(GROUP_W - SSD_WIDTH - SSD_HEADS)], axis=1),
        w_in[:, o_x:o_c],
        jnp.concatenate([w_in[:, o_c:o_dt], w_in[:, o_mq:]], axis=1),
    ]
    pad = LANES - SSD_HEADS
    return {
        "w_in": jnp.stack(groups).astype(BF16),
        "w_dt": w_dt.T.astype(BF16),
        "w_out": w_out.astype(BF16),
        "g_norm": g_norm.reshape(1, D_MODEL),
        "gq": (jnp.tile(g_q, N_Q_HEADS) * (1.0 / math.sqrt(HEAD_DIM))).reshape(1, ATTN_WIDTH),
        "gk": jnp.tile(g_k, N_KV_HEADS).reshape(1, KV_WIDTH),
        "sinks": sinks,
        "conv_w": conv_w,
        "conv_b": conv_b.reshape(1, CONV_CH),
        "dtb_row": jnp.pad(dt_bias, (0, pad)).reshape(1, LANES),
        "dtb_col": dt_bias.reshape(SSD_HEADS, 1),
        "alog_exp": jnp.repeat(a_log, SSD_HEAD_DIM).reshape(1, SSD_WIDTH),
        "alog_col": a_log.reshape(SSD_HEADS, 1),
        "d_skip": jnp.repeat(d_skip, SSD_HEAD_DIM).reshape(1, SSD_WIDTH),
        "g_ssd": g_ssd.reshape(1, SSD_WIDTH),
        "g_mq": g_mq.reshape(1, MEM_HEAD_DIM),
    }


PROMPT_TILE = 256


def kernel(x_prompt, x_sample, cache_win_k, cache_win_v, state_ssm, state_conv, cache_mem_k, cache_mem_v,
           mem_prompt, g_norm, w_in, g_q, g_k, sinks, conv_w, conv_b, dt_bias, a_log, d_skip, g_ssd,
           g_mem, w_mem_k, w_mem_v, g_mk, g_mq, w_out):
    depth = w_in.shape[0]
    assert depth == 1
    l = 0
    bp, t_p, _ = x_prompt.shape
    bs, t_s, _ = x_sample.shape
    params = _prep_params(g_norm[l], w_in[l], g_q[l], g_k[l], sinks[l], conv_w[l], conv_b[l], dt_bias[l],
                          a_log[l], d_skip[l], g_ssd[l], g_mq[l], w_out[l])

    mk_p, mv_p = _memory_kv(mem_prompt, g_mem[l], w_mem_k[l], w_mem_v[l], g_mk[l])

    yp, kp, vp, sp, cp = _layer(
        x_prompt, jnp.arange(t_p, dtype=jnp.int32), mk_p, mv_p, None, params, tile=min(PROMPT_TILE, t_p))

    past = (cache_win_k[l].reshape(bs, WINDOW, KV_WIDTH), cache_win_v[l].reshape(bs, WINDOW, KV_WIDTH),
            state_ssm[l].reshape(bs, SSD_WIDTH, SSD_STATE), state_conv[l])
    ys, ks, vs, ss, cs = _layer(
        x_sample, PAST_LEN + jnp.arange(t_s, dtype=jnp.int32),
        cache_mem_k[l].reshape(bs, MEM_TOKENS, MEM_WIDTH), cache_mem_v[l].reshape(bs, MEM_TOKENS, MEM_WIDTH),
        past, params, tile=t_s)

    kv5 = lambda a, b: a.reshape(1, b, WINDOW, N_KV_HEADS, HEAD_DIM)
    ssm5 = lambda a, b: a.reshape(1, b, SSD_HEADS, SSD_HEAD_DIM, SSD_STATE)
    mem5 = lambda a: a.reshape(1, bp, MEM_TOKENS, MEM_HEADS, MEM_HEAD_DIM)
    return (yp, ys,
            kv5(kp, bp), kv5(vp, bp), ssm5(sp, bp), cp[None],
            mem5(mk_p), mem5(mv_p),
            kv5(ks, bs), kv5(vs, bs), ssm5(ss, bs), cs[None])
```

```python
import functools
import math

import numpy as np
import jax
import jax.numpy as jnp
from jax import lax
from jax.experimental import pallas as pl
from jax.experimental.pallas import tpu as pltpu

D_MODEL = 1024
CHUNK = 64
HEAD_DIM = 64
N_Q_HEADS = 8
N_KV_HEADS = 2
ATTN_WIDTH = N_Q_HEADS * HEAD_DIM
KV_WIDTH = N_KV_HEADS * HEAD_DIM
WINDOW = 128
ROPE_THETA = 10000.0
SSD_WIDTH = D_MODEL
SSD_HEAD_DIM = 64
SSD_HEADS = SSD_WIDTH // SSD_HEAD_DIM
SSD_GROUPS = 2
SSD_STATE = 128
CONV_WIDTH = 4
CONV_CH = SSD_WIDTH + 2 * SSD_GROUPS * SSD_STATE
MEM_TOKENS = 256
MEM_HEADS = 4
MEM_HEAD_DIM = 128
MEM_WIDTH = MEM_HEADS * MEM_HEAD_DIM
MIX_WIDTH = 2 * D_MODEL
EPS = 1e-6
PAST_LEN = 2048

LANES = 128
SUBLANES = 8
VMEM_LIMIT_BYTES = 56 * 1024 * 1024

N_GROUPS = 4
GROUP_W = 1280
G0_Q, G0_K, G0_V, G0_GA = 0, ATTN_WIDTH, ATTN_WIDTH + KV_WIDTH, ATTN_WIDTH + 2 * KV_WIDTH
G1_Z, G1_DT = 0, SSD_WIDTH
G2_X, G2_B = 0, SSD_WIDTH
G3_C, G3_MQ, G3_GM = 0, SSD_GROUPS * SSD_STATE, SSD_GROUPS * SSD_STATE + MEM_WIDTH
PIECE_W = 256
N_PIECES = GROUP_W // PIECE_W
HEAD_ROWS = SUBLANES

F32 = jnp.float32
BF16 = jnp.bfloat16
NEG_BIG = -1e30


def _dot(a, b):
    return jnp.dot(a, b, preferred_element_type=F32)


def _dot_nt(a, b):
    return lax.dot_general(a, b, (((1,), (1,)), ((), ())), preferred_element_type=F32)


def _dot_tn(a, b):
    return lax.dot_general(a, b, (((0,), (0,)), ((), ())), preferred_element_type=F32)


def _split3(x):
    h1 = x.astype(BF16)
    r1 = x - h1.astype(F32)
    h2 = r1.astype(BF16)
    r2 = r1 - h2.astype(F32)
    return h1, h2, r2.astype(BF16)


def _dot_sel_rhs(x, sel):
    a, b, c = _split3(x)
    return (_dot(c, sel) + _dot(b, sel)) + _dot(a, sel)


def _dot_sel_lhs(sel, x):
    a, b, c = _split3(x)
    return (_dot(sel, c) + _dot(sel, b)) + _dot(sel, a)


def _dot_sel2(x, sel2):
    hi = x.astype(BF16)
    lo = (x - hi.astype(F32)).astype(BF16)
    return _dot(jnp.concatenate([hi, lo], axis=1), sel2)


def _seg_sumsq(x, seg):
    return _dot((x * x).astype(BF16), seg)


def _silu(x):
    h = 0.5 * x
    return h + h * jnp.tanh(h)


def _softplus(x):
    return jnp.maximum(x, 0.0) + jnp.log(1.0 + jnp.exp(-jnp.abs(x)))


def _rms_scale(x):
    return lax.rsqrt(jnp.mean(x * x, axis=-1, keepdims=True) + EPS)


def _softmax_rows(sc, extra=None):
    m = jnp.max(sc, axis=-1, keepdims=True)
    if extra is not None:
        m = jnp.maximum(m, extra)
    e = jnp.exp(sc - m)
    den = jnp.sum(e, axis=-1, keepdims=True)
    if extra is not None:
        den = den + jnp.exp(extra - m)
    return (e * (1.0 / den)).astype(BF16)


def _rope(x, cos, sin_signed, first_half):
    partner = jnp.where(first_half, pltpu.roll(x, LANES - HEAD_DIM // 2, 1), pltpu.roll(x, HEAD_DIM // 2, 1))
    return x * cos + partner * sin_signed


def _memory_kv_kernel(mem_ref, g_mem_ref, wk_ref, wv_ref, g_mk_ref, k_out, v_out):
    x = mem_ref[0]
    hb = (x * _rms_scale(x) * g_mem_ref[...]).astype(BF16)
    k = _dot(hb, wk_ref[...])
    v = _dot(hb, wv_ref[...])
    for hh in range(MEM_HEADS):
        sl = slice(hh * MEM_HEAD_DIM, (hh + 1) * MEM_HEAD_DIM)
        kh = k[:, sl]
        k_out[0, :, hh, :] = kh * _rms_scale(kh) * g_mk_ref[...]
        v_out[0, :, hh, :] = v[:, sl]


def _memory_kv(mem, g_mem, w_k, w_v, g_mk):
    b = mem.shape[0]
    full = lambda shape: pl.BlockSpec(shape, lambda i: (0,) * len(shape))
    return pl.pallas_call(
        _memory_kv_kernel,
        grid=(b,),
        in_specs=[
            pl.BlockSpec((1, MEM_TOKENS, D_MODEL), lambda i: (i, 0, 0)),
            full((1, D_MODEL)),
            full((D_MODEL, MEM_WIDTH)),
            full((D_MODEL, MEM_WIDTH)),
            full((1, MEM_HEAD_DIM)),
        ],
        out_specs=[
            pl.BlockSpec((1, MEM_TOKENS, MEM_HEADS, MEM_HEAD_DIM), lambda i: (i, 0, 0, 0)),
            pl.BlockSpec((1, MEM_TOKENS, MEM_HEADS, MEM_HEAD_DIM), lambda i: (i, 0, 0, 0)),
        ],
        out_shape=[jax.ShapeDtypeStruct((b, MEM_TOKENS, MEM_HEADS, MEM_HEAD_DIM), F32)] * 2,
        compiler_params=pltpu.CompilerParams(dimension_semantics=("arbitrary",)),
        name="memory_kv",
    )(mem, g_mem.reshape(1, D_MODEL), w_k.astype(BF16), w_v.astype(BF16), g_mk.reshape(1, MEM_HEAD_DIM))


def _layer_kernel(*refs, tile, n_tiles, n_sub, has_past):
    if has_past:
        (x_first_ref, x_next_ref, cos_ref, sin_ref, memk_ref, memv_ref,
         wink_ref, winv_ref, ssm_ref, conv_ref, *rest) = refs
    else:
        x_first_ref, x_next_ref, cos_ref, sin_ref, memk_ref, memv_ref, *rest = refs
        wink_ref = winv_ref = ssm_ref = conv_ref = None
    (w_in_ref, wdt_ref, w_out_ref,
     g_norm_ref, gq_ref, gk_ref, sinks_ref, convw_ref, convb_ref,
     dtb_row_ref, dtb_col_ref, alog_row_ref, alog_col_ref, dskip_ref, g_ssd_ref, g_mq_ref,
     segq_ref, segk_ref, expand_ref, tri_ref, duptri_ref,
     y_ref, newk_ref, newv_ref, ssm_out_ref, conv_out_ref,
     proj_a, proj_b, hb_a, hb_b, xres_a, xres_b,
     qs, kwin, vwin, kr, vr, actd, state_t, mix, memk_bf, memv_bf) = rest

    T = tile
    n_chunks = T // CHUNK
    n_blocks = n_chunks // n_sub
    R = n_sub * CHUNK
    groups_per_block = N_GROUPS // n_blocks
    s = pl.program_id(0)
    t = s % n_tiles
    body_rows = pl.ds(HEAD_ROWS, T)

    def normed_input(x_ref, hb, xres):
        x = x_ref[0]
        xres[...] = x
        hb[...] = (x * _rms_scale(x) * g_norm_ref[...]).astype(BF16)

    def project_piece(hb, proj, g, i):
        cols = slice(i * PIECE_W, (i + 1) * PIECE_W)
        proj[g, body_rows, cols] = _dot(hb[...], w_in_ref[g, :, cols])

    @pl.when(s == 0)
    def _prologue():
        normed_input(x_first_ref, hb_a, xres_a)
        for g in range(N_GROUPS):
            for i in range(N_PIECES):
                project_piece(hb_a, proj_a, g, i)

    lane = lax.broadcasted_iota(jnp.int32, (1, LANES), 1)
    first_half = (lane % HEAD_DIM) < (HEAD_DIM // 2)
    lo_half = lane < HEAD_DIM
    row64 = lax.broadcasted_iota(jnp.int32, (CHUNK, LANES), 0)
    lane64 = lax.broadcasted_iota(jnp.int32, (CHUNK, LANES), 1)
    causal_dup = row64 >= (lane64 % CHUNK)
    lo_half64 = lane64 < CHUNK
    row128 = lax.broadcasted_iota(jnp.int32, (2 * CHUNK, 1), 0)
    key_idx = lax.broadcasted_iota(jnp.int32, (1, WINDOW + CHUNK), 1)
    gw = SSD_WIDTH // SSD_GROUPS
    n_pairs = SSD_HEADS // 2
    pairs_per_group = n_pairs // SSD_GROUPS
    mem_scale = 1.0 / math.sqrt(MEM_HEAD_DIM)
    bc_w = SSD_GROUPS * SSD_STATE

    def build_masked(src, dst, rows):
        full = src[rows, :]
        swapped = pltpu.roll(full, HEAD_DIM, 1)
        zero = jnp.zeros_like(full)
        dst[0, rows, :] = jnp.where(lo_half, full, zero).astype(BF16)
        dst[1, rows, :] = jnp.where(lo_half, zero, swapped).astype(BF16)
        dst[2, rows, :] = jnp.where(lo_half, swapped, zero).astype(BF16)
        dst[3, rows, :] = jnp.where(lo_half, zero, full).astype(BF16)

    def step(proj, hb, xres, proj_nxt, hb_nxt, xres_nxt):
        @pl.when(t == 0)
        def _init():
            zero_head = jnp.zeros((HEAD_ROWS, GROUP_W), F32)
            proj[2, 0:HEAD_ROWS, :] = zero_head
            proj[3, 0:HEAD_ROWS, :] = zero_head
            if has_past:
                kwin[0:WINDOW, :] = wink_ref[0]
                vwin[0:WINDOW, :] = winv_ref[0]
                state_t[...] = ssm_ref[0].T
                tail = conv_ref[0]
                r_lo = HEAD_ROWS - (CONV_WIDTH - 1)
                proj[2, r_lo:HEAD_ROWS, :] = tail[:, 0:GROUP_W]
                proj[3, r_lo:HEAD_ROWS, 0:CONV_CH - GROUP_W] = tail[:, GROUP_W:CONV_CH]
                for src, dst in ((kwin, kr), (vwin, vr)):
                    build_masked(src, dst, slice(0, WINDOW))
            else:
                state_t[...] = jnp.zeros((SSD_STATE, SSD_WIDTH), F32)
                kwin[0:WINDOW, :] = jnp.zeros((WINDOW, KV_WIDTH), F32)
                vwin[0:WINDOW, :] = jnp.zeros((WINDOW, KV_WIDTH), F32)
                for v in range(4):
                    kr[v, 0:WINDOW, :] = jnp.zeros((WINDOW, KV_WIDTH), BF16)
                    vr[v, 0:WINDOW, :] = jnp.zeros((WINDOW, KV_WIDTH), BF16)
            for hh in range(MEM_HEADS):
                sl = slice(hh * MEM_HEAD_DIM, (hh + 1) * MEM_HEAD_DIM)
                memk_bf[:, sl] = memk_ref[0, :, hh, :].astype(BF16)
                memv_bf[:, sl] = memv_ref[0, :, hh, :].astype(BF16)

        if n_tiles > 1:
            @pl.when(t > 0)
            def _shift():
                kwin[0:WINDOW, :] = kwin[T:T + WINDOW, :]
                vwin[0:WINDOW, :] = vwin[T:T + WINDOW, :]
                for v in range(4):
                    kr[v, 0:WINDOW, :] = kr[v, T:T + WINDOW, :]
                    vr[v, 0:WINDOW, :] = vr[v, T:T + WINDOW, :]
                proj[2, 0:HEAD_ROWS, :] = proj_nxt[2, T:T + HEAD_ROWS, :]
                proj[3, 0:HEAD_ROWS, :] = proj_nxt[3, T:T + HEAD_ROWS, :]

        normed_input(x_next_ref, hb_nxt, xres_nxt)

        dt_t = _dot_nt(wdt_ref[...], hb[...])
        da_t = _softplus(dt_t + dtb_col_ref[...]) * (-jnp.exp(alog_col_ref[...]))
        acum_t = _dot_sel_rhs(da_t, duptri_ref[...])
        for cc in range(n_chunks):
            actd[cc] = acum_t[:, cc * LANES:(cc + 1) * LANES]

        a_row = -jnp.exp(alog_row_ref[...])
        tri = tri_ref[...]
        d_skip = dskip_ref[...]
        subs = range(n_sub)

        def block_body(i, carry):
            static = isinstance(i, int)
            r0 = i * R if static else pl.multiple_of(i * R, R)
            rows = pl.ds(r0, R)
            prow = pl.ds(r0 + HEAD_ROWS, R)
            wrow = pl.ds(r0 + WINDOW, R)
            sub_rows = [slice(u * CHUNK, (u + 1) * CHUNK) for u in subs]
            krows = [pl.ds(r0 + u * CHUNK, WINDOW + CHUNK) for u in subs]

            pieces = [(i * groups_per_block + gg, pp) for gg in range(groups_per_block) for pp in range(N_PIECES)]
            per_gap = len(pieces) // N_PIECES

            def issue(n):
                for _ in range(n):
                    g, pp = pieces.pop(0)
                    project_piece(hb_nxt, proj_nxt, g, pp)

            issue(per_gap)

            q = proj[0, prow, G0_Q:G0_Q + ATTN_WIDTH]
            k = proj[0, prow, G0_K:G0_K + KV_WIDTH]
            dt_row = _softplus(proj[1, prow, G1_DT:G1_DT + LANES] + dtb_row_ref[...])
            mem_q = []
            for hh in range(MEM_HEADS):
                mqh = proj[3, prow, G3_MQ + hh * MEM_HEAD_DIM:G3_MQ + (hh + 1) * MEM_HEAD_DIM]
                mem_q.append((mqh * _rms_scale(mqh) * g_mq_ref[...]).astype(BF16))

            seg = segq_ref[...]
            q_blocks = [q[:, j * 2 * LANES:(j + 1) * 2 * LANES] for j in range(ATTN_WIDTH // (2 * LANES))]
            q_ss = [_seg_sumsq(qj, seg) for qj in q_blocks]
            k_ss = _seg_sumsq(k, segk_ref[...])
            dtx = _dot_sel2(dt_row, expand_ref[...])
            acum_n = _dot_sel_lhs(tri, dt_row * a_row)
            mem_sc = [_dot_nt(mem_q[hh], memk_bf[:, hh * MEM_HEAD_DIM:(hh + 1) * MEM_HEAD_DIM])
                      for hh in range(MEM_HEADS)]

            issue(per_gap)

            def conv(group, c0, width, ch0):
                acc = convb_ref[:, ch0:ch0 + width]
                blk = proj[group, pl.ds(r0, HEAD_ROWS + R), c0:c0 + width]
                for j in range(CONV_WIDTH):
                    off = HEAD_ROWS - (CONV_WIDTH - 1) + j
                    acc = acc + blk[off:off + R, :] * convw_ref[j:j + 1, ch0:ch0 + width]
                return _silu(acc)

            b_all = conv(2, G2_B, bc_w, SSD_WIDTH).astype(BF16)
            c_all = conv(3, G3_C, bc_w, SSD_WIDTH + bc_w).astype(BF16)
            xs = conv(2, G2_X, SSD_WIDTH, 0)

            cos = cos_ref[rows, :]
            sin_s = sin_ref[rows, :]
            for j, (qj, ss) in enumerate(zip(q_blocks, q_ss)):
                qn = qj * lax.rsqrt(ss * (1.0 / HEAD_DIM) + EPS) * gq_ref[:, j * 2 * LANES:(j + 1) * 2 * LANES]
                for jj in range(2):
                    c0 = j * 2 * LANES + jj * LANES
                    qs[rows, c0:c0 + LANES] = _rope(qn[:, jj * LANES:(jj + 1) * LANES], cos, sin_s,
                                                    first_half).astype(BF16)
            kn = k * lax.rsqrt(k_ss * (1.0 / HEAD_DIM) + EPS) * gk_ref[...]
            kwin[wrow, :] = _rope(kn, cos, sin_s, first_half)
            vwin[wrow, :] = proj[0, prow, G0_V:G0_V + KV_WIDTH]
            build_masked(kwin, kr, wrow)
            build_masked(vwin, vr, wrow)

            mem_p = [_softmax_rows(mem_sc[hh] * mem_scale) for hh in range(MEM_HEADS)]
            xdt = xs * dtx

            bgs = [[b_all[sub_rows[u], g * SSD_STATE:(g + 1) * SSD_STATE] for g in range(SSD_GROUPS)] for u in subs]
            cgs = [[c_all[sub_rows[u], g * SSD_STATE:(g + 1) * SSD_STATE] for g in range(SSD_GROUPS)] for u in subs]
            cbs = [[_dot_nt(cgs[u][g], jnp.concatenate([bgs[u][g], bgs[u][g]], axis=0))
                    for g in range(SSD_GROUPS)] for u in subs]
            cross_cb = [_dot_nt(cgs[1][g], bgs[0][g]) for g in range(SSD_GROUPS)] if n_sub == 2 else None
            y_in = [_dot(c_all[:, g * SSD_STATE:(g + 1) * SSD_STATE], state_t[:, g * gw:(g + 1) * gw].astype(BF16))
                    for g in range(SSD_GROUPS)]
            acum = _dot_sel2(acum_n, expand_ref[...])
            att_sc = []
            for u in subs:
                for g in range(N_KV_HEADS):
                    base = g * 2 * LANES
                    qrows = pl.ds(r0 + u * CHUNK, CHUNK)
                    qg = jnp.concatenate([qs[qrows, base:base + LANES], qs[qrows, base + LANES:base + 2 * LANES]],
                                         axis=0)
                    for pos in range(2):
                        att_sc.append(_dot_nt(qg, kr[2 * g + pos, krows[u], :]))
            mem_o = [_dot(mem_p[hh], memv_bf[:, hh * MEM_HEAD_DIM:(hh + 1) * MEM_HEAD_DIM])
                     for hh in range(MEM_HEADS)]

            issue(per_gap)

            att_p = []
            for u in subs:
                if not has_past:
                    first_valid = jnp.where(t == 0, WINDOW - (r0 + u * CHUNK), 0)
                    key_ok = key_idx >= first_valid
                for g in range(N_KV_HEADS):
                    for pos in range(2):
                        hd0 = 4 * g + pos
                        sink = jnp.where(row128 < CHUNK, sinks_ref[hd0], sinks_ref[hd0 + 2])
                        sc = att_sc[(u * N_KV_HEADS + g) * 2 + pos]
                        if not has_past:
                            sc = jnp.where(key_ok, sc, NEG_BIG)
                        att_p.append(_softmax_rows(sc, sink))

            exp_acum = jnp.exp(acum)
            a_last = [acum[u * CHUNK + CHUNK - 1:(u + 1) * CHUNK, :] for u in subs]
            dec_end = [jnp.exp(a_last[u]) for u in subs]
            x_end = [(xdt[sub_rows[u], :] * jnp.exp(a_last[u] - acum[sub_rows[u], :])).astype(BF16) for u in subs]
            ws, xbds = [], []
            zero_bd = jnp.zeros((2 * CHUNK, LANES), BF16)
            for u in subs:
                act = actd[i * n_sub + u]
                for quad in range(n_pairs // 2):
                    w_parts, bd_parts = [], []
                    for pair in (2 * quad, 2 * quad + 1):
                        g = pair // pairs_per_group
                        cols = slice(pair * LANES, (pair + 1) * LANES)
                        rowb = jnp.where(lane < CHUNK, act[2 * pair:2 * pair + 1, :], act[2 * pair + 1:2 * pair + 2, :])
                        dec = jnp.exp(jnp.where(causal_dup, acum[sub_rows[u], cols] - rowb, -jnp.inf))
                        w_parts.append((cbs[u][g] * dec).astype(BF16))
                        xp = xdt[sub_rows[u], cols]
                        zero = jnp.zeros_like(xp)
                        bd_parts.append(jnp.concatenate(
                            [jnp.where(lo_half64, xp, zero), jnp.where(lo_half64, zero, xp)], axis=0).astype(BF16))
                    ws.append(jnp.concatenate(w_parts, axis=1))
                    xbds.append(jnp.concatenate(
                        [jnp.concatenate([bd_parts[0], zero_bd], axis=1),
                         jnp.concatenate([zero_bd, bd_parts[1]], axis=1)], axis=0))

            for hh in range(MEM_HEADS):
                gm = proj[3, prow, G3_GM + hh * MEM_HEAD_DIM:G3_GM + (hh + 1) * MEM_HEAD_DIM]
                c0 = ATTN_WIDTH + SSD_WIDTH + hh * MEM_HEAD_DIM
                mix[rows, c0:c0 + MEM_HEAD_DIM] = (mem_o[hh] * _silu(gm)).astype(BF16)

            att_o = []
            for u in subs:
                for g in range(N_KV_HEADS):
                    j0 = (u * N_KV_HEADS + g) * 2
                    att_o.append(_dot(att_p[j0], vr[2 * g, krows[u], :]) + _dot(att_p[j0 + 1], vr[2 * g + 1, krows[u], :]))
            for g in range(SSD_GROUPS):
                gcols = slice(g * gw, (g + 1) * gw)
                st = state_t[:, gcols]
                for u in subs:
                    st = st * dec_end[u][:, gcols] + _dot_tn(bgs[u][g], x_end[u][:, gcols])
                state_t[:, gcols] = st
            cross = ([_dot(cross_cb[g].astype(BF16), x_end[0][:, g * gw:(g + 1) * gw]) for g in range(SSD_GROUPS)]
                     if n_sub == 2 else None)
            y_diag = [_dot(w, xbd) for w, xbd in zip(ws, xbds)]

            issue(len(pieces))

            for u in subs:
                for g in range(N_KV_HEADS):
                    base = g * 2 * LANES
                    orow = pl.ds(r0 + u * CHUNK, CHUNK)
                    ga = proj[0, pl.ds(r0 + u * CHUNK + HEAD_ROWS, CHUNK), G0_GA + base:G0_GA + base + 2 * LANES]
                    o = att_o[u * N_KV_HEADS + g]
                    mix[orow, base:base + LANES] = (o[0:CHUNK] * _silu(ga[:, 0:LANES])).astype(BF16)
                    mix[orow, base + LANES:base + 2 * LANES] = (o[CHUNK:] * _silu(ga[:, LANES:])).astype(BF16)

            y_rows = []
            for u in subs:
                ys = []
                for pair in range(n_pairs):
                    g, hp = divmod(pair, pairs_per_group)
                    cols = slice(pair * LANES, (pair + 1) * LANES)
                    lc = slice(hp * LANES, (hp + 1) * LANES)
                    y_off = y_in[g][sub_rows[u], lc]
                    if u == 1:
                        y_off = y_off * dec_end[0][:, cols] + cross[g][:, lc]
                    yd = y_diag[(u * n_pairs + pair) // 2][:, (pair % 2) * LANES:(pair % 2 + 1) * LANES]
                    ys.append(yd + y_off * exp_acum[sub_rows[u], cols] + d_skip[:, cols] * xs[sub_rows[u], cols])
                y_rows.append(jnp.concatenate(ys, axis=1))
            y_all = y_rows[0] if n_sub == 1 else jnp.concatenate(y_rows, axis=0)
            z = proj[1, prow, G1_Z:G1_Z + SSD_WIDTH]
            yg = y_all * _silu(z)
            mix[rows, ATTN_WIDTH:ATTN_WIDTH + SSD_WIDTH] = (yg * _rms_scale(yg) * g_ssd_ref[...]).astype(BF16)
            return carry

        if n_blocks == 1:
            block_body(0, 0)
        else:
            lax.fori_loop(0, n_blocks, block_body, 0)

        y_ref[0] = xres[...] + _dot(mix[...], w_out_ref[...])

        @pl.when(t == n_tiles - 1)
        def _final():
            newk_ref[0] = kwin[T:T + WINDOW, :]
            newv_ref[0] = vwin[T:T + WINDOW, :]
            ssm_out_ref[0] = state_t[...].T
            r_lo = T + HEAD_ROWS - (CONV_WIDTH - 1)
            conv_out_ref[0, :, 0:GROUP_W] = proj[2, r_lo:T + HEAD_ROWS, :]
            conv_out_ref[0, :, GROUP_W:CONV_CH] = proj[3, r_lo:T + HEAD_ROWS, 0:CONV_CH - GROUP_W]

    parity = s % 2

    @pl.when(parity == 0)
    def _even():
        step(proj_a, hb_a, xres_a, proj_b, hb_b, xres_b)

    @pl.when(parity == 1)
    def _odd():
        step(proj_b, hb_b, xres_b, proj_a, hb_a, xres_a)


def _constants(tile, n_sub):
    n_chunks = tile // CHUNK
    rows_per_block = n_sub * CHUNK
    idx = np.arange
    segq = (idx(2 * LANES)[:, None] // HEAD_DIM == idx(2 * LANES)[None, :] // HEAD_DIM)
    segk = (idx(KV_WIDTH)[:, None] // HEAD_DIM == idx(KV_WIDTH)[None, :] // HEAD_DIM)
    expand = (idx(2 * LANES)[:, None] % LANES == idx(SSD_WIDTH)[None, :] // SSD_HEAD_DIM)
    r = idx(rows_per_block)
    tri = (r[:, None] // CHUNK == r[None, :] // CHUNK) & (r[:, None] >= r[None, :])
    src = idx(tile)[:, None]
    dst = idx(n_chunks * LANES)[None, :]
    duptri = (src // CHUNK == dst // LANES) & (src % CHUNK <= dst % CHUNK)
    return [jnp.asarray(m.astype(np.float32), dtype=BF16) for m in (segq, segk, expand, tri, duptri)]


def _layer(x, positions, mem_k, mem_v, past, params, *, tile):
    b, seq, _ = x.shape
    n_t = seq // tile
    n_steps = b * n_t
    has_past = past is not None
    n_chunks = tile // CHUNK
    n_sub = 2 if n_chunks % 2 == 0 else 1
    assert seq % tile == 0 and tile % CHUNK == 0 and N_GROUPS % (n_chunks // n_sub) == 0
    assert n_t == 1 or tile >= WINDOW

    half = HEAD_DIM // 2
    freqs = ROPE_THETA ** (-jnp.arange(half, dtype=F32) / half)
    ang = positions.astype(F32)[:, None] * freqs[None, :]
    cos, sin = jnp.cos(ang), jnp.sin(ang)
    cos_t = jnp.tile(cos, (1, LANES // half))
    sin_t = jnp.tile(jnp.concatenate([-sin, sin], axis=1), (1, LANES // HEAD_DIM))

    consts = _constants(tile, n_sub)

    def full(a):
        nd = a.ndim
        return pl.BlockSpec(a.shape, lambda i, _n=nd: (0,) * _n, pipeline_mode=pl.Buffered(1))

    def per_row(a):
        nd = a.ndim
        return pl.BlockSpec((1,) + a.shape[1:], lambda i, _n=nd: (i // n_t,) + (0,) * (_n - 1))

    def next_tile(i):
        j = jnp.minimum(i + 1, n_steps - 1)
        return (j // n_t, j % n_t, 0)

    vec_params = [params[k] for k in ("g_norm", "gq", "gk")]
    tail_params = [params[k] for k in ("conv_w", "conv_b", "dtb_row", "dtb_col", "alog_row", "alog_col",
                                       "d_skip", "g_ssd", "g_mq")]
    past_inputs = list(past) if has_past else []
    inputs = [x, x, cos_t, sin_t, mem_k, mem_v, *past_inputs,
              params["w_in"], params["w_dt"], params["w_out"],
              *vec_params, params["sinks"], *tail_params, *consts]
    in_specs = [
        pl.BlockSpec((1, tile, D_MODEL), lambda i: (0, 0, 0)),
        pl.BlockSpec((1, tile, D_MODEL), next_tile),
        pl.BlockSpec((tile, LANES), lambda i: (i % n_t, 0)),
        pl.BlockSpec((tile, LANES), lambda i: (i % n_t, 0)),
        per_row(mem_k), per_row(mem_v),
        *[per_row(p) for p in past_inputs],
        full(params["w_in"]), full(params["w_dt"]), full(params["w_out"]),
        *[full(p) for p in vec_params],
        pl.BlockSpec(memory_space=pltpu.SMEM),
        *[full(p) for p in tail_params],
        *[full(c) for c in consts],
    ]
    out_shape = [
        jax.ShapeDtypeStruct((b, seq, D_MODEL), F32),
        jax.ShapeDtypeStruct((b, WINDOW, KV_WIDTH), F32),
        jax.ShapeDtypeStruct((b, WINDOW, KV_WIDTH), F32),
        jax.ShapeDtypeStruct((b, SSD_WIDTH, SSD_STATE), F32),
        jax.ShapeDtypeStruct((b, CONV_WIDTH - 1, CONV_CH), F32),
    ]
    out_specs = [
        pl.BlockSpec((1, tile, D_MODEL), lambda i: (i // n_t, i % n_t, 0)),
        pl.BlockSpec((1, WINDOW, KV_WIDTH), lambda i: (i // n_t, 0, 0)),
        pl.BlockSpec((1, WINDOW, KV_WIDTH), lambda i: (i // n_t, 0, 0)),
        pl.BlockSpec((1, SSD_WIDTH, SSD_STATE), lambda i: (i // n_t, 0, 0)),
        pl.BlockSpec((1, CONV_WIDTH - 1, CONV_CH), lambda i: (i // n_t, 0, 0)),
    ]
    proj_shape = (N_GROUPS, HEAD_ROWS + tile, GROUP_W)
    scratch = [
        pltpu.VMEM(proj_shape, F32), pltpu.VMEM(proj_shape, F32),
        pltpu.VMEM((tile, D_MODEL), BF16), pltpu.VMEM((tile, D_MODEL), BF16),
        pltpu.VMEM((tile, D_MODEL), F32), pltpu.VMEM((tile, D_MODEL), F32),
        pltpu.VMEM((tile, ATTN_WIDTH), BF16),
        pltpu.VMEM((WINDOW + tile, KV_WIDTH), F32),
        pltpu.VMEM((WINDOW + tile, KV_WIDTH), F32),
        pltpu.VMEM((4, WINDOW + tile, KV_WIDTH), BF16),
        pltpu.VMEM((4, WINDOW + tile, KV_WIDTH), BF16),
        pltpu.VMEM((n_chunks, SSD_HEADS, LANES), F32),
        pltpu.VMEM((SSD_STATE, SSD_WIDTH), F32),
        pltpu.VMEM((tile, MIX_WIDTH), BF16),
        pltpu.VMEM((MEM_TOKENS, MEM_WIDTH), BF16),
        pltpu.VMEM((MEM_TOKENS, MEM_WIDTH), BF16),
    ]
    return pl.pallas_call(
        functools.partial(_layer_kernel, tile=tile, n_tiles=n_t, n_sub=n_sub, has_past=has_past),
        grid=(n_steps,),
        in_specs=in_specs,
        out_specs=out_specs,
        out_shape=out_shape,
        scratch_shapes=scratch,
        compiler_params=pltpu.CompilerParams(
            dimension_semantics=("arbitrary",),
            vmem_limit_bytes=VMEM_LIMIT_BYTES),
        name="layer_sample" if has_past else "layer_prompt",
    )(*inputs)


def _prep_params(g_norm, w_in, g_q, g_k, sinks, conv_w, conv_b, dt_bias, a_log, d_skip, g_ssd, g_mq, w_out):
    o_z = ATTN_WIDTH + 2 * KV_WIDTH + ATTN_WIDTH
    o_x = o_z + SSD_WIDTH
    o_b = o_x + SSD_WIDTH
    o_c = o_b + SSD_GROUPS * SSD_STATE
    o_dt = o_c + SSD_GROUPS * SSD_STATE
    o_mq = o_dt + SSD_HEADS
    w_dt = w_in[:, o_dt:o_mq]
    zeros = lambda n: jnp.zeros((D_MODEL, n), w_in.dtype)
    groups = [
        w_in[:, 0:o_z],
        jnp.concatenate([w_in[:, o_z:o_x], w_dt, zeros(GROUP_W - SSD_WIDTH - SSD_HEADS)], axis=1),
        w_in[:, o_x:o_c],
        jnp.concatenate([w_in[:, o_c:o_dt], w_in[:, o_mq:]], axis=1),
    ]
    pad = LANES - SSD_HEADS
    return {
        "w_in": jnp.stack(groups).astype(BF16),
        "w_dt": w_dt.T.astype(BF16),
        "w_out": w_out.astype(BF16),
        "g_norm": g_norm.reshape(1, D_MODEL),
        "gq": (jnp.tile(g_q, N_Q_HEADS) * (1.0 / math.sqrt(HEAD_DIM))).reshape(1, ATTN_WIDTH),
        "gk": jnp.tile(g_k, N_KV_HEADS).reshape(1, KV_WIDTH),
        "sinks": sinks,
        "conv_w": conv_w,
        "conv_b": conv_b.reshape(1, CONV_CH),
        "dtb_row": jnp.pad(dt_bias, (0, pad)).reshape(1, LANES),
        "dtb_col": dt_bias.reshape(SSD_HEADS, 1),
        "alog_row": jnp.pad(a_log, (0, pad)).reshape(1, LANES),
        "alog_col": a_log.reshape(SSD_HEADS, 1),
        "d_skip": jnp.repeat(d_skip, SSD_HEAD_DIM).reshape(1, SSD_WIDTH),
        "g_ssd": g_ssd.reshape(1, SSD_WIDTH),
        "g_mq": g_mq.reshape(1, MEM_HEAD_DIM),
    }


PROMPT_TILE = 256


def kernel(x_prompt, x_sample, cache_win_k, cache_win_v, state_ssm, state_conv, cache_mem_k, cache_mem_v,
           mem_prompt, g_norm, w_in, g_q, g_k, sinks, conv_w, conv_b, dt_bias, a_log, d_skip, g_ssd,
           g_mem, w_mem_k, w_mem_v, g_mk, g_mq, w_out):
    depth = w_in.shape[0]
    assert depth == 1
    l = 0
    bp, t_p, _ = x_prompt.shape
    bs, t_s, _ = x_sample.shape
    params = _prep_params(g_norm[l], w_in[l], g_q[l], g_k[l], sinks[l], conv_w[l], conv_b[l], dt_bias[l],
                          a_log[l], d_skip[l], g_ssd[l], g_mq[l], w_out[l])

    mk_p, mv_p = _memory_kv(mem_prompt, g_mem[l], w_mem_k[l], w_mem_v[l], g_mk[l])

    yp, kp, vp, sp, cp = _layer(
        x_prompt, jnp.arange(t_p, dtype=jnp.int32), mk_p, mv_p, None, params, tile=min(PROMPT_TILE, t_p))

    past = (cache_win_k[l].reshape(bs, WINDOW, KV_WIDTH), cache_win_v[l].reshape(bs, WINDOW, KV_WIDTH),
            state_ssm[l].reshape(bs, SSD_WIDTH, SSD_STATE), state_conv[l])
    ys, ks, vs, ss, cs = _layer(
        x_sample, PAST_LEN + jnp.arange(t_s, dtype=jnp.int32),
        cache_mem_k[l], cache_mem_v[l],
        past, params, tile=t_s)

    kv5 = lambda a, b: a.reshape(1, b, WINDOW, N_KV_HEADS, HEAD_DIM)
    ssm5 = lambda a, b: a.reshape(1, b, SSD_HEADS, SSD_HEAD_DIM, SSD_STATE)
    mem5 = lambda a: a[None]
    return (yp, ys,
            kv5(kp, bp), kv5(vp, bp), ssm5(sp, bp), cp[None],
            mem5(mk_p), mem5(mv_p),
            kv5(ks, bs), kv5(vs, bs), ssm5(ss, bs), cs[None])
```

```python
import functools
import math

import numpy as np
import jax
import jax.numpy as jnp
from jax import lax
from jax.experimental import pallas as pl
from jax.experimental.pallas import tpu as pltpu

D_MODEL = 1024
CHUNK = 64
HEAD_DIM = 64
N_Q_HEADS = 8
N_KV_HEADS = 2
ATTN_WIDTH = N_Q_HEADS * HEAD_DIM
KV_WIDTH = N_KV_HEADS * HEAD_DIM
WINDOW = 128
ROPE_THETA = 10000.0
SSD_WIDTH = D_MODEL
SSD_HEAD_DIM = 64
SSD_HEADS = SSD_WIDTH // SSD_HEAD_DIM
SSD_GROUPS = 2
SSD_STATE = 128
CONV_WIDTH = 4
CONV_CH = SSD_WIDTH + 2 * SSD_GROUPS * SSD_STATE
MEM_TOKENS = 256
MEM_HEADS = 4
MEM_HEAD_DIM = 128
MEM_WIDTH = MEM_HEADS * MEM_HEAD_DIM
MIX_WIDTH = 2 * D_MODEL
EPS = 1e-6
PAST_LEN = 2048

LANES = 128
SUBLANES = 8
VMEM_LIMIT_BYTES = 56 * 1024 * 1024

N_GROUPS = 4
GROUP_W = 1280
G0_Q, G0_K, G0_V, G0_GA = 0, ATTN_WIDTH, ATTN_WIDTH + KV_WIDTH, ATTN_WIDTH + 2 * KV_WIDTH
G1_Z, G1_DT = 0, SSD_WIDTH
G2_X, G2_B = 0, SSD_WIDTH
G3_C, G3_MQ, G3_GM = 0, SSD_GROUPS * SSD_STATE, SSD_GROUPS * SSD_STATE + MEM_WIDTH
PIECE_W = 256
N_PIECES = GROUP_W // PIECE_W
HEAD_ROWS = SUBLANES

F32 = jnp.float32
BF16 = jnp.bfloat16
NEG_BIG = -1e30


def _dot(a, b):
    return jnp.dot(a, b, preferred_element_type=F32)


def _dot_nt(a, b):
    return lax.dot_general(a, b, (((1,), (1,)), ((), ())), preferred_element_type=F32)


def _dot_tn(a, b):
    return lax.dot_general(a, b, (((0,), (0,)), ((), ())), preferred_element_type=F32)


def _split3(x):
    h1 = x.astype(BF16)
    r1 = x - h1.astype(F32)
    h2 = r1.astype(BF16)
    r2 = r1 - h2.astype(F32)
    return h1, h2, r2.astype(BF16)


def _dot_sel_rhs(x, sel):
    a, b, c = _split3(x)
    return (_dot(c, sel) + _dot(b, sel)) + _dot(a, sel)


def _dot_sel_lhs(sel, x):
    a, b, c = _split3(x)
    return (_dot(sel, c) + _dot(sel, b)) + _dot(sel, a)


def _dot_sel2(x, sel2):
    hi = x.astype(BF16)
    lo = (x - hi.astype(F32)).astype(BF16)
    return _dot(jnp.concatenate([hi, lo], axis=1), sel2)


def _seg_sumsq(x, seg):
    return _dot((x * x).astype(BF16), seg)


def _silu(x):
    h = 0.5 * x
    return h + h * jnp.tanh(h)


def _softplus(x):
    return jnp.maximum(x, 0.0) + jnp.log(1.0 + jnp.exp(-jnp.abs(x)))


def _rms_scale(x):
    return lax.rsqrt(jnp.mean(x * x, axis=-1, keepdims=True) + EPS)


def _softmax_rows(sc, extra=None):
    m = jnp.max(sc, axis=-1, keepdims=True)
    if extra is not None:
        m = jnp.maximum(m, extra)
    e = jnp.exp(sc - m)
    den = jnp.sum(e, axis=-1, keepdims=True)
    if extra is not None:
        den = den + jnp.exp(extra - m)
    return (e * (1.0 / den)).astype(BF16)


def _rope(x, cos, sin_signed, first_half):
    partner = jnp.where(first_half, pltpu.roll(x, LANES - HEAD_DIM // 2, 1), pltpu.roll(x, HEAD_DIM // 2, 1))
    return x * cos + partner * sin_signed


def _memory_kv_kernel(mem_ref, g_mem_ref, wk_ref, wv_ref, g_mk_ref, k_out, v_out):
    x = mem_ref[0]
    hb = (x * _rms_scale(x) * g_mem_ref[...]).astype(BF16)
    k = _dot(hb, wk_ref[...])
    v = _dot(hb, wv_ref[...])
    for hh in range(MEM_HEADS):
        sl = slice(hh * MEM_HEAD_DIM, (hh + 1) * MEM_HEAD_DIM)
        kh = k[:, sl]
        k_out[0, :, hh, :] = kh * _rms_scale(kh) * g_mk_ref[...]
        v_out[0, :, hh, :] = v[:, sl]


def _memory_kv(mem, g_mem, w_k, w_v, g_mk):
    b = mem.shape[0]
    full = lambda shape: pl.BlockSpec(shape, lambda i: (0,) * len(shape))
    return pl.pallas_call(
        _memory_kv_kernel,
        grid=(b,),
        in_specs=[
            pl.BlockSpec((1, MEM_TOKENS, D_MODEL), lambda i: (i, 0, 0)),
            full((1, D_MODEL)),
            full((D_MODEL, MEM_WIDTH)),
            full((D_MODEL, MEM_WIDTH)),
            full((1, MEM_HEAD_DIM)),
        ],
        out_specs=[
            pl.BlockSpec((1, MEM_TOKENS, MEM_HEADS, MEM_HEAD_DIM), lambda i: (i, 0, 0, 0)),
            pl.BlockSpec((1, MEM_TOKENS, MEM_HEADS, MEM_HEAD_DIM), lambda i: (i, 0, 0, 0)),
        ],
        out_shape=[jax.ShapeDtypeStruct((b, MEM_TOKENS, MEM_HEADS, MEM_HEAD_DIM), F32)] * 2,
        compiler_params=pltpu.CompilerParams(dimension_semantics=("arbitrary",)),
        name="memory_kv",
    )(mem, g_mem.reshape(1, D_MODEL), w_k.astype(BF16), w_v.astype(BF16), g_mk.reshape(1, MEM_HEAD_DIM))


def _layer_kernel(*refs, tile, n_tiles, n_sub, has_past):
    x_first_ref, x_second_ref, x_ahead_ref, cos_ref, sin_ref, memk_ref, memv_ref, *rest = refs
    if has_past:
        wink_ref, winv_ref, ssm_ref, conv_ref, *rest = rest
    else:
        wink_ref = winv_ref = ssm_ref = conv_ref = None
    (w_in_ref, wdt_ref, w_out_ref,
     g_norm_ref, gq_ref, gk_ref, sinks_ref, convw_ref, convb_ref,
     dtb_row_ref, dtb_col_ref, alog_row_ref, alog_col_ref, dskip_ref, g_ssd_ref, g_mq_ref,
     segq_ref, segk_ref, expand_ref, tri_ref, duptri_ref,
     y_ref, newk_ref, newv_ref, ssm_out_ref, conv_out_ref,
     proj_a, proj_b, hb_a, hb_b, xres_a, xres_b,
     qs, kwin, vwin, kr, vr, state_t, mix, memk_bf, memv_bf) = rest

    T = tile
    n_chunks = T // CHUNK
    n_blocks = n_chunks // n_sub
    R = n_sub * CHUNK
    groups_per_block = N_GROUPS // n_blocks
    s = pl.program_id(0)
    t = s % n_tiles
    body_rows = pl.ds(HEAD_ROWS, T)

    def normed_input(x_ref, hb, xres):
        x = x_ref[0]
        hb[...] = (x * _rms_scale(x) * g_norm_ref[...]).astype(BF16)
        xres[...] = x

    def project_piece(hb, proj, g, i):
        cols = slice(i * PIECE_W, (i + 1) * PIECE_W)
        proj[g, body_rows, cols] = _dot(hb[...], w_in_ref[g, :, cols])

    @pl.when(s == 0)
    def _prologue():
        normed_input(x_first_ref, hb_a, xres_a)
        normed_input(x_second_ref, hb_b, xres_b)
        for g in range(N_GROUPS):
            for i in range(N_PIECES):
                project_piece(hb_a, proj_a, g, i)

    lane = lax.broadcasted_iota(jnp.int32, (1, LANES), 1)
    first_half = (lane % HEAD_DIM) < (HEAD_DIM // 2)
    lo_half = lane < HEAD_DIM
    row64 = lax.broadcasted_iota(jnp.int32, (CHUNK, LANES), 0)
    lane64 = lax.broadcasted_iota(jnp.int32, (CHUNK, LANES), 1)
    causal_dup = row64 >= (lane64 % CHUNK)
    lo_half64 = lane64 < CHUNK
    row128 = lax.broadcasted_iota(jnp.int32, (2 * CHUNK, 1), 0)
    key_idx = lax.broadcasted_iota(jnp.int32, (1, WINDOW + CHUNK), 1)
    gw = SSD_WIDTH // SSD_GROUPS
    n_pairs = SSD_HEADS // 2
    pairs_per_group = n_pairs // SSD_GROUPS
    mem_scale = 1.0 / math.sqrt(MEM_HEAD_DIM)
    bc_w = SSD_GROUPS * SSD_STATE

    def build_masked(src, dst, rows):
        full = src[rows, :]
        swapped = pltpu.roll(full, HEAD_DIM, 1)
        zero = jnp.zeros_like(full)
        dst[0, rows, :] = jnp.where(lo_half, full, zero).astype(BF16)
        dst[1, rows, :] = jnp.where(lo_half, zero, swapped).astype(BF16)
        dst[2, rows, :] = jnp.where(lo_half, swapped, zero).astype(BF16)
        dst[3, rows, :] = jnp.where(lo_half, zero, full).astype(BF16)

    def step(proj, hb, xres, proj_nxt, hb_nxt, xres_nxt):
        @pl.when(t == 0)
        def _init():
            zero_head = jnp.zeros((HEAD_ROWS, GROUP_W), F32)
            proj[2, 0:HEAD_ROWS, :] = zero_head
            proj[3, 0:HEAD_ROWS, :] = zero_head
            if has_past:
                kwin[0:WINDOW, :] = wink_ref[0]
                vwin[0:WINDOW, :] = winv_ref[0]
                state_t[...] = ssm_ref[0].T
                tail = conv_ref[0]
                r_lo = HEAD_ROWS - (CONV_WIDTH - 1)
                proj[2, r_lo:HEAD_ROWS, :] = tail[:, 0:GROUP_W]
                proj[3, r_lo:HEAD_ROWS, 0:CONV_CH - GROUP_W] = tail[:, GROUP_W:CONV_CH]
                for src, dst in ((kwin, kr), (vwin, vr)):
                    build_masked(src, dst, slice(0, WINDOW))
            else:
                state_t[...] = jnp.zeros((SSD_STATE, SSD_WIDTH), F32)
                kwin[0:WINDOW, :] = jnp.zeros((WINDOW, KV_WIDTH), F32)
                vwin[0:WINDOW, :] = jnp.zeros((WINDOW, KV_WIDTH), F32)
                for v in range(4):
                    kr[v, 0:WINDOW, :] = jnp.zeros((WINDOW, KV_WIDTH), BF16)
                    vr[v, 0:WINDOW, :] = jnp.zeros((WINDOW, KV_WIDTH), BF16)
            for hh in range(MEM_HEADS):
                sl = slice(hh * MEM_HEAD_DIM, (hh + 1) * MEM_HEAD_DIM)
                memk_bf[:, sl] = memk_ref[0, :, hh, :].astype(BF16)
                memv_bf[:, sl] = memv_ref[0, :, hh, :].astype(BF16)

        if n_tiles > 1:
            @pl.when(t > 0)
            def _shift():
                kwin[0:WINDOW, :] = kwin[T:T + WINDOW, :]
                vwin[0:WINDOW, :] = vwin[T:T + WINDOW, :]
                for v in range(4):
                    kr[v, 0:WINDOW, :] = kr[v, T:T + WINDOW, :]
                    vr[v, 0:WINDOW, :] = vr[v, T:T + WINDOW, :]
                proj[2, 0:HEAD_ROWS, :] = proj_nxt[2, T:T + HEAD_ROWS, :]
                proj[3, 0:HEAD_ROWS, :] = proj_nxt[3, T:T + HEAD_ROWS, :]

        a_col = -jnp.exp(alog_col_ref[...])
        a_row =-jnp.exp(alog_row_ref[...])
        tri = tri_ref[...]
        d_skip = dskip_ref[...]
        subs = range(n_sub)

        def block_body(i, carry):
            static = isinstance(i, int)
            r0 = i * R if static else pl.multiple_of(i * R, R)
            rows = pl.ds(r0, R)
            prow = pl.ds(r0 + HEAD_ROWS, R)
            wrow = pl.ds(r0 + WINDOW, R)
            sub_rows = [slice(u * CHUNK, (u + 1) * CHUNK) for u in subs]
            krows = [pl.ds(r0 + u * CHUNK, WINDOW + CHUNK) for u in subs]

            pieces = [(i * groups_per_block + gg, pp) for gg in range(groups_per_block) for pp in range(N_PIECES)]
            per_gap = len(pieces) // N_PIECES

            def issue(n):
                for _ in range(n):
                    g, pp = pieces.pop(0)
                    project_piece(hb_nxt, proj_nxt, g, pp)

            issue(per_gap)

            q = proj[0, prow, G0_Q:G0_Q + ATTN_WIDTH]
            k = proj[0, prow, G0_K:G0_K + KV_WIDTH]
            dt_row = _softplus(proj[1, prow, G1_DT:G1_DT + LANES] + dtb_row_ref[...])
            mem_q = []
            for hh in range(MEM_HEADS):
                mqh = proj[3, prow, G3_MQ + hh * MEM_HEAD_DIM:G3_MQ + (hh + 1) * MEM_HEAD_DIM]
                mem_q.append((mqh * _rms_scale(mqh) * g_mq_ref[...]).astype(BF16))

            seg = segq_ref[...]
            q_blocks = [q[:, j * 2 * LANES:(j + 1) * 2 * LANES] for j in range(ATTN_WIDTH // (2 * LANES))]
            q_ss = [_seg_sumsq(qj, seg) for qj in q_blocks]
            k_ss = _seg_sumsq(k, segk_ref[...])
            dtx = _dot_sel2(dt_row, expand_ref[...])
            acum_n = _dot_sel_lhs(tri, dt_row * a_row)
            mem_sc = [_dot_nt(mem_q[hh], memk_bf[:, hh * MEM_HEAD_DIM:(hh + 1) * MEM_HEAD_DIM])
                      for hh in range(MEM_HEADS)]
            dt_t = _dot_nt(wdt_ref[...], hb[rows, :])

            issue(per_gap)

            def conv(group, c0, width, ch0):
                acc = convb_ref[:, ch0:ch0 + width]
                blk = proj[group, pl.ds(r0, HEAD_ROWS + R), c0:c0 + width]
                for j in range(CONV_WIDTH):
                    back = CONV_WIDTH - 1 - j
                    tap = blk if back == 0 else pltpu.roll(blk, back, 0)
                    acc = acc + tap[HEAD_ROWS:HEAD_ROWS + R, :] * convw_ref[j:j + 1, ch0:ch0 + width]
                return _silu(acc)

            b_all = conv(2, G2_B, bc_w, SSD_WIDTH).astype(BF16)
            c_all = conv(3, G3_C, bc_w, SSD_WIDTH + bc_w).astype(BF16)
            xs = conv(2, G2_X, SSD_WIDTH, 0)

            cos = cos_ref[rows, :]
            sin_s = sin_ref[rows, :]
            for j, (qj, ss) in enumerate(zip(q_blocks, q_ss)):
                qn = qj * lax.rsqrt(ss * (1.0 / HEAD_DIM) + EPS) * gq_ref[:, j * 2 * LANES:(j + 1) * 2 * LANES]
                for jj in range(2):
                    c0 = j * 2 * LANES + jj * LANES
                    qs[rows, c0:c0 + LANES] = _rope(qn[:, jj * LANES:(jj + 1) * LANES], cos, sin_s,
                                                    first_half).astype(BF16)
            kn = k * lax.rsqrt(k_ss * (1.0 / HEAD_DIM) + EPS) * gk_ref[...]
            kwin[wrow, :] = _rope(kn, cos, sin_s, first_half)
            vwin[wrow, :] = proj[0, prow, G0_V:G0_V + KV_WIDTH]
            build_masked(kwin, kr, wrow)
            build_masked(vwin, vr, wrow)

            mem_p = [_softmax_rows(mem_sc[hh] * mem_scale) for hh in range(MEM_HEADS)]
            xdt = xs * dtx
            da_t3 = _split3(_softplus(dt_t + dtb_col_ref[...]) * a_col)

            bgs = [[b_all[sub_rows[u], g * SSD_STATE:(g + 1) * SSD_STATE] for g in range(SSD_GROUPS)] for u in subs]
            cgs = [[c_all[sub_rows[u], g * SSD_STATE:(g + 1) * SSD_STATE] for g in range(SSD_GROUPS)] for u in subs]
            cbs = [[_dot_nt(cgs[u][g], jnp.concatenate([bgs[u][g], bgs[u][g]], axis=0))
                    for g in range(SSD_GROUPS)] for u in subs]
            cross_cb = [_dot_nt(cgs[1][g], bgs[0][g]) for g in range(SSD_GROUPS)] if n_sub == 2 else None
            y_in = [_dot(c_all[:, g * SSD_STATE:(g + 1) * SSD_STATE], state_t[:, g * gw:(g + 1) * gw].astype(BF16))
                    for g in range(SSD_GROUPS)]
            acum = _dot_sel2(acum_n, expand_ref[...])
            dup = duptri_ref[...]
            acum_t = (_dot(da_t3[2], dup) + _dot(da_t3[1], dup)) + _dot(da_t3[0], dup)
            att_sc = []
            for u in subs:
                for g in range(N_KV_HEADS):
                    base = g * 2 * LANES
                    qrows = pl.ds(r0 + u * CHUNK, CHUNK)
                    qg = jnp.concatenate([qs[qrows, base:base + LANES], qs[qrows, base + LANES:base + 2 * LANES]],
                                         axis=0)
                    for pos in range(2):
                        att_sc.append(_dot_nt(qg, kr[2 * g + pos, krows[u], :]))
            mem_o = [_dot(mem_p[hh], memv_bf[:, hh * MEM_HEAD_DIM:(hh + 1) * MEM_HEAD_DIM])
                     for hh in range(MEM_HEADS)]

            issue(per_gap)

            att_p = []
            for u in subs:
                if not has_past:
                    first_valid = jnp.where(t == 0, WINDOW - (r0 + u * CHUNK), 0)
                    key_ok = key_idx >= first_valid
                for g in range(N_KV_HEADS):
                    for pos in range(2):
                        hd0 = 4 * g + pos
                        sink = jnp.where(row128 < CHUNK, sinks_ref[hd0], sinks_ref[hd0 + 2])
                        sc = att_sc[(u * N_KV_HEADS + g) * 2 + pos]
                        if not has_past:
                            sc = jnp.where(key_ok, sc, NEG_BIG)
                        att_p.append(_softmax_rows(sc, sink))

            exp_acum = jnp.exp(acum)
            a_last = [acum[u * CHUNK + CHUNK - 1:(u + 1) * CHUNK, :] for u in subs]
            dec_end = [jnp.exp(a_last[u]) for u in subs]
            x_end = [(xdt[sub_rows[u], :] * jnp.exp(a_last[u] - acum[sub_rows[u], :])).astype(BF16) for u in subs]
            ws, xbds = [], []
            zero_bd = jnp.zeros((2 * CHUNK, LANES), BF16)
            for u in subs:
                act = acum_t[:, u * LANES:(u + 1) * LANES]
                for quad in range(n_pairs // 2):
                    w_parts, bd_parts = [], []
                    for pair in (2 * quad, 2 * quad + 1):
                        g = pair // pairs_per_group
                        cols = slice(pair * LANES, (pair + 1) * LANES)
                        rowb = jnp.where(lane < CHUNK, act[2 * pair:2 * pair + 1, :], act[2 * pair + 1:2 * pair + 2, :])
                        dec = jnp.exp(jnp.where(causal_dup, acum[sub_rows[u], cols] - rowb, -jnp.inf))
                        w_parts.append((cbs[u][g] * dec).astype(BF16))
                        xp = xdt[sub_rows[u], cols]
                        zero = jnp.zeros_like(xp)
                        bd_parts.append(jnp.concatenate(
                            [jnp.where(lo_half64, xp, zero), jnp.where(lo_half64, zero, xp)], axis=0).astype(BF16))
                    ws.append(jnp.concatenate(w_parts, axis=1))
                    xbds.append(jnp.concatenate(
                        [jnp.concatenate([bd_parts[0], zero_bd], axis=1),
                         jnp.concatenate([zero_bd, bd_parts[1]], axis=1)], axis=0))

            for hh in range(MEM_HEADS):
                gm = proj[3, prow, G3_GM + hh * MEM_HEAD_DIM:G3_GM + (hh + 1) * MEM_HEAD_DIM]
                c0 = ATTN_WIDTH + SSD_WIDTH + hh * MEM_HEAD_DIM
                mix[rows, c0:c0 + MEM_HEAD_DIM] = (mem_o[hh] * _silu(gm)).astype(BF16)

            att_o = []
            for u in subs:
                for g in range(N_KV_HEADS):
                    j0 = (u * N_KV_HEADS + g) * 2
                    att_o.append(_dot(att_p[j0], vr[2 * g, krows[u], :]) + _dot(att_p[j0 + 1], vr[2 * g + 1, krows[u], :]))
            for g in range(SSD_GROUPS):
                gcols = slice(g * gw, (g + 1) * gw)
                st = state_t[:, gcols]
                for u in subs:
                    st = st * dec_end[u][:, gcols] + _dot_tn(bgs[u][g], x_end[u][:, gcols])
                state_t[:, gcols] = st
            cross = ([_dot(cross_cb[g].astype(BF16), x_end[0][:, g * gw:(g + 1) * gw]) for g in range(SSD_GROUPS)]
                     if n_sub == 2 else None)
            y_diag = [_dot(w, xbd) for w, xbd in zip(ws, xbds)]

            issue(len(pieces))

            for u in subs:
                for g in range(N_KV_HEADS):
                    base = g * 2 * LANES
                    orow = pl.ds(r0 + u * CHUNK, CHUNK)
                    ga = proj[0, pl.ds(r0 + u * CHUNK + HEAD_ROWS, CHUNK), G0_GA + base:G0_GA + base + 2 * LANES]
                    o = att_o[u * N_KV_HEADS + g]
                    mix[orow, base:base + LANES] = (o[0:CHUNK] * _silu(ga[:, 0:LANES])).astype(BF16)
                    mix[orow, base + LANES:base + 2 * LANES] = (o[CHUNK:] * _silu(ga[:, LANES:])).astype(BF16)

            y_rows = []
            for u in subs:
                ys = []
                for pair in range(n_pairs):
                    g, hp = divmod(pair, pairs_per_group)
                    cols = slice(pair * LANES, (pair + 1) * LANES)
                    lc = slice(hp * LANES, (hp + 1) * LANES)
                    y_off = y_in[g][sub_rows[u], lc]
                    if u == 1:
                        y_off = y_off * dec_end[0][:, cols] + cross[g][:, lc]
                    yd = y_diag[(u * n_pairs + pair) // 2][:, (pair % 2) * LANES:(pair % 2 + 1) * LANES]
                    ys.append(yd + y_off * exp_acum[sub_rows[u], cols] + d_skip[:, cols] * xs[sub_rows[u], cols])
                y_rows.append(jnp.concatenate(ys, axis=1))
            y_all = y_rows[0] if n_sub == 1 else jnp.concatenate(y_rows, axis=0)
            z = proj[1, prow, G1_Z:G1_Z + SSD_WIDTH]
            yg = y_all * _silu(z)
            mix[rows, ATTN_WIDTH:ATTN_WIDTH + SSD_WIDTH] = (yg * _rms_scale(yg) * g_ssd_ref[...]).astype(BF16)

            xa = x_ahead_ref[0, rows, :]
            hb[rows, :] = (xa * _rms_scale(xa) * g_norm_ref[...]).astype(BF16)
            return carry

        if n_blocks == 1:
            block_body(0, 0)
        else:
            lax.fori_loop(0, n_blocks, block_body, 0)

        y_ref[0] = xres[...] + _dot(mix[...], w_out_ref[...])
        xres[...] = x_ahead_ref[0]

        @pl.when(t == n_tiles - 1)
        def _final():
            newk_ref[0] = kwin[T:T + WINDOW, :]
            newv_ref[0] = vwin[T:T + WINDOW, :]
            ssm_out_ref[0] = state_t[...].T
            r_lo = T + HEAD_ROWS - (CONV_WIDTH - 1)
            conv_out_ref[0, :, 0:GROUP_W] = proj[2, r_lo:T + HEAD_ROWS, :]
            conv_out_ref[0, :, GROUP_W:CONV_CH] = proj[3, r_lo:T + HEAD_ROWS, 0:CONV_CH - GROUP_W]

    parity = s % 2

    @pl.when(parity == 0)
    def _even():
        step(proj_a, hb_a, xres_a, proj_b, hb_b, xres_b)

    @pl.when(parity == 1)
    def _odd():
        step(proj_b, hb_b, xres_b, proj_a, hb_a, xres_a)


def _constants(tile, n_sub):
    n_chunks = tile // CHUNK
    rows_per_block = n_sub * CHUNK
    idx = np.arange
    segq = (idx(2 * LANES)[:, None] // HEAD_DIM == idx(2 * LANES)[None, :] // HEAD_DIM)
    segk = (idx(KV_WIDTH)[:, None] // HEAD_DIM == idx(KV_WIDTH)[None, :] // HEAD_DIM)
    expand = (idx(2 * LANES)[:, None] % LANES == idx(SSD_WIDTH)[None, :] // SSD_HEAD_DIM)
    r = idx(rows_per_block)
    tri = (r[:, None] // CHUNK == r[None, :] // CHUNK) & (r[:, None] >= r[None, :])
    src = idx(rows_per_block)[:, None]
    dst = idx(n_sub * LANES)[None, :]
    duptri = (src // CHUNK == dst // LANES) & (src % CHUNK <= dst % CHUNK)
    return [jnp.asarray(m.astype(np.float32), dtype=BF16) for m in (segq, segk, expand, tri, duptri)]


def _layer(x, positions, mem_k, mem_v, past, params, *, tile):
    b, seq, _ = x.shape
    n_t = seq // tile
    n_steps = b * n_t
    has_past = past is not None
    n_chunks = tile // CHUNK
    n_sub = 2 if n_chunks % 2 == 0 else 1
    assert seq % tile == 0 and tile % CHUNK == 0 and N_GROUPS % (n_chunks // n_sub) == 0
    assert n_t == 1 or tile >= WINDOW

    half = HEAD_DIM // 2
    freqs = ROPE_THETA ** (-jnp.arange(half, dtype=F32) / half)
    ang = positions.astype(F32)[:, None] * freqs[None, :]
    cos, sin = jnp.cos(ang), jnp.sin(ang)
    cos_t = jnp.tile(cos, (1, LANES // half))
    sin_t = jnp.tile(jnp.concatenate([-sin, sin], axis=1), (1, LANES // HEAD_DIM))

    consts = _constants(tile, n_sub)

    def full(a):
        nd = a.ndim
        return pl.BlockSpec(a.shape, lambda i, _n=nd: (0,) * _n, pipeline_mode=pl.Buffered(1))

    def per_row(a):
        nd = a.ndim
        return pl.BlockSpec((1,) + a.shape[1:], lambda i, _n=nd: (i // n_t,) + (0,) * (_n - 1))

    def tile_at(j):
        j = jnp.minimum(j, n_steps - 1)
        return (j // n_t, j % n_t, 0)

    vec_params = [params[k] for k in ("g_norm", "gq", "gk")]
    tail_params = [params[k] for k in ("conv_w", "conv_b", "dtb_row", "dtb_col", "alog_row", "alog_col",
                                       "d_skip", "g_ssd", "g_mq")]
    past_inputs = list(past) if has_past else []
    inputs = [x, x, x, cos_t, sin_t, mem_k, mem_v, *past_inputs,
              params["w_in"], params["w_dt"], params["w_out"],
              *vec_params, params["sinks"], *tail_params, *consts]
    in_specs = [
        pl.BlockSpec((1, tile, D_MODEL), lambda i: tile_at(0)),
        pl.BlockSpec((1, tile, D_MODEL), lambda i: tile_at(1)),
        pl.BlockSpec((1, tile, D_MODEL), lambda i: tile_at(i + 2)),
        pl.BlockSpec((tile, LANES), lambda i: (i % n_t, 0)),
        pl.BlockSpec((tile, LANES), lambda i: (i % n_t, 0)),
        per_row(mem_k), per_row(mem_v),
        *[per_row(p) for p in past_inputs],
        full(params["w_in"]), full(params["w_dt"]), full(params["w_out"]),
        *[full(p) for p in vec_params],
        pl.BlockSpec(memory_space=pltpu.SMEM),
        *[full(p) for p in tail_params],
        *[full(c) for c in consts],
    ]
    out_shape = [
        jax.ShapeDtypeStruct((b, seq, D_MODEL), F32),
        jax.ShapeDtypeStruct((b, WINDOW, KV_WIDTH), F32),
        jax.ShapeDtypeStruct((b, WINDOW, KV_WIDTH), F32),
        jax.ShapeDtypeStruct((b, SSD_WIDTH, SSD_STATE), F32),
        jax.ShapeDtypeStruct((b, CONV_WIDTH - 1, CONV_CH), F32),
    ]
    out_specs = [
        pl.BlockSpec((1, tile, D_MODEL), lambda i: (i // n_t, i % n_t, 0)),
        pl.BlockSpec((1, WINDOW, KV_WIDTH), lambda i: (i // n_t, 0, 0)),
        pl.BlockSpec((1, WINDOW, KV_WIDTH), lambda i: (i // n_t, 0, 0)),
        pl.BlockSpec((1, SSD_WIDTH, SSD_STATE), lambda i: (i // n_t, 0, 0)),
        pl.BlockSpec((1, CONV_WIDTH - 1, CONV_CH), lambda i: (i // n_t, 0, 0)),
    ]
    proj_shape = (N_GROUPS, HEAD_ROWS + tile, GROUP_W)
    scratch = [
        pltpu.VMEM(proj_shape, F32), pltpu.VMEM(proj_shape, F32),
        pltpu.VMEM((tile, D_MODEL), BF16), pltpu.VMEM((tile, D_MODEL), BF16),
        pltpu.VMEM((tile, D_MODEL), F32), pltpu.VMEM((tile, D_MODEL), F32),
        pltpu.VMEM((tile, ATTN_WIDTH), BF16),
        pltpu.VMEM((WINDOW + tile, KV_WIDTH), F32),
        pltpu.VMEM((WINDOW + tile, KV_WIDTH), F32),
        pltpu.VMEM((4, WINDOW + tile, KV_WIDTH), BF16),
        pltpu.VMEM((4, WINDOW + tile, KV_WIDTH), BF16),
        pltpu.VMEM((SSD_STATE, SSD_WIDTH), F32),
        pltpu.VMEM((tile, MIX_WIDTH), BF16),
        pltpu.VMEM((MEM_TOKENS, MEM_WIDTH), BF16),
        pltpu.VMEM((MEM_TOKENS, MEM_WIDTH), BF16),
    ]
    return pl.pallas_call(
        functools.partial(_layer_kernel, tile=tile, n_tiles=n_t, n_sub=n_sub, has_past=has_past),
        grid=(n_steps,),
        in_specs=in_specs,
        out_specs=out_specs,
        out_shape=out_shape,
        scratch_shapes=scratch,
        compiler_params=pltpu.CompilerParams(
            dimension_semantics=("arbitrary",),
            vmem_limit_bytes=VMEM_LIMIT_BYTES),
        name="layer_sample" if has_past else "layer_prompt",
    )(*inputs)


def _prep_params(g_norm, w_in, g_q, g_k, sinks, conv_w, conv_b, dt_bias, a_log, d_skip, g_ssd, g_mq, w_out):
    o_z = ATTN_WIDTH + 2 * KV_WIDTH + ATTN_WIDTH
    o_x = o_z + SSD_WIDTH
    o_b = o_x + SSD_WIDTH
    o_c = o_b + SSD_GROUPS * SSD_STATE
    o_dt = o_c + SSD_GROUPS * SSD_STATE
    o_mq = o_dt + SSD_HEADS
    w_dt = w_in[:, o_dt:o_mq]
    zeros = lambda n: jnp.zeros((D_MODEL, n), w_in.dtype)
    groups = [
        w_in[:, 0:o_z],
        jnp.concatenate([w_in[:, o_z:o_x], w_dt, zeros(GROUP_W - SSD_WIDTH - SSD_HEADS)], axis=1),
        w_in[:, o_x:o_c],
        jnp.concatenate([w_in[:, o_c:o_dt], w_in[:, o_mq:]], axis=1),
    ]
    pad = LANES - SSD_HEADS
    return {
        "w_in": jnp.stack(groups).astype(BF16),
        "w_dt": w_dt.T.astype(BF16),
        "w_out": w_out.astype(BF16),
        "g_norm": g_norm.reshape(1, D_MODEL),
        "gq": (jnp.tile(g_q, N_Q_HEADS) * (1.0 / math.sqrt(HEAD_DIM))).reshape(1, ATTN_WIDTH),
        "gk": jnp.tile(g_k, N_KV_HEADS).reshape(1, KV_WIDTH),
        "sinks": sinks,
        "conv_w": conv_w,
        "conv_b": conv_b.reshape(1, CONV_CH),
        "dtb_row": jnp.pad(dt_bias, (0, pad)).reshape(1, LANES),
        "dtb_col": dt_bias.reshape(SSD_HEADS, 1),
        "alog_row": jnp.pad(a_log, (0, pad)).reshape(1, LANES),
        "alog_col": a_log.reshape(SSD_HEADS, 1),
        "d_skip": jnp.repeat(d_skip, SSD_HEAD_DIM).reshape(1, SSD_WIDTH),
        "g_ssd": g_ssd.reshape(1, SSD_WIDTH),
        "g_mq": g_mq.reshape(1, MEM_HEAD_DIM),
    }


PROMPT_TILE = 256


def kernel(x_prompt, x_sample, cache_win_k, cache_win_v, state_ssm, state_conv, cache_mem_k, cache_mem_v,
           mem_prompt, g_norm, w_in, g_q, g_k, sinks, conv_w, conv_b, dt_bias, a_log, d_skip, g_ssd,
           g_mem, w_mem_k, w_mem_v, g_mk, g_mq, w_out):
    depth = w_in.shape[0]
    assert depth == 1
    l = 0
    bp, t_p, _ = x_prompt.shape
    bs, t_s, _ = x_sample.shape
    params = _prep_params(g_norm[l], w_in[l], g_q[l], g_k[l], sinks[l], conv_w[l], conv_b[l], dt_bias[l],
                          a_log[l], d_skip[l], g_ssd[l], g_mq[l], w_out[l])

    mk_p, mv_p = _memory_kv(mem_prompt, g_mem[l], w_mem_k[l], w_mem_v[l], g_mk[l])

    yp, kp, vp, sp, cp = _layer(
        x_prompt, jnp.arange(t_p, dtype=jnp.int32), mk_p, mv_p, None, params, tile=min(PROMPT_TILE, t_p))

    past = (cache_win_k[l].reshape(bs, WINDOW, KV_WIDTH), cache_win_v[l].reshape(bs, WINDOW, KV_WIDTH),
            state_ssm[l].reshape(bs, SSD_WIDTH, SSD_STATE), state_conv[l])
    ys, ks, vs, ss, cs = _layer(
        x_sample, PAST_LEN + jnp.arange(t_s, dtype=jnp.int32),
        cache_mem_k[l], cache_mem_v[l],
        past, params, tile=t_s)

    kv5 = lambda a, b: a.reshape(1, b, WINDOW, N_KV_HEADS, HEAD_DIM)
    ssm5 = lambda a, b: a.reshape(1, b, SSD_HEADS, SSD_HEAD_DIM, SSD_STATE)
    mem5 = lambda a: a[None]
    return (yp, ys,
            kv5(kp, bp), kv5(vp, bp), ssm5(sp, bp), cp[None],
            mem5(mk_p), mem5(mv_p),
            kv5(ks, bs), kv5(vs, bs), ssm5(ss, bs), cs[None])
```

```python
import functools
import math

import numpy as np
import jax
import jax.numpy as jnp
from jax import lax
from jax.experimental import pallas as pl
from jax.experimental.pallas import tpu as pltpu

D_MODEL = 1024
CHUNK = 64
HEAD_DIM = 64
N_Q_HEADS = 8
N_KV_HEADS = 2
ATTN_WIDTH = N_Q_HEADS * HEAD_DIM
KV_WIDTH = N_KV_HEADS * HEAD_DIM
WINDOW = 128
ROPE_THETA = 10000.0
SSD_WIDTH = D_MODEL
SSD_HEAD_DIM = 64
SSD_HEADS = SSD_WIDTH // SSD_HEAD_DIM
SSD_GROUPS = 2
SSD_STATE = 128
CONV_WIDTH = 4
CONV_CH = SSD_WIDTH + 2 * SSD_GROUPS * SSD_STATE
MEM_TOKENS = 256
MEM_HEADS = 4
MEM_HEAD_DIM = 128
MEM_WIDTH = MEM_HEADS * MEM_HEAD_DIM
MIX_WIDTH = 2 * D_MODEL
EPS = 1e-6
PAST_LEN = 2048

LANES = 128
SUBLANES = 8
VMEM_LIMIT_BYTES = 56 * 1024 * 1024

N_GROUPS = 4
GROUP_W = 1280
G0_Q, G0_K, G0_V, G0_GA = 0, ATTN_WIDTH, ATTN_WIDTH + KV_WIDTH, ATTN_WIDTH + 2 * KV_WIDTH
G1_Z, G1_DT = 0, SSD_WIDTH
G2_X, G2_B = 0, SSD_WIDTH
G3_C, G3_MQ, G3_GM = 0, SSD_GROUPS * SSD_STATE, SSD_GROUPS * SSD_STATE + MEM_WIDTH
PIECE_W = 256
N_PIECES = GROUP_W // PIECE_W
OUT_PIECES = D_MODEL // PIECE_W
HEAD_ROWS = SUBLANES

F32 = jnp.float32
BF16 = jnp.bfloat16
NEG_BIG = -1e30


def _dot(a, b):
    return jnp.dot(a, b, preferred_element_type=F32)


def _dot_nt(a, b):
    return lax.dot_general(a, b, (((1,), (1,)), ((), ())), preferred_element_type=F32)


def _dot_tn(a, b):
    return lax.dot_general(a, b, (((0,), (0,)), ((), ())), preferred_element_type=F32)


def _split3(x):
    h1 = x.astype(BF16)
    r1 = x - h1.astype(F32)
    h2 = r1.astype(BF16)
    r2 = r1 - h2.astype(F32)
    return h1, h2, r2.astype(BF16)


def _dot_sel_rhs(x, sel):
    a, b, c = _split3(x)
    return (_dot(c, sel) + _dot(b, sel)) + _dot(a, sel)


def _dot_sel_lhs(sel, x):
    a, b, c = _split3(x)
    return (_dot(sel, c) + _dot(sel, b)) + _dot(sel, a)


def _dot_sel2(x, sel2):
    hi = x.astype(BF16)
    lo = (x - hi.astype(F32)).astype(BF16)
    return _dot(jnp.concatenate([hi, lo], axis=1), sel2)


def _seg_sumsq(x, seg):
    return _dot((x * x).astype(BF16), seg)


def _silu(x):
    h = 0.5 * x
    return h + h * jnp.tanh(h)


def _softplus(x):
    return jnp.maximum(x, 0.0) + jnp.log(1.0 + jnp.exp(-jnp.abs(x)))


def _rms_scale(x):
    return lax.rsqrt(jnp.mean(x * x, axis=-1, keepdims=True) + EPS)


def _softmax_rows(sc, extra=None):
    m = jnp.max(sc, axis=-1, keepdims=True)
    if extra is not None:
        m = jnp.maximum(m, extra)
    e = jnp.exp(sc - m)
    den = jnp.sum(e, axis=-1, keepdims=True)
    if extra is not None:
        den = den + jnp.exp(extra - m)
    return (e * (1.0 / den)).astype(BF16)


def _rope(x, cos, sin_signed, first_half):
    partner = jnp.where(first_half, pltpu.roll(x, LANES - HEAD_DIM // 2, 1), pltpu.roll(x, HEAD_DIM // 2, 1))
    return x * cos + partner * sin_signed


def _memory_kv_kernel(mem_ref, g_mem_ref, wk_ref, wv_ref, g_mk_ref, k_out, v_out):
    x = mem_ref[0]
    hb = (x * _rms_scale(x) * g_mem_ref[...]).astype(BF16)
    k = _dot(hb, wk_ref[...])
    v = _dot(hb, wv_ref[...])
    for hh in range(MEM_HEADS):
        sl = slice(hh * MEM_HEAD_DIM, (hh + 1) * MEM_HEAD_DIM)
        kh = k[:, sl]
        k_out[0, :, hh, :] = kh * _rms_scale(kh) * g_mk_ref[...]
        v_out[0, :, hh, :] = v[:, sl]


def _memory_kv(mem, g_mem, w_k, w_v, g_mk):
    b = mem.shape[0]
    full = lambda shape: pl.BlockSpec(shape, lambda i: (0,) * len(shape))
    return pl.pallas_call(
        _memory_kv_kernel,
        grid=(b,),
        in_specs=[
            pl.BlockSpec((1, MEM_TOKENS, D_MODEL), lambda i: (i, 0, 0)),
            full((1, D_MODEL)),
            full((D_MODEL, MEM_WIDTH)),
            full((D_MODEL, MEM_WIDTH)),
            full((1, MEM_HEAD_DIM)),
        ],
        out_specs=[
            pl.BlockSpec((1, MEM_TOKENS, MEM_HEADS, MEM_HEAD_DIM), lambda i: (i, 0, 0, 0)),
            pl.BlockSpec((1, MEM_TOKENS, MEM_HEADS, MEM_HEAD_DIM), lambda i: (i, 0, 0, 0)),
        ],
        out_shape=[jax.ShapeDtypeStruct((b, MEM_TOKENS, MEM_HEADS, MEM_HEAD_DIM), F32)] * 2,
        compiler_params=pltpu.CompilerParams(dimension_semantics=("arbitrary",)),
        name="memory_kv",
    )(mem, g_mem.reshape(1, D_MODEL), w_k.astype(BF16), w_v.astype(BF16), g_mk.reshape(1, MEM_HEAD_DIM))


def _layer_kernel(*refs, tile, n_tiles, n_sub, has_past):
    x_first_ref, x_second_ref, x_ahead_ref, cos_ref, sin_ref, memk_ref, memv_ref, *rest = refs
    if has_past:
        wink_ref, winv_ref, ssm_ref, conv_ref, *rest = rest
    else:
        wink_ref = winv_ref = ssm_ref = conv_ref = None
    (w_in_ref, wdt_ref, w_out_ref,
     g_norm_ref, gq_ref, gk_ref, sinks_ref, convw_ref, convb_ref,
     dtb_row_ref, dtb_col_ref, alog_row_ref, alog_col_ref, dskip_ref, g_ssd_ref, g_mq_ref,
     segq_ref, segk_ref, expand_ref, tri_ref, duptri_ref,
     y_ref, newk_ref, newv_ref, ssm_out_ref, conv_out_ref,
     proj_a, proj_b, hb_a, hb_b, xres_a, xres_b,
     qs, kwin, vwin, kr, vr, state_t, mix, memk_bf, memv_bf) = rest

    T = tile
    n_chunks = T // CHUNK
    n_blocks = n_chunks // n_sub
    R = n_sub * CHUNK
    groups_per_block = N_GROUPS // n_blocks
    s = pl.program_id(0)
    t = s % n_tiles
    body_rows = pl.ds(HEAD_ROWS, T)

    def normed_input(x_ref, hb, xres):
        x = x_ref[0]
        hb[...] = (x * _rms_scale(x) * g_norm_ref[...]).astype(BF16)
        xres[...] = x

    def project_piece(hb, proj, g, i):
        cols = slice(i * PIECE_W, (i + 1) * PIECE_W)
        proj[g, body_rows, cols] = _dot(hb[...], w_in_ref[g, :, cols])

    @pl.when(s == 0)
    def _prologue():
        normed_input(x_first_ref, hb_a, xres_a)
        normed_input(x_second_ref, hb_b, xres_b)
        for g in range(N_GROUPS):
            for i in range(N_PIECES):
                project_piece(hb_a, proj_a, g, i)

    lane = lax.broadcasted_iota(jnp.int32, (1, LANES), 1)
    first_half = (lane % HEAD_DIM) < (HEAD_DIM // 2)
    lo_half = lane < HEAD_DIM
    row64 = lax.broadcasted_iota(jnp.int32, (CHUNK, LANES), 0)
    lane64 = lax.broadcasted_iota(jnp.int32, (CHUNK, LANES), 1)
    causal_dup = row64 >= (lane64 % CHUNK)
    lo_half64 = lane64 < CHUNK
    row128 = lax.broadcasted_iota(jnp.int32, (2 * CHUNK, 1), 0)
    key_idx = lax.broadcasted_iota(jnp.int32, (1, WINDOW + CHUNK), 1)
    gw = SSD_WIDTH // SSD_GROUPS
    n_pairs = SSD_HEADS // 2
    pairs_per_group = n_pairs // SSD_GROUPS
    mem_scale = 1.0 / math.sqrt(MEM_HEAD_DIM)
    bc_w = SSD_GROUPS * SSD_STATE

    def build_masked(src, dst, rows):
        full = src[rows, :]
        swapped = pltpu.roll(full, HEAD_DIM, 1)
        zero = jnp.zeros_like(full)
        dst[0, rows, :] = jnp.where(lo_half, full, zero).astype(BF16)
        dst[1, rows, :] = jnp.where(lo_half, zero, swapped).astype(BF16)
        dst[2, rows, :] = jnp.where(lo_half, swapped, zero).astype(BF16)
        dst[3, rows, :] = jnp.where(lo_half, zero, full).astype(BF16)

    def step(proj, hb, xres, proj_nxt, hb_nxt, xres_nxt):
        @pl.when(t == 0)
        def _init():
            zero_head = jnp.zeros((HEAD_ROWS, GROUP_W), F32)
            proj[2, 0:HEAD_ROWS, :] = zero_head
            proj[3, 0:HEAD_ROWS, :] = zero_head
            if has_past:
                kwin[0:WINDOW, :] = wink_ref[0]
                vwin[0:WINDOW, :] = winv_ref[0]
                state_t[...] = ssm_ref[0].T
                tail = conv_ref[0]
                r_lo = HEAD_ROWS - (CONV_WIDTH - 1)
                proj[2, r_lo:HEAD_ROWS, :] = tail[:, 0:GROUP_W]
                proj[3, r_lo:HEAD_ROWS, 0:CONV_CH - GROUP_W] = tail[:, GROUP_W:CONV_CH]
                for src, dst in ((kwin, kr), (vwin, vr)):
                    build_masked(src, dst, slice(0, WINDOW))
            else:
                state_t[...] = jnp.zeros((SSD_STATE, SSD_WIDTH), F32)
                kwin[0:WINDOW, :] = jnp.zeros((WINDOW, KV_WIDTH), F32)
                vwin[0:WINDOW, :] = jnp.zeros((WINDOW, KV_WIDTH), F32)
                for v in range(4):
                    kr[v, 0:WINDOW, :] = jnp.zeros((WINDOW, KV_WIDTH), BF16)
                    vr[v, 0:WINDOW, :] = jnp.zeros((WINDOW, KV_WIDTH), BF16)
            for hh in range(MEM_HEADS):
                sl = slice(hh * MEM_HEAD_DIM, (hh + 1) * MEM_HEAD_DIM)
                memk_bf[:, sl] = memk_ref[0, :, hh, :].astype(BF16)
                memv_bf[:, sl] = memv_ref[0, :, hh, :].astype(BF16)

        if n_tiles > 1:
            @pl.when(t > 0)
            def _shift():
                kwin[0:WINDOW, :] = kwin[T:T + WINDOW, :]
                vwin[0:WINDOW, :] = vwin[T:T + WINDOW, :]
                for v in range(4):
                    kr[v, 0:WINDOW, :] = kr[v, T:T + WINDOW, :]
                    vr[v, 0:WINDOW, :] = vr[v, T:T + WINDOW, :]
                proj[2, 0:HEAD_ROWS, :] = proj_nxt[2, T:T + HEAD_ROWS, :]
                proj[3, 0:HEAD_ROWS, :] = proj_nxt[3, T:T + HEAD_ROWS, :]

        a_col = -jnp.exp(alog_col_ref[...])
        a_row =-jnp.exp(alog_row_ref[...])
        tri = tri_ref[...]
        d_skip = dskip_ref[...]
        subs = range(n_sub)

        def out_piece(blk, pp):
            orows = slice(blk * R, (blk + 1) * R)
            cols = slice(pp * PIECE_W, (pp + 1) * PIECE_W)
            y_ref[0, orows, cols] = xres[orows, cols] + _dot(mix[orows, :], w_out_ref[:, cols])

        def block_body(i, carry):
            static = isinstance(i, int)
            r0 = i * R if static else pl.multiple_of(i * R, R)
            rows = pl.ds(r0, R)
            prow = pl.ds(r0 + HEAD_ROWS, R)
            wrow = pl.ds(r0 + WINDOW, R)
            sub_rows = [slice(u * CHUNK, (u + 1) * CHUNK) for u in subs]
            krows = [pl.ds(r0 + u * CHUNK, WINDOW + CHUNK) for u in subs]

            pieces = [(i * groups_per_block + gg, pp) for gg in range(groups_per_block) for pp in range(N_PIECES)]
            gap_share = (1, 4, 4, 1)
            per_gap = [len(pieces) * sh // sum(gap_share) for sh in gap_share]

            def issue(n):
                for _ in range(n):
                    g, pp = pieces.pop(0)
                    project_piece(hb_nxt, proj_nxt, g, pp)

            issue(per_gap[0])

            q = proj[0, prow, G0_Q:G0_Q + ATTN_WIDTH]
            k = proj[0, prow, G0_K:G0_K + KV_WIDTH]
            dt_row = _softplus(proj[1, prow, G1_DT:G1_DT + LANES] + dtb_row_ref[...])
            mem_q = []
            for hh in range(MEM_HEADS):
                mqh = proj[3, prow, G3_MQ + hh * MEM_HEAD_DIM:G3_MQ + (hh + 1) * MEM_HEAD_DIM]
                mem_q.append((mqh * _rms_scale(mqh) * g_mq_ref[...]).astype(BF16))

            seg = segq_ref[...]
            q_blocks = [q[:, j * 2 * LANES:(j + 1) * 2 * LANES] for j in range(ATTN_WIDTH // (2 * LANES))]
            q_ss = [_seg_sumsq(qj, seg) for qj in q_blocks]
            k_ss = _seg_sumsq(k, segk_ref[...])
            dtx = _dot_sel2(dt_row, expand_ref[...])
            acum_n = _dot_sel_lhs(tri, dt_row * a_row)
            mem_sc = [_dot_nt(mem_q[hh], memk_bf[:, hh * MEM_HEAD_DIM:(hh + 1) * MEM_HEAD_DIM])
                      for hh in range(MEM_HEADS)]
            dt_t = _dot_nt(wdt_ref[...], hb[rows, :])

            issue(per_gap[1])

            def conv(group, c0, width, ch0):
                acc = convb_ref[:, ch0:ch0 + width]
                blk = proj[group, pl.ds(r0, HEAD_ROWS + R), c0:c0 + width]
                for j in range(CONV_WIDTH):
                    back = CONV_WIDTH - 1 - j
                    tap = blk if back == 0 else pltpu.roll(blk, back, 0)
                    acc = acc + tap[HEAD_ROWS:HEAD_ROWS + R, :] * convw_ref[j:j + 1, ch0:ch0 + width]
                return _silu(acc)

            b_all = conv(2, G2_B, bc_w, SSD_WIDTH).astype(BF16)
            c_all = conv(3, G3_C, bc_w, SSD_WIDTH + bc_w).astype(BF16)
            xs = conv(2, G2_X, SSD_WIDTH, 0)

            cos = cos_ref[rows, :]
            sin_s = sin_ref[rows, :]
            for j, (qj, ss) in enumerate(zip(q_blocks, q_ss)):
                qn = qj * lax.rsqrt(ss * (1.0 / HEAD_DIM) + EPS) * gq_ref[:, j * 2 * LANES:(j + 1) * 2 * LANES]
                for jj in range(2):
                    c0 = j * 2 * LANES + jj * LANES
                    qs[rows, c0:c0 + LANES] = _rope(qn[:, jj * LANES:(jj + 1) * LANES], cos, sin_s,
                                                    first_half).astype(BF16)
            kn = k * lax.rsqrt(k_ss * (1.0 / HEAD_DIM) + EPS) * gk_ref[...]
            kwin[wrow, :] = _rope(kn, cos, sin_s, first_half)
            vwin[wrow, :] = proj[0, prow, G0_V:G0_V + KV_WIDTH]
            build_masked(kwin, kr, wrow)
            build_masked(vwin, vr, wrow)

            mem_p = [_softmax_rows(mem_sc[hh] * mem_scale) for hh in range(MEM_HEADS)]
            xdt = xs * dtx
            da_t3 = _split3(_softplus(dt_t + dtb_col_ref[...]) * a_col)

            bgs = [[b_all[sub_rows[u], g * SSD_STATE:(g + 1) * SSD_STATE] for g in range(SSD_GROUPS)] for u in subs]
            cgs = [[c_all[sub_rows[u], g * SSD_STATE:(g + 1) * SSD_STATE] for g in range(SSD_GROUPS)] for u in subs]
            cbs = [[_dot_nt(cgs[u][g], jnp.concatenate([bgs[u][g], bgs[u][g]], axis=0))
                    for g in range(SSD_GROUPS)] for u in subs]
            cross_cb = [_dot_nt(cgs[1][g], bgs[0][g]) for g in range(SSD_GROUPS)] if n_sub == 2 else None
            y_in = [_dot(c_all[:, g * SSD_STATE:(g + 1) * SSD_STATE], state_t[:, g * gw:(g + 1) * gw].astype(BF16))
                    for g in range(SSD_GROUPS)]
            acum = _dot_sel2(acum_n, expand_ref[...])
            dup = duptri_ref[...]
            acum_t = (_dot(da_t3[2], dup) + _dot(da_t3[1], dup)) + _dot(da_t3[0], dup)
            att_sc = []
            for u in subs:
                for g in range(N_KV_HEADS):
                    base = g * 2 * LANES
                    qrows = pl.ds(r0 + u * CHUNK, CHUNK)
                    qg = jnp.concatenate([qs[qrows, base:base + LANES], qs[qrows, base + LANES:base + 2 * LANES]],
                                         axis=0)
                    for pos in range(2):
                        att_sc.append(_dot_nt(qg, kr[2 * g + pos, krows[u], :]))
            mem_o = [_dot(mem_p[hh], memv_bf[:, hh * MEM_HEAD_DIM:(hh + 1) * MEM_HEAD_DIM])
                     for hh in range(MEM_HEADS)]

            issue(per_gap[2])
            if i > 0:
                for pp in range(OUT_PIECES // 2):
                    out_piece(i - 1, pp)

            att_p = []
            for u in subs:
                if not has_past:
                    first_valid = jnp.where(t == 0, WINDOW - (r0 + u * CHUNK), 0)
                    key_ok = key_idx >= first_valid
                for g in range(N_KV_HEADS):
                    for pos in range(2):
                        hd0 = 4 * g + pos
                        sink = jnp.where(row128 < CHUNK, sinks_ref[hd0], sinks_ref[hd0 + 2])
                        sc = att_sc[(u * N_KV_HEADS + g) * 2 + pos]
                        if not has_past:
                            sc = jnp.where(key_ok, sc, NEG_BIG)
                        att_p.append(_softmax_rows(sc, sink))

            exp_acum = jnp.exp(acum)
            a_last = [acum[u * CHUNK + CHUNK - 1:(u + 1) * CHUNK, :] for u in subs]
            dec_end = [jnp.exp(a_last[u]) for u in subs]
            x_end = [(xdt[sub_rows[u], :] * jnp.exp(a_last[u] - acum[sub_rows[u], :])).astype(BF16) for u in subs]
            ws, xbds = [], []
            zero_bd = jnp.zeros((2 * CHUNK, LANES), BF16)
            for u in subs:
                act = acum_t[:, u * LANES:(u + 1) * LANES]
                for quad in range(n_pairs // 2):
                    w_parts, bd_parts = [], []
                    for pair in (2 * quad, 2 * quad + 1):
                        g = pair // pairs_per_group
                        cols = slice(pair * LANES, (pair + 1) * LANES)
                        rowb = jnp.where(lane < CHUNK, act[2 * pair:2 * pair + 1, :], act[2 * pair + 1:2 * pair + 2, :])
                        dec = jnp.exp(jnp.where(causal_dup, acum[sub_rows[u], cols] - rowb, -jnp.inf))
                        w_parts.append((cbs[u][g] * dec).astype(BF16))
                        xp = xdt[sub_rows[u], cols]
                        zero = jnp.zeros_like(xp)
                        bd_parts.append(jnp.concatenate(
                            [jnp.where(lo_half64, xp, zero), jnp.where(lo_half64, zero, xp)], axis=0).astype(BF16))
                    ws.append(jnp.concatenate(w_parts, axis=1))
                    xbds.append(jnp.concatenate(
                        [jnp.concatenate([bd_parts[0], zero_bd], axis=1),
                         jnp.concatenate([zero_bd, bd_parts[1]], axis=1)], axis=0))

            for hh in range(MEM_HEADS):
                gm = proj[3, prow, G3_GM + hh * MEM_HEAD_DIM:G3_GM + (hh + 1) * MEM_HEAD_DIM]
                c0 = ATTN_WIDTH + SSD_WIDTH + hh * MEM_HEAD_DIM
                mix[rows, c0:c0 + MEM_HEAD_DIM] = (mem_o[hh] * _silu(gm)).astype(BF16)

            att_o = []
            for u in subs:
                for g in range(N_KV_HEADS):
                    j0 = (u * N_KV_HEADS + g) * 2
                    att_o.append(_dot(att_p[j0], vr[2 * g, krows[u], :]) + _dot(att_p[j0 + 1], vr[2 * g + 1, krows[u], :]))
            for g in range(SSD_GROUPS):
                gcols = slice(g * gw, (g + 1) * gw)
                st = state_t[:, gcols]
                for u in subs:
                    st = st * dec_end[u][:, gcols] + _dot_tn(bgs[u][g], x_end[u][:, gcols])
                state_t[:, gcols] = st
            cross = ([_dot(cross_cb[g].astype(BF16), x_end[0][:, g * gw:(g + 1) * gw]) for g in range(SSD_GROUPS)]
                     if n_sub == 2 else None)
            y_diag = [_dot(w, xbd) for w, xbd in zip(ws, xbds)]

            issue(len(pieces))
            if i > 0:
                for pp in range(OUT_PIECES // 2, OUT_PIECES):
                    out_piece(i - 1, pp)

            for u in subs:
                for g in range(N_KV_HEADS):
                    base = g * 2 * LANES
                    orow = pl.ds(r0 + u * CHUNK, CHUNK)
                    ga = proj[0, pl.ds(r0 + u * CHUNK + HEAD_ROWS, CHUNK), G0_GA + base:G0_GA + base + 2 * LANES]
                    o = att_o[u * N_KV_HEADS + g]
                    mix[orow, base:base + LANES] = (o[0:CHUNK] * _silu(ga[:, 0:LANES])).astype(BF16)
                    mix[orow, base + LANES:base + 2 * LANES] = (o[CHUNK:] * _silu(ga[:, LANES:])).astype(BF16)

            y_rows = []
            for u in subs:
                ys = []
                for pair in range(n_pairs):
                    g, hp = divmod(pair, pairs_per_group)
                    cols = slice(pair * LANES, (pair + 1) * LANES)
                    lc = slice(hp * LANES, (hp + 1) * LANES)
                    y_off = y_in[g][sub_rows[u], lc]
                    if u == 1:
                        y_off = y_off * dec_end[0][:, cols] + cross[g][:, lc]
                    yd = y_diag[(u * n_pairs + pair) // 2][:, (pair % 2) * LANES:(pair % 2 + 1) * LANES]
                    ys.append(yd + y_off * exp_acum[sub_rows[u], cols] + d_skip[:, cols] * xs[sub_rows[u], cols])
                y_rows.append(jnp.concatenate(ys, axis=1))
            y_all = y_rows[0] if n_sub == 1 else jnp.concatenate(y_rows, axis=0)
            z = proj[1, prow, G1_Z:G1_Z + SSD_WIDTH]
            yg = y_all * _silu(z)
            mix[rows, ATTN_WIDTH:ATTN_WIDTH + SSD_WIDTH] = (yg * _rms_scale(yg) * g_ssd_ref[...]).astype(BF16)

            xa = x_ahead_ref[0, rows, :]
            hb[rows, :] = (xa * _rms_scale(xa) * g_norm_ref[...]).astype(BF16)
            return carry

        for blk in range(n_blocks):
            block_body(blk, 0)

        for pp in range(OUT_PIECES):
            out_piece(n_blocks - 1, pp)
        xres[...] = x_ahead_ref[0]

        @pl.when(t == n_tiles - 1)
        def _final():
            newk_ref[0] = kwin[T:T + WINDOW, :]
            newv_ref[0] = vwin[T:T + WINDOW, :]
            ssm_out_ref[0] = state_t[...].T
            r_lo = T + HEAD_ROWS - (CONV_WIDTH - 1)
            conv_out_ref[0, :, 0:GROUP_W] = proj[2, r_lo:T + HEAD_ROWS, :]
            conv_out_ref[0, :, GROUP_W:CONV_CH] = proj[3, r_lo:T + HEAD_ROWS, 0:CONV_CH - GROUP_W]

    parity = s % 2

    @pl.when(parity == 0)
    def _even():
        step(proj_a, hb_a, xres_a, proj_b, hb_b, xres_b)

    @pl.when(parity == 1)
    def _odd():
        step(proj_b, hb_b, xres_b, proj_a, hb_a, xres_a)


def _constants(tile, n_sub):
    n_chunks = tile // CHUNK
    rows_per_block = n_sub * CHUNK
    idx = np.arange
    segq = (idx(2 * LANES)[:, None] // HEAD_DIM == idx(2 * LANES)[None, :] // HEAD_DIM)
    segk = (idx(KV_WIDTH)[:, None] // HEAD_DIM == idx(KV_WIDTH)[None, :] // HEAD_DIM)
    expand = (idx(2 * LANES)[:, None] % LANES == idx(SSD_WIDTH)[None, :] // SSD_HEAD_DIM)
    r = idx(rows_per_block)
    tri = (r[:, None] // CHUNK == r[None, :] // CHUNK) & (r[:, None] >= r[None, :])
    src = idx(rows_per_block)[:, None]
    dst = idx(n_sub * LANES)[None, :]
    duptri = (src // CHUNK == dst // LANES) & (src % CHUNK <= dst % CHUNK)
    return [jnp.asarray(m.astype(np.float32), dtype=BF16) for m in (segq, segk, expand, tri, duptri)]


def _layer(x, positions, mem_k, mem_v, past, params, *, tile):
    b, seq, _ = x.shape
    n_t = seq // tile
    n_steps = b * n_t
    has_past = past is not None
    n_chunks = tile // CHUNK
    n_sub = 2 if n_chunks % 2 == 0 else 1
    assert seq % tile == 0 and tile % CHUNK == 0 and N_GROUPS % (n_chunks // n_sub) == 0
    assert n_t == 1 or tile >= WINDOW

    half = HEAD_DIM // 2
    freqs = ROPE_THETA ** (-jnp.arange(half, dtype=F32) / half)
    ang = positions.astype(F32)[:, None] * freqs[None, :]
    cos, sin = jnp.cos(ang), jnp.sin(ang)
    cos_t = jnp.tile(cos, (1, LANES // half))
    sin_t = jnp.tile(jnp.concatenate([-sin, sin], axis=1), (1, LANES // HEAD_DIM))

    consts = _constants(tile, n_sub)

    def full(a):
        nd = a.ndim
        return pl.BlockSpec(a.shape, lambda i, _n=nd: (0,) * _n, pipeline_mode=pl.Buffered(1))

    def per_row(a):
        nd = a.ndim
        return pl.BlockSpec((1,) + a.shape[1:], lambda i, _n=nd: (i // n_t,) + (0,) * (_n - 1))

    def tile_at(j):
        j = jnp.minimum(j, n_steps - 1)
        return (j // n_t, j % n_t, 0)

    vec_params = [params[k] for k in ("g_norm", "gq", "gk")]
    tail_params = [params[k] for k in ("conv_w", "conv_b", "dtb_row", "dtb_col", "alog_row", "alog_col",
                                       "d_skip", "g_ssd", "g_mq")]
    past_inputs = list(past) if has_past else []
    inputs = [x, x, x, cos_t, sin_t, mem_k, mem_v, *past_inputs,
              params["w_in"], params["w_dt"], params["w_out"],
              *vec_params, params["sinks"], *tail_params, *consts]
    in_specs = [
        pl.BlockSpec((1, tile, D_MODEL), lambda i: tile_at(0)),
        pl.BlockSpec((1, tile, D_MODEL), lambda i: tile_at(1)),
        pl.BlockSpec((1, tile, D_MODEL), lambda i: tile_at(i + 2)),
        pl.BlockSpec((tile, LANES), lambda i: (i % n_t, 0)),
        pl.BlockSpec((tile, LANES), lambda i: (i % n_t, 0)),
        per_row(mem_k), per_row(mem_v),
        *[per_row(p) for p in past_inputs],
        full(params["w_in"]), full(params["w_dt"]), full(params["w_out"]),
        *[full(p) for p in vec_params],
        pl.BlockSpec(memory_space=pltpu.SMEM),
        *[full(p) for p in tail_params],
        *[full(c) for c in consts],
    ]
    out_shape = [
        jax.ShapeDtypeStruct((b, seq, D_MODEL), F32),
        jax.ShapeDtypeStruct((b, WINDOW, KV_WIDTH), F32),
        jax.ShapeDtypeStruct((b, WINDOW, KV_WIDTH), F32),
        jax.ShapeDtypeStruct((b, SSD_WIDTH, SSD_STATE), F32),
        jax.ShapeDtypeStruct((b, CONV_WIDTH - 1, CONV_CH), F32),
    ]
    out_specs = [
        pl.BlockSpec((1, tile, D_MODEL), lambda i: (i // n_t, i % n_t, 0)),
        pl.BlockSpec((1, WINDOW, KV_WIDTH), lambda i: (i // n_t, 0, 0)),
        pl.BlockSpec((1, WINDOW, KV_WIDTH), lambda i: (i // n_t, 0, 0)),
        pl.BlockSpec((1, SSD_WIDTH, SSD_STATE), lambda i: (i // n_t, 0, 0)),
        pl.BlockSpec((1, CONV_WIDTH - 1, CONV_CH), lambda i: (i // n_t, 0, 0)),
    ]
    proj_shape = (N_GROUPS, HEAD_ROWS + tile, GROUP_W)
    scratch = [
        pltpu.VMEM(proj_shape, F32), pltpu.VMEM(proj_shape, F32),
        pltpu.VMEM((tile, D_MODEL), BF16), pltpu.VMEM((tile, D_MODEL), BF16),
        pltpu.VMEM((tile, D_MODEL), F32), pltpu.VMEM((tile, D_MODEL), F32),
        pltpu.VMEM((tile, ATTN_WIDTH), BF16),
        pltpu.VMEM((WINDOW + tile, KV_WIDTH), F32),
        pltpu.VMEM((WINDOW + tile, KV_WIDTH), F32),
        pltpu.VMEM((4, WINDOW + tile, KV_WIDTH), BF16),
        pltpu.VMEM((4, WINDOW + tile, KV_WIDTH), BF16),
        pltpu.VMEM((SSD_STATE, SSD_WIDTH), F32),
        pltpu.VMEM((tile, MIX_WIDTH), BF16),
        pltpu.VMEM((MEM_TOKENS, MEM_WIDTH), BF16),
        pltpu.VMEM((MEM_TOKENS, MEM_WIDTH), BF16),
    ]
    return pl.pallas_call(
        functools.partial(_layer_kernel, tile=tile, n_tiles=n_t, n_sub=n_sub, has_past=has_past),
        grid=(n_steps,),
        in_specs=in_specs,
        out_specs=out_specs,
        out_shape=out_shape,
        scratch_shapes=scratch,
        compiler_params=pltpu.CompilerParams(
            dimension_semantics=("arbitrary",),
            vmem_limit_bytes=VMEM_LIMIT_BYTES),
        name="layer_sample" if has_past else "layer_prompt",
    )(*inputs)


def _prep_params(g_norm, w_in, g_q, g_k, sinks, conv_w, conv_b, dt_bias, a_log, d_skip, g_ssd, g_mq, w_out):
    o_z = ATTN_WIDTH + 2 * KV_WIDTH + ATTN_WIDTH
    o_x = o_z + SSD_WIDTH
    o_b = o_x + SSD_WIDTH
    o_c = o_b + SSD_GROUPS * SSD_STATE
    o_dt = o_c + SSD_GROUPS * SSD_STATE
    o_mq = o_dt + SSD_HEADS
    w_dt = w_in[:, o_dt:o_mq]
    zeros = lambda n: jnp.zeros((D_MODEL, n), w_in.dtype)
    groups = [
        w_in[:, 0:o_z],
        jnp.concatenate([w_in[:, o_z:o_x], w_dt, zeros(GROUP_W - SSD_WIDTH - SSD_HEADS)], axis=1),
        w_in[:, o_x:o_c],
        jnp.concatenate([w_in[:, o_c:o_dt], w_in[:, o_mq:]], axis=1),
    ]
    pad = LANES - SSD_HEADS
    return {
        "w_in": jnp.stack(groups).astype(BF16),
        "w_dt": w_dt.T.astype(BF16),
        "w_out": w_out.astype(BF16),
        "g_norm": g_norm.reshape(1, D_MODEL),
        "gq": (jnp.tile(g_q, N_Q_HEADS) * (1.0 / math.sqrt(HEAD_DIM))).reshape(1, ATTN_WIDTH),
        "gk": jnp.tile(g_k, N_KV_HEADS).reshape(1, KV_WIDTH),
        "sinks": sinks,
        "conv_w": conv_w,
        "conv_b": conv_b.reshape(1, CONV_CH),
        "dtb_row": jnp.pad(dt_bias, (0, pad)).reshape(1, LANES),
        "dtb_col": dt_bias.reshape(SSD_HEADS, 1),
        "alog_row": jnp.pad(a_log, (0, pad)).reshape(1, LANES),
        "alog_col": a_log.reshape(SSD_HEADS, 1),
        "d_skip": jnp.repeat(d_skip, SSD_HEAD_DIM).reshape(1, SSD_WIDTH),
        "g_ssd": g_ssd.reshape(1, SSD_WIDTH),
        "g_mq": g_mq.reshape(1, MEM_HEAD_DIM),
    }


PROMPT_TILE = 256


def kernel(x_prompt, x_sample, cache_win_k, cache_win_v, state_ssm, state_conv, cache_mem_k, cache_mem_v,
           mem_prompt, g_norm, w_in, g_q, g_k, sinks, conv_w, conv_b, dt_bias, a_log, d_skip, g_ssd,
           g_mem, w_mem_k, w_mem_v, g_mk, g_mq, w_out):
    depth = w_in.shape[0]
    assert depth == 1
    l = 0
    bp, t_p, _ = x_prompt.shape
    bs, t_s, _ = x_sample.shape
    params = _prep_params(g_norm[l], w_in[l], g_q[l], g_k[l], sinks[l], conv_w[l], conv_b[l], dt_bias[l],
                          a_log[l], d_skip[l], g_ssd[l], g_mq[l], w_out[l])

    mk_p, mv_p = _memory_kv(mem_prompt, g_mem[l], w_mem_k[l], w_mem_v[l], g_mk[l])

    yp, kp, vp, sp, cp = _layer(
        x_prompt, jnp.arange(t_p, dtype=jnp.int32), mk_p, mv_p, None, params, tile=min(PROMPT_TILE, t_p))

    past = (cache_win_k[l].reshape(bs, WINDOW, KV_WIDTH), cache_win_v[l].reshape(bs, WINDOW, KV_WIDTH),
            state_ssm[l].reshape(bs, SSD_WIDTH, SSD_STATE), state_conv[l])
    ys, ks, vs, ss, cs = _layer(
        x_sample, PAST_LEN + jnp.arange(t_s, dtype=jnp.int32),
        cache_mem_k[l], cache_mem_v[l],
        past, params, tile=t_s)

    kv5 = lambda a, b: a.reshape(1, b, WINDOW, N_KV_HEADS, HEAD_DIM)
    ssm5 = lambda a, b: a.reshape(1, b, SSD_HEADS, SSD_HEAD_DIM, SSD_STATE)
    mem5 = lambda a: a[None]
    return (yp, ys,
            kv5(kp, bp), kv5(vp, bp), ssm5(sp, bp), cp[None],
            mem5(mk_p), mem5(mv_p),
            kv5(ks, bs), kv5(vs, bs), ssm5(ss, bs), cs[None])
```

```python
import functools
import math

import numpy as np
import jax
import jax.numpy as jnp
from jax import lax
from jax.experimental import pallas as pl
from jax.experimental.pallas import tpu as pltpu

D_MODEL = 1024
CHUNK = 64
HEAD_DIM = 64
N_Q_HEADS = 8
N_KV_HEADS = 2
ATTN_WIDTH = N_Q_HEADS * HEAD_DIM
KV_WIDTH = N_KV_HEADS * HEAD_DIM
WINDOW = 128
ROPE_THETA = 10000.0
SSD_WIDTH = D_MODEL
SSD_HEAD_DIM = 64
SSD_HEADS = SSD_WIDTH // SSD_HEAD_DIM
SSD_GROUPS = 2
SSD_STATE = 128
CONV_WIDTH = 4
CONV_CH = SSD_WIDTH + 2 * SSD_GROUPS * SSD_STATE
MEM_TOKENS = 256
MEM_HEADS = 4
MEM_HEAD_DIM = 128
MEM_WIDTH = MEM_HEADS * MEM_HEAD_DIM
MIX_WIDTH = 2 * D_MODEL
EPS = 1e-6
PAST_LEN = 2048

LANES = 128
SUBLANES = 8
VMEM_LIMIT_BYTES = 56 * 1024 * 1024

N_GROUPS = 4
GROUP_W = 1280
G0_Q, G0_K, G0_V, G0_GA = 0, ATTN_WIDTH, ATTN_WIDTH + KV_WIDTH, ATTN_WIDTH + 2 * KV_WIDTH
G1_Z, G1_DT = 0, SSD_WIDTH
G2_X, G2_B = 0, SSD_WIDTH
G3_C, G3_MQ, G3_GM = 0, SSD_GROUPS * SSD_STATE, SSD_GROUPS * SSD_STATE + MEM_WIDTH
PIECE_W = 256
N_PIECES = GROUP_W // PIECE_W
OUT_PIECES = D_MODEL // PIECE_W
HEAD_ROWS = SUBLANES

F32 = jnp.float32
BF16 = jnp.bfloat16
NEG_BIG = -1e30


def _dot(a, b):
    return jnp.dot(a, b, preferred_element_type=F32)


def _dot_nt(a, b):
    return lax.dot_general(a, b, (((1,), (1,)), ((), ())), preferred_element_type=F32)


def _dot_tn(a, b):
    return lax.dot_general(a, b, (((0,), (0,)), ((), ())), preferred_element_type=F32)


def _split3(x):
    h1 = x.astype(BF16)
    r1 = x - h1.astype(F32)
    h2 = r1.astype(BF16)
    r2 = r1 - h2.astype(F32)
    return h1, h2, r2.astype(BF16)


def _dot_sel_rhs(x, sel):
    a, b, c = _split3(x)
    return (_dot(c, sel) + _dot(b, sel)) + _dot(a, sel)


def _dot_sel_lhs(sel, x):
    a, b, c = _split3(x)
    return (_dot(sel, c) + _dot(sel, b)) + _dot(sel, a)


def _dot_sel2(x, sel2):
    hi = x.astype(BF16)
    lo = (x - hi.astype(F32)).astype(BF16)
    return _dot(jnp.concatenate([hi, lo], axis=1), sel2)


def _seg_sumsq(x, seg):
    return _dot((x * x).astype(BF16), seg)


def _silu(x):
    h = 0.5 * x
    return h + h * jnp.tanh(h)


def _softplus(x):
    return jnp.maximum(x, 0.0) + jnp.log(1.0 + jnp.exp(-jnp.abs(x)))


def _rms_scale(x):
    return lax.rsqrt(jnp.mean(x * x, axis=-1, keepdims=True) + EPS)


def _softmax_rows(sc, extra=None):
    m = jnp.max(sc, axis=-1, keepdims=True)
    if extra is not None:
        m = jnp.maximum(m, extra)
    e = jnp.exp(sc - m)
    den = jnp.sum(e, axis=-1, keepdims=True)
    if extra is not None:
        den = den + jnp.exp(extra - m)
    return (e * (1.0 / den)).astype(BF16)


def _rope(x, cos, sin_signed, first_half):
    partner = jnp.where(first_half, pltpu.roll(x, LANES - HEAD_DIM // 2, 1), pltpu.roll(x, HEAD_DIM // 2, 1))
    return x * cos + partner * sin_signed


def _memory_kv_kernel(mem_ref, g_mem_ref, wk_ref, wv_ref, g_mk_ref, k_out, v_out):
    x = mem_ref[0]
    hb = (x * _rms_scale(x) * g_mem_ref[...]).astype(BF16)
    k = _dot(hb, wk_ref[...])
    v = _dot(hb, wv_ref[...])
    for hh in range(MEM_HEADS):
        sl = slice(hh * MEM_HEAD_DIM, (hh + 1) * MEM_HEAD_DIM)
        kh = k[:, sl]
        k_out[0, :, hh, :] = kh * _rms_scale(kh) * g_mk_ref[...]
        v_out[0, :, hh, :] = v[:, sl]


def _memory_kv(mem, g_mem, w_k, w_v, g_mk):
    b = mem.shape[0]
    full = lambda shape: pl.BlockSpec(shape, lambda i: (0,) * len(shape))
    return pl.pallas_call(
        _memory_kv_kernel,
        grid=(b,),
        in_specs=[
            pl.BlockSpec((1, MEM_TOKENS, D_MODEL), lambda i: (i, 0, 0)),
            full((1, D_MODEL)),
            full((D_MODEL, MEM_WIDTH)),
            full((D_MODEL, MEM_WIDTH)),
            full((1, MEM_HEAD_DIM)),
        ],
        out_specs=[
            pl.BlockSpec((1, MEM_TOKENS, MEM_HEADS, MEM_HEAD_DIM), lambda i: (i, 0, 0, 0)),
            pl.BlockSpec((1, MEM_TOKENS, MEM_HEADS, MEM_HEAD_DIM), lambda i: (i, 0, 0, 0)),
        ],
        out_shape=[jax.ShapeDtypeStruct((b, MEM_TOKENS, MEM_HEADS, MEM_HEAD_DIM), F32)] * 2,
        compiler_params=pltpu.CompilerParams(dimension_semantics=("arbitrary",)),
        name="memory_kv",
    )(mem, g_mem.reshape(1, D_MODEL), w_k.astype(BF16), w_v.astype(BF16), g_mk.reshape(1, MEM_HEAD_DIM))


def _layer_kernel(*refs, tile, n_tiles, total_steps, n_sub, has_past):
    x_first_ref, x_second_ref, x_ahead_ref, x_prev_ref, cos_ref, sin_ref, memk_ref, memv_ref, *rest = refs
    if has_past:
        wink_ref, winv_ref, ssm_ref, conv_ref, *rest = rest
    else:
        wink_ref = winv_ref = ssm_ref = conv_ref = None
    (w_in_ref, wdt_ref, w_out_ref,
     g_norm_ref, gq_ref, gk_ref, sinks_ref, convw_ref, convb_ref,
     dtb_row_ref, dtb_col_ref, alog_row_ref, alog_col_ref, dskip_ref, g_ssd_ref, g_mq_ref,
     segq_ref, segk_ref, expand_ref, tri_ref, duptri_ref,
     y_ref, newk_ref, newv_ref, ssm_out_ref, conv_out_ref,
     proj_a, proj_b, hb_a, hb_b, mix_a, mix_b,
     qs, kwin, vwin, kr, vr, state_t, memk_bf, memv_bf) = rest

    T = tile
    n_chunks = T // CHUNK
    n_blocks = n_chunks // n_sub
    R = n_sub * CHUNK
    groups_per_block = N_GROUPS // n_blocks
    s = pl.program_id(0)
    t = s % n_tiles
    body_rows = pl.ds(HEAD_ROWS, T)

    def normed_input(x_ref, hb):
        x = x_ref[0]
        hb[...] = (x * _rms_scale(x) * g_norm_ref[...]).astype(BF16)

    def project_piece(hb, proj, g, i):
        cols = slice(i * PIECE_W, (i + 1) * PIECE_W)
        proj[g, body_rows, cols] = _dot(hb[...], w_in_ref[g, :, cols])

    @pl.when(s == 0)
    def _prologue():
        normed_input(x_first_ref, hb_a)
        normed_input(x_second_ref, hb_b)
        mix_b[...] = jnp.zeros((T, MIX_WIDTH), BF16)
        for g in range(N_GROUPS):
            for i in range(N_PIECES):
                project_piece(hb_a, proj_a, g, i)

    lane = lax.broadcasted_iota(jnp.int32, (1, LANES), 1)
    first_half = (lane % HEAD_DIM) < (HEAD_DIM // 2)
    lo_half = lane < HEAD_DIM
    row64 = lax.broadcasted_iota(jnp.int32, (CHUNK, LANES), 0)
    lane64 = lax.broadcasted_iota(jnp.int32, (CHUNK, LANES), 1)
    causal_dup = row64 >= (lane64 % CHUNK)
    lo_half64 = lane64 < CHUNK
    row128 = lax.broadcasted_iota(jnp.int32, (2 * CHUNK, 1), 0)
    key_idx = lax.broadcasted_iota(jnp.int32, (1, WINDOW + CHUNK), 1)
    gw = SSD_WIDTH // SSD_GROUPS
    n_pairs = SSD_HEADS // 2
    pairs_per_group = n_pairs // SSD_GROUPS
    mem_scale = 1.0 / math.sqrt(MEM_HEAD_DIM)
    bc_w = SSD_GROUPS * SSD_STATE

    def build_masked(src, dst, rows):
        full = src[rows, :]
        swapped = pltpu.roll(full, HEAD_DIM, 1)
        zero = jnp.zeros_like(full)
        dst[0, rows, :] = jnp.where(lo_half, full, zero).astype(BF16)
        dst[1, rows, :] = jnp.where(lo_half, zero, swapped).astype(BF16)
        dst[2, rows, :] = jnp.where(lo_half, swapped, zero).astype(BF16)
        dst[3, rows, :] = jnp.where(lo_half, zero, full).astype(BF16)

    def out_piece(mix_prev, pp):
        cols = slice(pp * PIECE_W, (pp + 1) * PIECE_W)
        y_ref[0, :, cols] = x_prev_ref[0, :, cols] + _dot(mix_prev[...], w_out_ref[:, cols])

    def step(proj, hb, mix, proj_nxt, hb_nxt, mix_prev):
        @pl.when(t == 0)
        def _init():
            zero_head = jnp.zeros((HEAD_ROWS, GROUP_W), F32)
            proj[2, 0:HEAD_ROWS, :] = zero_head
            proj[3, 0:HEAD_ROWS, :] = zero_head
            if has_past:
                kwin[0:WINDOW, :] = wink_ref[0]
                vwin[0:WINDOW, :] = winv_ref[0]
                state_t[...] = ssm_ref[0].T
                tail = conv_ref[0]
                r_lo = HEAD_ROWS - (CONV_WIDTH - 1)
                proj[2, r_lo:HEAD_ROWS, :] = tail[:, 0:GROUP_W]
                proj[3, r_lo:HEAD_ROWS, 0:CONV_CH - GROUP_W] = tail[:, GROUP_W:CONV_CH]
                for src, dst in ((kwin, kr), (vwin, vr)):
                    build_masked(src, dst, slice(0, WINDOW))
            else:
                state_t[...] = jnp.zeros((SSD_STATE, SSD_WIDTH), F32)
                kwin[0:WINDOW, :] = jnp.zeros((WINDOW, KV_WIDTH), F32)
                vwin[0:WINDOW, :] = jnp.zeros((WINDOW, KV_WIDTH), F32)
                for v in range(4):
                    kr[v, 0:WINDOW, :] = jnp.zeros((WINDOW, KV_WIDTH), BF16)
                    vr[v, 0:WINDOW, :] = jnp.zeros((WINDOW, KV_WIDTH), BF16)
            for hh in range(MEM_HEADS):
                sl = slice(hh * MEM_HEAD_DIM, (hh + 1) * MEM_HEAD_DIM)
                memk_bf[:, sl] = memk_ref[0, :, hh, :].astype(BF16)
                memv_bf[:, sl] = memv_ref[0, :, hh, :].astype(BF16)

        if n_tiles > 1:
            @pl.when(t > 0)
            def _shift():
                kwin[0:WINDOW, :] = kwin[T:T + WINDOW, :]
                vwin[0:WINDOW, :] = vwin[T:T + WINDOW, :]
                for v in range(4):
                    kr[v, 0:WINDOW, :] = kr[v, T:T + WINDOW, :]
                    vr[v, 0:WINDOW, :] = vr[v, T:T + WINDOW, :]
                proj[2, 0:HEAD_ROWS, :] = proj_nxt[2, T:T + HEAD_ROWS, :]
                proj[3, 0:HEAD_ROWS, :] = proj_nxt[3, T:T + HEAD_ROWS, :]

        a_col = -jnp.exp(alog_col_ref[...])
        a_row =-jnp.exp(alog_row_ref[...])
        tri = tri_ref[...]
        d_skip = dskip_ref[...]
        subs = range(n_sub)

        out_per_block = OUT_PIECES // n_blocks

        def block_body(i, carry):
            static = isinstance(i, int)
            r0 = i * R if static else pl.multiple_of(i * R, R)
            rows = pl.ds(r0, R)
            prow = pl.ds(r0 + HEAD_ROWS, R)
            wrow = pl.ds(r0 + WINDOW, R)
            sub_rows = [slice(u * CHUNK, (u + 1) * CHUNK) for u in subs]
            krows = [pl.ds(r0 + u * CHUNK, WINDOW + CHUNK) for u in subs]

            pieces = [(i * groups_per_block + gg, pp) for gg in range(groups_per_block) for pp in range(N_PIECES)]
            gap_share = (1, 4, 4, 1)
            per_gap = [len(pieces) * sh // sum(gap_share) for sh in gap_share]

            def issue(n):
                for _ in range(n):
                    g, pp = pieces.pop(0)
                    project_piece(hb_nxt, proj_nxt, g, pp)

            issue(per_gap[0])

            q = proj[0, prow, G0_Q:G0_Q + ATTN_WIDTH]
            k = proj[0, prow, G0_K:G0_K + KV_WIDTH]
            dt_row = _softplus(proj[1, prow, G1_DT:G1_DT + LANES] + dtb_row_ref[...])
            mem_q = []
            for hh in range(MEM_HEADS):
                mqh = proj[3, prow, G3_MQ + hh * MEM_HEAD_DIM:G3_MQ + (hh + 1) * MEM_HEAD_DIM]
                mem_q.append((mqh * _rms_scale(mqh) * g_mq_ref[...]).astype(BF16))

            seg = segq_ref[...]
            q_blocks = [q[:, j * 2 * LANES:(j + 1) * 2 * LANES] for j in range(ATTN_WIDTH // (2 * LANES))]
            q_ss = [_seg_sumsq(qj, seg) for qj in q_blocks]
            k_ss = _seg_sumsq(k, segk_ref[...])
            dtx = _dot_sel2(dt_row, expand_ref[...])
            acum_n = _dot_sel_lhs(tri, dt_row * a_row)
            mem_sc = [_dot_nt(mem_q[hh], memk_bf[:, hh * MEM_HEAD_DIM:(hh + 1) * MEM_HEAD_DIM])
                      for hh in range(MEM_HEADS)]
            dt_t = _dot_nt(wdt_ref[...], hb[rows, :])

            issue(per_gap[1])

            def conv(group, c0, width, ch0):
                acc = convb_ref[:, ch0:ch0 + width]
                blk = proj[group, pl.ds(r0, HEAD_ROWS + R), c0:c0 + width]
                for j in range(CONV_WIDTH):
                    back = CONV_WIDTH - 1 - j
                    tap = blk if back == 0 else pltpu.roll(blk, back, 0)
                    acc = acc + tap[HEAD_ROWS:HEAD_ROWS + R, :] * convw_ref[j:j + 1, ch0:ch0 + width]
                return _silu(acc)

            b_all = conv(2, G2_B, bc_w, SSD_WIDTH).astype(BF16)
            c_all = conv(3, G3_C, bc_w, SSD_WIDTH + bc_w).astype(BF16)
            xs = conv(2, G2_X, SSD_WIDTH, 0)

            cos = cos_ref[rows, :]
            sin_s = sin_ref[rows, :]
            for j, (qj, ss) in enumerate(zip(q_blocks, q_ss)):
                qn = qj * lax.rsqrt(ss * (1.0 / HEAD_DIM) + EPS) * gq_ref[:, j * 2 * LANES:(j + 1) * 2 * LANES]
                for jj in range(2):
                    c0 = j * 2 * LANES + jj * LANES
                    qs[rows, c0:c0 + LANES] = _rope(qn[:, jj * LANES:(jj + 1) * LANES], cos, sin_s,
                                                    first_half).astype(BF16)
            kn = k * lax.rsqrt(k_ss * (1.0 / HEAD_DIM) + EPS) * gk_ref[...]
            kwin[wrow, :] = _rope(kn, cos, sin_s, first_half)
            vwin[wrow, :] = proj[0, prow, G0_V:G0_V + KV_WIDTH]
            build_masked(kwin, kr, wrow)
            build_masked(vwin, vr, wrow)

            mem_p = [_softmax_rows(mem_sc[hh] * mem_scale) for hh in range(MEM_HEADS)]
            xdt = xs * dtx
            da_t3 = _split3(_softplus(dt_t + dtb_col_ref[...]) * a_col)

            bgs = [[b_all[sub_rows[u], g * SSD_STATE:(g + 1) * SSD_STATE] for g in range(SSD_GROUPS)] for u in subs]
            cgs = [[c_all[sub_rows[u], g * SSD_STATE:(g + 1) * SSD_STATE] for g in range(SSD_GROUPS)] for u in subs]
            cbs = [[_dot_nt(cgs[u][g], jnp.concatenate([bgs[u][g], bgs[u][g]], axis=0))
                    for g in range(SSD_GROUPS)] for u in subs]
            cross_cb = [_dot_nt(cgs[1][g], bgs[0][g]) for g in range(SSD_GROUPS)] if n_sub == 2 else None
            y_in = [_dot(c_all[:, g * SSD_STATE:(g + 1) * SSD_STATE], state_t[:, g * gw:(g + 1) * gw].astype(BF16))
                    for g in range(SSD_GROUPS)]
            acum = _dot_sel2(acum_n, expand_ref[...])
            dup = duptri_ref[...]
            acum_t = (_dot(da_t3[2], dup) + _dot(da_t3[1], dup)) + _dot(da_t3[0], dup)
            att_sc = []
            for u in subs:
                for g in range(N_KV_HEADS):
                    base = g * 2 * LANES
                    qrows = pl.ds(r0 + u * CHUNK, CHUNK)
                    qg = jnp.concatenate([qs[qrows, base:base + LANES], qs[qrows, base + LANES:base + 2 * LANES]],
                                         axis=0)
                    for pos in range(2):
                        att_sc.append(_dot_nt(qg, kr[2 * g + pos, krows[u], :]))
            mem_o = [_dot(mem_p[hh], memv_bf[:, hh * MEM_HEAD_DIM:(hh + 1) * MEM_HEAD_DIM])
                     for hh in range(MEM_HEADS)]

            issue(per_gap[2])
            for pp in range(i * out_per_block, i * out_per_block + out_per_block // 2):
                out_piece(mix_prev, pp)

            att_p = []
            for u in subs:
                if not has_past:
                    first_valid = jnp.where(t == 0, WINDOW - (r0 + u * CHUNK), 0)
                    key_ok = key_idx >= first_valid
                for g in range(N_KV_HEADS):
                    for pos in range(2):
                        hd0 = 4 * g + pos
                        sink = jnp.where(row128 < CHUNK, sinks_ref[hd0], sinks_ref[hd0 + 2])
                        sc = att_sc[(u * N_KV_HEADS + g) * 2 + pos]
                        if not has_past:
                            sc = jnp.where(key_ok, sc, NEG_BIG)
                        att_p.append(_softmax_rows(sc, sink))

            exp_acum = jnp.exp(acum)
            a_last = [acum[u * CHUNK + CHUNK - 1:(u + 1) * CHUNK, :] for u in subs]
            dec_end = [jnp.exp(a_last[u]) for u in subs]
            x_end = [(xdt[sub_rows[u], :] * jnp.exp(a_last[u] - acum[sub_rows[u], :])).astype(BF16) for u in subs]
            ws, xbds = [], []
            zero_bd = jnp.zeros((2 * CHUNK, LANES), BF16)
            for u in subs:
                act = acum_t[:, u * LANES:(u + 1) * LANES]
                for quad in range(n_pairs // 2):
                    w_parts, bd_parts = [], []
                    for pair in (2 * quad, 2 * quad + 1):
                        g = pair // pairs_per_group
                        cols = slice(pair * LANES, (pair + 1) * LANES)
                        rowb = jnp.where(lane < CHUNK, act[2 * pair:2 * pair + 1, :], act[2 * pair + 1:2 * pair + 2, :])
                        dec = jnp.exp(jnp.where(causal_dup, acum[sub_rows[u], cols] - rowb, -jnp.inf))
                        w_parts.append((cbs[u][g] * dec).astype(BF16))
                        xp = xdt[sub_rows[u], cols]
                        zero = jnp.zeros_like(xp)
                        bd_parts.append(jnp.concatenate(
                            [jnp.where(lo_half64, xp, zero), jnp.where(lo_half64, zero, xp)], axis=0).astype(BF16))
                    ws.append(jnp.concatenate(w_parts, axis=1))
                    xbds.append(jnp.concatenate(
                        [jnp.concatenate([bd_parts[0], zero_bd], axis=1),
                         jnp.concatenate([zero_bd, bd_parts[1]], axis=1)], axis=0))

            for hh in range(MEM_HEADS):
                gm = proj[3, prow, G3_GM + hh * MEM_HEAD_DIM:G3_GM + (hh + 1) * MEM_HEAD_DIM]
                c0 = ATTN_WIDTH + SSD_WIDTH + hh * MEM_HEAD_DIM
                mix[rows, c0:c0 + MEM_HEAD_DIM] = (mem_o[hh] * _silu(gm)).astype(BF16)

            att_o = []
            for u in subs:
                for g in range(N_KV_HEADS):
                    j0 = (u * N_KV_HEADS + g) * 2
                    att_o.append(_dot(att_p[j0], vr[2 * g, krows[u], :]) + _dot(att_p[j0 + 1], vr[2 * g + 1, krows[u], :]))
            for g in range(SSD_GROUPS):
                gcols = slice(g * gw, (g + 1) * gw)
                st = state_t[:, gcols]
                for u in subs:
                    st = st * dec_end[u][:, gcols] + _dot_tn(bgs[u][g], x_end[u][:, gcols])
                state_t[:, gcols] = st
            cross = ([_dot(cross_cb[g].astype(BF16), x_end[0][:, g * gw:(g + 1) * gw]) for g in range(SSD_GROUPS)]
                     if n_sub == 2 else None)
            y_diag = [_dot(w, xbd) for w, xbd in zip(ws, xbds)]

            issue(len(pieces))
            for pp in range(i * out_per_block + out_per_block // 2, (i + 1) * out_per_block):
                out_piece(mix_prev, pp)

            for u in subs:
                for g in range(N_KV_HEADS):
                    base = g * 2 * LANES
                    orow = pl.ds(r0 + u * CHUNK, CHUNK)
                    ga = proj[0, pl.ds(r0 + u * CHUNK + HEAD_ROWS, CHUNK), G0_GA + base:G0_GA + base + 2 * LANES]
                    o = att_o[u * N_KV_HEADS + g]
                    mix[orow, base:base + LANES] = (o[0:CHUNK] * _silu(ga[:, 0:LANES])).astype(BF16)
                    mix[orow, base + LANES:base + 2 * LANES] = (o[CHUNK:] * _silu(ga[:, LANES:])).astype(BF16)

            y_rows = []
            for u in subs:
                ys = []
                for pair in range(n_pairs):
                    g, hp = divmod(pair, pairs_per_group)
                    cols = slice(pair * LANES, (pair + 1) * LANES)
                    lc = slice(hp * LANES, (hp + 1) * LANES)
                    y_off = y_in[g][sub_rows[u], lc]
                    if u == 1:
                        y_off = y_off * dec_end[0][:, cols] + cross[g][:, lc]
                    yd = y_diag[(u * n_pairs + pair) // 2][:, (pair % 2) * LANES:(pair % 2 + 1) * LANES]
                    ys.append(yd + y_off * exp_acum[sub_rows[u], cols] + d_skip[:, cols] * xs[sub_rows[u], cols])
                y_rows.append(jnp.concatenate(ys, axis=1))
            y_all = y_rows[0] if n_sub == 1 else jnp.concatenate(y_rows, axis=0)
            z = proj[1, prow, G1_Z:G1_Z + SSD_WIDTH]
            yg = y_all * _silu(z)
            mix[rows, ATTN_WIDTH:ATTN_WIDTH + SSD_WIDTH] = (yg * _rms_scale(yg) * g_ssd_ref[...]).astype(BF16)

            xa = x_ahead_ref[0, rows, :]
            hb[rows, :] = (xa * _rms_scale(xa) * g_norm_ref[...]).astype(BF16)
            return carry

        for blk in range(n_blocks):
            block_body(blk, 0)

        @pl.when(t == n_tiles - 1)
        def _final():
            newk_ref[0] = kwin[T:T + WINDOW, :]
            newv_ref[0] = vwin[T:T + WINDOW, :]
            ssm_out_ref[0] = state_t[...].T
            r_lo = T + HEAD_ROWS - (CONV_WIDTH - 1)
            conv_out_ref[0, :, 0:GROUP_W] = proj[2, r_lo:T + HEAD_ROWS, :]
            conv_out_ref[0, :, GROUP_W:CONV_CH] = proj[3, r_lo:T + HEAD_ROWS, 0:CONV_CH - GROUP_W]

    parity = s % 2

    @pl.when((parity == 0) & (s < total_steps))
    def _even():
        step(proj_a, hb_a, mix_a, proj_b, hb_b, mix_b)

    @pl.when((parity == 1) & (s < total_steps))
    def _odd():
        step(proj_b, hb_b, mix_b, proj_a, hb_a, mix_a)

    @pl.when(s == total_steps)
    def _flush():
        for pp in range(OUT_PIECES):
            out_piece(mix_b if total_steps % 2 == 0 else mix_a, pp)


def _constants(tile, n_sub):
    n_chunks = tile // CHUNK
    rows_per_block = n_sub * CHUNK
    idx = np.arange
    segq = (idx(2 * LANES)[:, None] // HEAD_DIM == idx(2 * LANES)[None, :] // HEAD_DIM)
    segk = (idx(KV_WIDTH)[:, None] // HEAD_DIM == idx(KV_WIDTH)[None, :] // HEAD_DIM)
    expand = (idx(2 * LANES)[:, None] % LANES == idx(SSD_WIDTH)[None, :] // SSD_HEAD_DIM)
    r = idx(rows_per_block)
    tri = (r[:, None] // CHUNK == r[None, :] // CHUNK) & (r[:, None] >= r[None, :])
    src = idx(rows_per_block)[:, None]
    dst = idx(n_sub * LANES)[None, :]
    duptri = (src // CHUNK == dst // LANES) & (src % CHUNK <= dst % CHUNK)
    return [jnp.asarray(m.astype(np.float32), dtype=BF16) for m in (segq, segk, expand, tri, duptri)]


def _layer(x, positions, mem_k, mem_v, past, params, *, tile):
    b, seq, _ = x.shape
    n_t = seq // tile
    n_steps = b * n_t
    has_past = past is not None
    n_chunks = tile // CHUNK
    n_sub = 2 if n_chunks % 2 == 0 else 1
    assert seq % tile == 0 and tile % CHUNK == 0 and N_GROUPS % (n_chunks // n_sub) == 0
    assert n_t == 1 or tile >= WINDOW

    half = HEAD_DIM // 2
    freqs = ROPE_THETA ** (-jnp.arange(half, dtype=F32) / half)
    ang = positions.astype(F32)[:, None] * freqs[None, :]
    cos, sin = jnp.cos(ang), jnp.sin(ang)
    cos_t = jnp.tile(cos, (1, LANES // half))
    sin_t = jnp.tile(jnp.concatenate([-sin, sin], axis=1), (1, LANES // HEAD_DIM))

    consts = _constants(tile, n_sub)

    def full(a):
        nd = a.ndim
        return pl.BlockSpec(a.shape, lambda i, _n=nd: (0,) * _n, pipeline_mode=pl.Buffered(1))

    def row_of(i):
        return jnp.minimum(i, n_steps - 1) // n_t

    def per_row(a):
        nd = a.ndim
        return pl.BlockSpec((1,) + a.shape[1:], lambda i, _n=nd: (row_of(i),) + (0,) * (_n - 1))

    def tile_at(j):
        j = jnp.clip(j, 0, n_steps - 1)
        return (j // n_t, j % n_t, 0)

    vec_params = [params[k] for k in ("g_norm", "gq", "gk")]
    tail_params = [params[k] for k in ("conv_w", "conv_b", "dtb_row", "dtb_col", "alog_row", "alog_col",
                                       "d_skip", "g_ssd", "g_mq")]
    past_inputs = list(past) if has_past else []
    inputs = [x, x, x, x, cos_t, sin_t, mem_k, mem_v, *past_inputs,
              params["w_in"], params["w_dt"], params["w_out"],
              *vec_params, params["sinks"], *tail_params, *consts]
    in_specs = [
        pl.BlockSpec((1, tile, D_MODEL), lambda i: tile_at(0)),
        pl.BlockSpec((1, tile, D_MODEL), lambda i: tile_at(1)),
        pl.BlockSpec((1, tile, D_MODEL), lambda i: tile_at(i + 2)),
        pl.BlockSpec((1, tile, D_MODEL), lambda i: tile_at(i - 1)),
        pl.BlockSpec((tile, LANES), lambda i: (i % n_t, 0)),
        pl.BlockSpec((tile, LANES), lambda i: (i % n_t, 0)),
        per_row(mem_k), per_row(mem_v),
        *[per_row(p) for p in past_inputs],
        full(params["w_in"]), full(params["w_dt"]), full(params["w_out"]),
        *[full(p) for p in vec_params],
        pl.BlockSpec(memory_space=pltpu.SMEM),
        *[full(p) for p in tail_params],
        *[full(c) for c in consts],
    ]
    out_shape = [
        jax.ShapeDtypeStruct((b, seq, D_MODEL), F32),
        jax.ShapeDtypeStruct((b, WINDOW, KV_WIDTH), F32),
        jax.ShapeDtypeStruct((b, WINDOW, KV_WIDTH), F32),
        jax.ShapeDtypeStruct((b, SSD_WIDTH, SSD_STATE), F32),
        jax.ShapeDtypeStruct((b, CONV_WIDTH - 1, CONV_CH), F32),
    ]
    out_specs = [
        pl.BlockSpec((1, tile, D_MODEL), lambda i: tile_at(i - 1)),
        pl.BlockSpec((1, WINDOW, KV_WIDTH), lambda i: (row_of(i), 0, 0)),
        pl.BlockSpec((1, WINDOW, KV_WIDTH), lambda i: (row_of(i), 0, 0)),
        pl.BlockSpec((1, SSD_WIDTH, SSD_STATE), lambda i: (row_of(i), 0, 0)),
        pl.BlockSpec((1, CONV_WIDTH - 1, CONV_CH), lambda i: (row_of(i), 0, 0)),
    ]
    proj_shape = (N_GROUPS, HEAD_ROWS + tile, GROUP_W)
    scratch = [
        pltpu.VMEM(proj_shape, F32), pltpu.VMEM(proj_shape, F32),
        pltpu.VMEM((tile, D_MODEL), BF16), pltpu.VMEM((tile, D_MODEL), BF16),
        pltpu.VMEM((tile, MIX_WIDTH), BF16), pltpu.VMEM((tile, MIX_WIDTH), BF16),
        pltpu.VMEM((tile, ATTN_WIDTH), BF16),
        pltpu.VMEM((WINDOW + tile, KV_WIDTH), F32),
        pltpu.VMEM((WINDOW + tile, KV_WIDTH), F32),
        pltpu.VMEM((4, WINDOW + tile, KV_WIDTH), BF16),
        pltpu.VMEM((4, WINDOW + tile, KV_WIDTH), BF16),
        pltpu.VMEM((SSD_STATE, SSD_WIDTH), F32),
        pltpu.VMEM((MEM_TOKENS, MEM_WIDTH), BF16),
        pltpu.VMEM((MEM_TOKENS, MEM_WIDTH), BF16),
    ]
    return pl.pallas_call(
        functools.partial(_layer_kernel, tile=tile, n_tiles=n_t, total_steps=n_steps, n_sub=n_sub,
                          has_past=has_past),
        grid=(n_steps + 1,),
        in_specs=in_specs,
        out_specs=out_specs,
        out_shape=out_shape,
        scratch_shapes=scratch,
        compiler_params=pltpu.CompilerParams(
            dimension_semantics=("arbitrary",),
            vmem_limit_bytes=VMEM_LIMIT_BYTES),
        name="layer_sample" if has_past else "layer_prompt",
    )(*inputs)


def _prep_params(g_norm, w_in, g_q, g_k, sinks, conv_w, conv_b, dt_bias, a_log, d_skip, g_ssd, g_mq, w_out):
    o_z = ATTN_WIDTH + 2 * KV_WIDTH + ATTN_WIDTH
    o_x = o_z + SSD_WIDTH
    o_b = o_x + SSD_WIDTH
    o_c = o_b + SSD_GROUPS * SSD_STATE
    o_dt = o_c + SSD_GROUPS * SSD_STATE
    o_mq = o_dt + SSD_HEADS
    w_dt = w_in[:, o_dt:o_mq]
    zeros = lambda n: jnp.zeros((D_MODEL, n), w_in.dtype)
    groups = [
        w_in[:, 0:o_z],
        jnp.concatenate([w_in[:, o_z:o_x], w_dt, zeros(GROUP_W - SSD_WIDTH - SSD_HEADS)], axis=1),
        w_in[:, o_x:o_c],
        jnp.concatenate([w_in[:, o_c:o_dt], w_in[:, o_mq:]], axis=1),
    ]
    pad = LANES - SSD_HEADS
    return {
        "w_in": jnp.stack(groups).astype(BF16),
        "w_dt": w_dt.T.astype(BF16),
        "w_out": w_out.astype(BF16),
        "g_norm": g_norm.reshape(1, D_MODEL),
        "gq": (jnp.tile(g_q, N_Q_HEADS) * (1.0 / math.sqrt(HEAD_DIM))).reshape(1, ATTN_WIDTH),
        "gk": jnp.tile(g_k, N_KV_HEADS).reshape(1, KV_WIDTH),
        "sinks": sinks,
        "conv_w": conv_w,
        "conv_b": conv_b.reshape(1, CONV_CH),
        "dtb_row": jnp.pad(dt_bias, (0, pad)).reshape(1, LANES),
        "dtb_col": dt_bias.reshape(SSD_HEADS, 1),
        "alog_row": jnp.pad(a_log, (0, pad)).reshape(1, LANES),
        "alog_col": a_log.reshape(SSD_HEADS, 1),
        "d_skip": jnp.repeat(d_skip, SSD_HEAD_DIM).reshape(1, SSD_WIDTH),
        "g_ssd": g_ssd.reshape(1, SSD_WIDTH),
        "g_mq": g_mq.reshape(1, MEM_HEAD_DIM),
    }


PROMPT_TILE = 256


def kernel(x_prompt, x_sample, cache_win_k, cache_win_v, state_ssm, state_conv, cache_mem_k, cache_mem_v,
           mem_prompt, g_norm, w_in, g_q, g_k, sinks, conv_w, conv_b, dt_bias, a_log, d_skip, g_ssd,
           g_mem, w_mem_k, w_mem_v, g_mk, g_mq, w_out):
    depth = w_in.shape[0]
    assert depth == 1
    l = 0
    bp, t_p, _ = x_prompt.shape
    bs, t_s, _ = x_sample.shape
    params = _prep_params(g_norm[l], w_in[l], g_q[l], g_k[l], sinks[l], conv_w[l], conv_b[l], dt_bias[l],
                          a_log[l], d_skip[l], g_ssd[l], g_mq[l], w_out[l])

    mk_p, mv_p = _memory_kv(mem_prompt, g_mem[l], w_mem_k[l], w_mem_v[l], g_mk[l])

    yp, kp, vp, sp, cp = _layer(
        x_prompt, jnp.arange(t_p, dtype=jnp.int32), mk_p, mv_p, None, params, tile=min(PROMPT_TILE, t_p))

    past = (cache_win_k[l].reshape(bs, WINDOW, KV_WIDTH), cache_win_v[l].reshape(bs, WINDOW, KV_WIDTH),
            state_ssm[l].reshape(bs, SSD_WIDTH, SSD_STATE), state_conv[l])
    ys, ks, vs, ss, cs = _layer(
        x_sample, PAST_LEN + jnp.arange(t_s, dtype=jnp.int32),
        cache_mem_k[l], cache_mem_v[l],
        past, params, tile=t_s)

    kv5 = lambda a, b: a.reshape(1, b, WINDOW, N_KV_HEADS, HEAD_DIM)
    ssm5 = lambda a, b: a.reshape(1, b, SSD_HEADS, SSD_HEAD_DIM, SSD_STATE)
    mem5 = lambda a: a[None]
    return (yp, ys,
            kv5(kp, bp), kv5(vp, bp), ssm5(sp, bp), cp[None],
            mem5(mk_p), mem5(mv_p),
            kv5(ks, bs), kv5(vs, bs), ssm5(ss, bs), cs[None])
```

```python
import functools
import math

import numpy as np
import jax
import jax.numpy as jnp
from jax import lax
from jax.experimental import pallas as pl
from jax.experimental.pallas import tpu as pltpu

D_MODEL = 1024
CHUNK = 64
HEAD_DIM = 64
N_Q_HEADS = 8
N_KV_HEADS = 2
ATTN_WIDTH = N_Q_HEADS * HEAD_DIM
KV_WIDTH = N_KV_HEADS * HEAD_DIM
WINDOW = 128
ROPE_THETA = 10000.0
SSD_WIDTH = D_MODEL
SSD_HEAD_DIM = 64
SSD_HEADS = SSD_WIDTH // SSD_HEAD_DIM
SSD_GROUPS = 2
SSD_STATE = 128
CONV_WIDTH = 4
CONV_CH = SSD_WIDTH + 2 * SSD_GROUPS * SSD_STATE
MEM_TOKENS = 256
MEM_HEADS = 4
MEM_HEAD_DIM = 128
MEM_WIDTH = MEM_HEADS * MEM_HEAD_DIM
MIX_WIDTH = 2 * D_MODEL
EPS = 1e-6
PAST_LEN = 2048

LANES = 128
SUBLANES = 8
VMEM_LIMIT_BYTES = 56 * 1024 * 1024

N_GROUPS = 4
GROUP_W = 1280
G0_Q, G0_K, G0_V, G0_GA = 0, ATTN_WIDTH, ATTN_WIDTH + KV_WIDTH, ATTN_WIDTH + 2 * KV_WIDTH
G1_Z, G1_DT = 0, SSD_WIDTH
G2_X, G2_B = 0, SSD_WIDTH
G3_C, G3_MQ, G3_GM = 0, SSD_GROUPS * SSD_STATE, SSD_GROUPS * SSD_STATE + MEM_WIDTH
PIECE_W = 256
N_PIECES = GROUP_W // PIECE_W
OUT_PIECES = D_MODEL // PIECE_W
HEAD_ROWS = SUBLANES

F32 = jnp.float32
BF16 = jnp.bfloat16
NEG_BIG = -1e30
LOG2E = math.log2(math.e)


def _dot(a, b):
    return jnp.dot(a, b, preferred_element_type=F32)


def _dot_nt(a, b):
    return lax.dot_general(a, b, (((1,), (1,)), ((), ())), preferred_element_type=F32)


def _dot_tn(a, b):
    return lax.dot_general(a, b, (((0,), (0,)), ((), ())), preferred_element_type=F32)


def _split3(x):
    h1 = x.astype(BF16)
    r1 = x - h1.astype(F32)
    h2 = r1.astype(BF16)
    r2 = r1 - h2.astype(F32)
    return h1, h2, r2.astype(BF16)


def _dot_sel_rhs(x, sel):
    a, b, c = _split3(x)
    return (_dot(c, sel) + _dot(b, sel)) + _dot(a, sel)


def _dot_sel_lhs(sel, x):
    a, b, c = _split3(x)
    return (_dot(sel, c) + _dot(sel, b)) + _dot(sel, a)


def _seg_sumsq(x, seg):
    return _dot((x * x).astype(BF16), seg)


def _silu(x):
    h = 0.5 * x
    return h + h * jnp.tanh(h)


def _softplus(x):
    return jnp.maximum(x, 0.0) + jnp.log(1.0 + jnp.exp(-jnp.abs(x)))


def _rms_scale(x):
    return lax.rsqrt(jnp.mean(x * x, axis=-1, keepdims=True) + EPS)


def _softmax_rows_log2(sc, extra=None):
    m = jnp.max(sc, axis=-1, keepdims=True)
    if extra is not None:
        m = jnp.maximum(m, extra)
    e = jnp.exp2(sc - m)
    den = jnp.sum(e, axis=-1, keepdims=True)
    if extra is not None:
        den = den + jnp.exp2(extra - m)
    return (e * (1.0 / den)).astype(BF16)


def _rope(x, cos, sin_signed, first_half):
    partner = jnp.where(first_half, pltpu.roll(x, LANES - HEAD_DIM // 2, 1), pltpu.roll(x, HEAD_DIM // 2, 1))
    return x * cos + partner * sin_signed


def _memory_kv_kernel(mem_ref, g_mem_ref, wk_ref, wv_ref, g_mk_ref, k_out, v_out):
    x = mem_ref[0]
    hb = (x * _rms_scale(x) * g_mem_ref[...]).astype(BF16)
    k = _dot(hb, wk_ref[...])
    v = _dot(hb, wv_ref[...])
    for hh in range(MEM_HEADS):
        sl = slice(hh * MEM_HEAD_DIM, (hh + 1) * MEM_HEAD_DIM)
        kh = k[:, sl]
        k_out[0, :, hh, :] = kh * _rms_scale(kh) * g_mk_ref[...]
        v_out[0, :, hh, :] = v[:, sl]


def _memory_kv(mem, g_mem, w_k, w_v, g_mk):
    b = mem.shape[0]
    full = lambda shape: pl.BlockSpec(shape, lambda i: (0,) * len(shape))
    return pl.pallas_call(
        _memory_kv_kernel,
        grid=(b,),
        in_specs=[
            pl.BlockSpec((1, MEM_TOKENS, D_MODEL), lambda i: (i, 0, 0)),
            full((1, D_MODEL)),
            full((D_MODEL, MEM_WIDTH)),
            full((D_MODEL, MEM_WIDTH)),
            full((1, MEM_HEAD_DIM)),
        ],
        out_specs=[
            pl.BlockSpec((1, MEM_TOKENS, MEM_HEADS, MEM_HEAD_DIM), lambda i: (i, 0, 0, 0)),
            pl.BlockSpec((1, MEM_TOKENS, MEM_HEADS, MEM_HEAD_DIM), lambda i: (i, 0, 0, 0)),
        ],
        out_shape=[jax.ShapeDtypeStruct((b, MEM_TOKENS, MEM_HEADS, MEM_HEAD_DIM), F32)] * 2,
        compiler_params=pltpu.CompilerParams(dimension_semantics=("arbitrary",)),
        name="memory_kv",
    )(mem, g_mem.reshape(1, D_MODEL), w_k.astype(BF16), w_v.astype(BF16), g_mk.reshape(1, MEM_HEAD_DIM))


def _layer_kernel(*refs, tile, n_tiles, total_steps, n_sub, has_past):
    x_first_ref, x_second_ref, x_ahead_ref, x_prev_ref, cos_ref, sin_ref, memk_ref, memv_ref, *rest = refs
    if has_past:
        wink_ref, winv_ref, ssm_ref, conv_ref, *rest = rest
    else:
        wink_ref = winv_ref = ssm_ref = conv_ref = None
    (w_in_ref, wdt_ref, w_out_ref,
     g_norm_ref, gq_ref, gk_ref, sinks_ref, convw_ref, convb_ref,
     dtb_row_ref, dtb_col_ref, alog_row_ref, alog_col_ref, dskip_ref, g_ssd_ref, g_mq_ref,
     segq_ref, segk_ref, tri_ref, duptri_ref,
     y_ref, newk_ref, newv_ref, ssm_out_ref, conv_out_ref,
     proj_a, proj_b, hb_a, hb_b, mix_a, mix_b,
     qs, kwin, vwin, kr, vr, state_t, memk_bf, memv_bf) = rest

    T = tile
    n_chunks = T // CHUNK
    n_blocks = n_chunks // n_sub
    R = n_sub * CHUNK
    groups_per_block = N_GROUPS // n_blocks
    s = pl.program_id(0)
    t = s % n_tiles
    body_rows = pl.ds(HEAD_ROWS, T)

    def normed_input(x_ref, hb):
        x = x_ref[0]
        hb[...] = (x * _rms_scale(x) * g_norm_ref[...]).astype(BF16)

    def project_piece(hb, proj, g, i):
        cols = slice(i * PIECE_W, (i + 1) * PIECE_W)
        proj[g, body_rows, cols] = _dot(hb[...], w_in_ref[g, :, cols])

    @pl.when(s == 0)
    def _prologue():
        normed_input(x_first_ref, hb_a)
        normed_input(x_second_ref, hb_b)
        mix_b[...] = jnp.zeros((T, MIX_WIDTH), BF16)
        for g in range(N_GROUPS):
            for i in range(N_PIECES):
                project_piece(hb_a, proj_a, g, i)

    lane = lax.broadcasted_iota(jnp.int32, (1, LANES), 1)
    first_half = (lane % HEAD_DIM) < (HEAD_DIM // 2)
    lo_half = lane < HEAD_DIM
    row64 = lax.broadcasted_iota(jnp.int32, (CHUNK, LANES), 0)
    lane64 = lax.broadcasted_iota(jnp.int32, (CHUNK, LANES), 1)
    causal_dup = row64 >= (lane64 % CHUNK)
    lo_half64 = lane64 < CHUNK
    row128 = lax.broadcasted_iota(jnp.int32, (2 * CHUNK, 1), 0)
    key_idx = lax.broadcasted_iota(jnp.int32, (1, WINDOW + CHUNK), 1)
    gw = SSD_WIDTH // SSD_GROUPS
    n_pairs = SSD_HEADS // 2
    pairs_per_group = n_pairs // SSD_GROUPS
    mem_scale = LOG2E / math.sqrt(MEM_HEAD_DIM)
    bc_w = SSD_GROUPS * SSD_STATE

    def expand_heads(x):
        tiles = []
        for pair in range(n_pairs):
            even = jnp.broadcast_to(x[:, 2 * pair:2 * pair + 1], (x.shape[0], LANES))
            odd = jnp.broadcast_to(x[:, 2 * pair + 1:2 * pair + 2], (x.shape[0], LANES))
            tiles.append(jnp.where(lo_half, even, odd))
        return jnp.concatenate(tiles, axis=1)

    def build_masked(src, dst, rows):
        full = src[rows, :]
        swapped = pltpu.roll(full, HEAD_DIM, 1)
        zero = jnp.zeros_like(full)
        dst[0, rows, :] = jnp.where(lo_half, full, zero).astype(BF16)
        dst[1, rows, :] = jnp.where(lo_half, zero, swapped).astype(BF16)
        dst[2, rows, :] = jnp.where(lo_half, swapped, zero).astype(BF16)
        dst[3, rows, :] = jnp.where(lo_half, zero, full).astype(BF16)

    def out_piece(mix_prev, pp):
        cols = slice(pp * PIECE_W, (pp + 1) * PIECE_W)
        y_ref[0, :, cols] = x_prev_ref[0, :, cols] + _dot(mix_prev[...], w_out_ref[:, cols])

    def step(proj, hb, mix, proj_nxt, hb_nxt, mix_prev):
        @pl.when(t == 0)
        def _init():
            zero_head = jnp.zeros((HEAD_ROWS, GROUP_W), F32)
            proj[2, 0:HEAD_ROWS, :] = zero_head
            proj[3, 0:HEAD_ROWS, :] = zero_head
            if has_past:
                kwin[0:WINDOW, :] = wink_ref[0]
                vwin[0:WINDOW, :] = winv_ref[0]
                state_t[...] = ssm_ref[0].T
                tail = conv_ref[0]
                r_lo = HEAD_ROWS - (CONV_WIDTH - 1)
                proj[2, r_lo:HEAD_ROWS, :] = tail[:, 0:GROUP_W]
                proj[3, r_lo:HEAD_ROWS, 0:CONV_CH - GROUP_W] = tail[:, GROUP_W:CONV_CH]
                for src, dst in ((kwin, kr), (vwin, vr)):
                    build_masked(src, dst, slice(0, WINDOW))
            else:
                state_t[...] = jnp.zeros((SSD_STATE, SSD_WIDTH), F32)
                kwin[0:WINDOW, :] = jnp.zeros((WINDOW, KV_WIDTH), F32)
                vwin[0:WINDOW, :] = jnp.zeros((WINDOW, KV_WIDTH), F32)
                for v in range(4):
                    kr[v, 0:WINDOW, :] = jnp.zeros((WINDOW, KV_WIDTH), BF16)
                    vr[v, 0:WINDOW, :] = jnp.zeros((WINDOW, KV_WIDTH), BF16)
            for hh in range(MEM_HEADS):
                sl = slice(hh * MEM_HEAD_DIM, (hh + 1) * MEM_HEAD_DIM)
                memk_bf[:, sl] = memk_ref[0, :, hh, :].astype(BF16)
                memv_bf[:, sl] = memv_ref[0, :, hh, :].astype(BF16)

        if n_tiles > 1:
            @pl.when(t > 0)
            def _shift():
                kwin[0:WINDOW, :] = kwin[T:T + WINDOW, :]
                vwin[0:WINDOW, :] = vwin[T:T + WINDOW, :]
                for v in range(4):
                    kr[v, 0:WINDOW, :] = kr[v, T:T + WINDOW, :]
                    vr[v, 0:WINDOW, :] = vr[v, T:T + WINDOW, :]
                proj[2, 0:HEAD_ROWS, :] = proj_nxt[2, T:T + HEAD_ROWS, :]
                proj[3, 0:HEAD_ROWS, :] = proj_nxt[3, T:T + HEAD_ROWS, :]

        a_col = -jnp.exp(alog_col_ref[...]) * LOG2E
        a_row = -jnp.exp(alog_row_ref[...]) * LOG2E
        tri = tri_ref[...]
        d_skip = dskip_ref[...]
        subs = range(n_sub)

        out_per_block = OUT_PIECES // n_blocks

        def block_body(i):
            r0 = i * R
            rows = pl.ds(r0, R)
            prow = pl.ds(r0 + HEAD_ROWS, R)
            wrow = pl.ds(r0 + WINDOW, R)
            sub_rows = [slice(u * CHUNK, (u + 1) * CHUNK) for u in subs]
            krows = [pl.ds(r0 + u * CHUNK, WINDOW + CHUNK) for u in subs]

            pieces = [(i * groups_per_block + gg, pp) for gg in range(groups_per_block) for pp in range(N_PIECES)]
            gap_share = (1, 3, 5, 1)
            per_gap = [len(pieces) * sh // sum(gap_share) for sh in gap_share]

            def issue(n):
                for _ in range(n):
                    g, pp = pieces.pop(0)
                    project_piece(hb_nxt, proj_nxt, g, pp)

            issue(per_gap[0])

            q = proj[0, prow, G0_Q:G0_Q + ATTN_WIDTH]
            k = proj[0, prow, G0_K:G0_K + KV_WIDTH]
            dt_row = _softplus(proj[1, prow, G1_DT:G1_DT + LANES] + dtb_row_ref[...])
            mem_q = []
            for hh in range(MEM_HEADS):
                mqh = proj[3, prow, G3_MQ + hh * MEM_HEAD_DIM:G3_MQ + (hh + 1) * MEM_HEAD_DIM]
                mem_q.append((mqh * _rms_scale(mqh) * g_mq_ref[...]).astype(BF16))

            seg = segq_ref[...]
            q_blocks = [q[:, j * 2 * LANES:(j + 1) * 2 * LANES] for j in range(ATTN_WIDTH // (2 * LANES))]
            q_ss = [_seg_sumsq(qj, seg) for qj in q_blocks]
            k_ss = _seg_sumsq(k, segk_ref[...])
            dtx = expand_heads(dt_row)
            acum_n = _dot_sel_lhs(tri, dt_row * a_row)
            mem_sc = [_dot_nt(mem_q[hh], memk_bf[:, hh * MEM_HEAD_DIM:(hh + 1) * MEM_HEAD_DIM])
                      for hh in range(MEM_HEADS)]
            dt_t = _dot_nt(wdt_ref[...], hb[rows, :])

            issue(per_gap[1])
            yield

            def conv(group, c0, width, ch0):
                acc = convb_ref[:, ch0:ch0 + width]
                blk = proj[group, pl.ds(r0, HEAD_ROWS + R), c0:c0 + width]
                for j in range(CONV_WIDTH):
                    back = CONV_WIDTH - 1 - j
                    tap = blk if back == 0 else pltpu.roll(blk, back, 0)
                    acc = acc + tap[HEAD_ROWS:HEAD_ROWS + R, :] * convw_ref[j:j + 1, ch0:ch0 + width]
                return _silu(acc)

            b_all = conv(2, G2_B, bc_w, SSD_WIDTH).astype(BF16)
            c_all = conv(3, G3_C, bc_w, SSD_WIDTH + bc_w).astype(BF16)
            xs = conv(2, G2_X, SSD_WIDTH, 0)

            cos = cos_ref[rows, :]
            sin_s = sin_ref[rows, :]
            for j, (qj, ss) in enumerate(zip(q_blocks, q_ss)):
                qn = qj * lax.rsqrt(ss * (1.0 / HEAD_DIM) + EPS) * gq_ref[:, j * 2 * LANES:(j + 1) * 2 * LANES]
                for jj in range(2):
                    c0 = j * 2 * LANES + jj * LANES
                    qs[rows, c0:c0 + LANES] = _rope(qn[:, jj * LANES:(jj + 1) * LANES], cos, sin_s,
                                                    first_half).astype(BF16)
            kn = k * lax.rsqrt(k_ss * (1.0 / HEAD_DIM) + EPS) * gk_ref[...]
            kwin[wrow, :] = _rope(kn, cos, sin_s, first_half)
            vwin[wrow, :] = proj[0, prow, G0_V:G0_V + KV_WIDTH]
            build_masked(kwin, kr, wrow)
            build_masked(vwin, vr, wrow)

            mem_p = [_softmax_rows_log2(mem_sc[hh] * mem_scale) for hh in range(MEM_HEADS)]
            xdt = xs * dtx
            da_t3 = _split3(_softplus(dt_t + dtb_col_ref[...]) * a_col)

            bgs = [[b_all[sub_rows[u], g * SSD_STATE:(g + 1) * SSD_STATE] for g in range(SSD_GROUPS)] for u in subs]
            cgs = [[c_all[sub_rows[u], g * SSD_STATE:(g + 1) * SSD_STATE] for g in range(SSD_GROUPS)] for u in subs]
            cbs = [[_dot_nt(cgs[u][g], jnp.concatenate([bgs[u][g], bgs[u][g]], axis=0))
                    for g in range(SSD_GROUPS)] for u in subs]
            cross_cb = [_dot_nt(cgs[1][g], bgs[0][g]) for g in range(SSD_GROUPS)] if n_sub == 2 else None
            y_in = [_dot(c_all[:, g * SSD_STATE:(g + 1) * SSD_STATE], state_t[:, g * gw:(g + 1) * gw].astype(BF16))
                    for g in range(SSD_GROUPS)]
            acum = expand_heads(acum_n)
            dup = duptri_ref[...]
            acum_t = (_dot(da_t3[2], dup) + _dot(da_t3[1], dup)) + _dot(da_t3[0], dup)
            att_sc = []
            for u in subs:
                for g in range(N_KV_HEADS):
                    base = g * 2 * LANES
                    qrows = pl.ds(r0 + u * CHUNK, CHUNK)
                    qg = jnp.concatenate([qs[qrows, base:base + LANES], qs[qrows, base + LANES:base + 2 * LANES]],
                                         axis=0)
                    for pos in range(2):
                        att_sc.append(_dot_nt(qg, kr[2 * g + pos, krows[u], :]))
            mem_o = [_dot(mem_p[hh], memv_bf[:, hh * MEM_HEAD_DIM:(hh + 1) * MEM_HEAD_DIM])
                     for hh in range(MEM_HEADS)]

            issue(per_gap[2])
            for pp in range(i * out_per_block, i * out_per_block + out_per_block // 2):
                out_piece(mix_prev, pp)
            yield

            att_p = []
            for u in subs:
                needs_mask = (not has_past) and (i * R + u * CHUNK < WINDOW)
                if needs_mask:
                    first_valid = jnp.where(t == 0, WINDOW - (r0 + u * CHUNK), 0)
                    key_ok = key_idx >= first_valid
                for g in range(N_KV_HEADS):
                    for pos in range(2):
                        hd0 = 4 * g + pos
                        sink = jnp.where(row128 < CHUNK, sinks_ref[hd0], sinks_ref[hd0 + 2]) * LOG2E
                        sc = att_sc[(u * N_KV_HEADS + g) * 2 + pos]
                        if needs_mask:
                            sc = jnp.where(key_ok, sc, NEG_BIG)
                        att_p.append(_softmax_rows_log2(sc, sink))

            exp_acum = jnp.exp2(acum)
            a_last = [acum[u * CHUNK + CHUNK - 1:(u + 1) * CHUNK, :] for u in subs]
            dec_end = [jnp.exp2(a_last[u]) for u in subs]
            x_end_f = [xdt[sub_rows[u], :] * jnp.exp2(a_last[u] - acum[sub_rows[u], :]) for u in subs]
            x_end = [xe.astype(BF16) for xe in x_end_f]
            if n_sub == 2:
                x_end_stack = jnp.concatenate([(x_end_f[0] * dec_end[1]).astype(BF16), x_end[1]], axis=0)
            ws, xbds = [], []
            zero_bd = jnp.zeros((2 * CHUNK, LANES), BF16)
            for u in subs:
                act = acum_t[:, u * LANES:(u + 1) * LANES]
                for quad in range(n_pairs // 2):
                    w_parts, bd_parts = [], []
                    for pair in (2 * quad, 2 * quad + 1):
                        g = pair // pairs_per_group
                        cols = slice(pair * LANES, (pair + 1) * LANES)
                        rowb = jnp.where(lane < CHUNK, act[2 * pair:2 * pair + 1, :], act[2 * pair + 1:2 * pair + 2, :])
                        dec = jnp.exp2(jnp.where(causal_dup, acum[sub_rows[u], cols] - rowb, -jnp.inf))
                        w_parts.append((cbs[u][g] * dec).astype(BF16))
                        xp = xdt[sub_rows[u], cols]
                        zero = jnp.zeros_like(xp)
                        bd_parts.append(jnp.concatenate(
                            [jnp.where(lo_half64, xp, zero), jnp.where(lo_half64, zero, xp)], axis=0).astype(BF16))
                    ws.append(jnp.concatenate(w_parts, axis=1))
                    xbds.append(jnp.concatenate(
                        [jnp.concatenate([bd_parts[0], zero_bd], axis=1),
                         jnp.concatenate([zero_bd, bd_parts[1]], axis=1)], axis=0))

            for hh in range(MEM_HEADS):
                gm = proj[3, prow, G3_GM + hh * MEM_HEAD_DIM:G3_GM + (hh + 1) * MEM_HEAD_DIM]
                c0 = ATTN_WIDTH + SSD_WIDTH + hh * MEM_HEAD_DIM
                mix[rows, c0:c0 + MEM_HEAD_DIM] = (mem_o[hh] * _silu(gm)).astype(BF16)

            att_o = []
            for u in subs:
                for g in range(N_KV_HEADS):
                    j0 = (u * N_KV_HEADS + g) * 2
                    att_o.append(_dot(att_p[j0], vr[2 * g, krows[u], :]) + _dot(att_p[j0 + 1], vr[2 * g + 1, krows[u], :]))
            for g in range(SSD_GROUPS):
                gcols = slice(g * gw, (g + 1) * gw)
                st = state_t[:, gcols]
                if n_sub == 1:
                    st = st * dec_end[0][:, gcols] + _dot_tn(bgs[0][g], x_end[0][:, gcols])
                else:
                    st = st * (dec_end[0][:, gcols] * dec_end[1][:, gcols]) + _dot_tn(
                        b_all[:, g * SSD_STATE:(g + 1) * SSD_STATE], x_end_stack[:, gcols])
                state_t[:, gcols] = st
            cross = ([_dot(cross_cb[g].astype(BF16), x_end[0][:, g * gw:(g + 1) * gw]) for g in range(SSD_GROUPS)]
                     if n_sub == 2 else None)
            y_diag = [_dot(w, xbd) for w, xbd in zip(ws, xbds)]

            issue(len(pieces))
            for pp in range(i * out_per_block + out_per_block // 2, (i + 1) * out_per_block):
                out_piece(mix_prev, pp)
            yield

            for u in subs:
                for g in range(N_KV_HEADS):
                    base = g * 2 * LANES
                    orow = pl.ds(r0 + u * CHUNK, CHUNK)
                    ga = proj[0, pl.ds(r0 + u * CHUNK + HEAD_ROWS, CHUNK), G0_GA + base:G0_GA + base + 2 * LANES]
                    o = att_o[u * N_KV_HEADS + g]
                    mix[orow, base:base + LANES] = (o[0:CHUNK] * _silu(ga[:, 0:LANES])).astype(BF16)
                    mix[orow, base + LANES:base + 2 * LANES] = (o[CHUNK:] * _silu(ga[:, LANES:])).astype(BF16)

            y_rows = []
            for u in subs:
                ys = []
                for pair in range(n_pairs):
                    g, hp = divmod(pair, pairs_per_group)
                    cols = slice(pair * LANES, (pair + 1) * LANES)
                    lc = slice(hp * LANES, (hp + 1) * LANES)
                    y_off = y_in[g][sub_rows[u], lc]
                    if u == 1:
                        y_off = y_off * dec_end[0][:, cols] + cross[g][:, lc]
                    yd = y_diag[(u * n_pairs + pair) // 2][:, (pair % 2) * LANES:(pair % 2 + 1) * LANES]
                    ys.append(yd + y_off * exp_acum[sub_rows[u], cols] + d_skip[:, cols] * xs[sub_rows[u], cols])
                y_rows.append(jnp.concatenate(ys, axis=1))
            y_all = y_rows[0] if n_sub == 1 else jnp.concatenate(y_rows, axis=0)
            z = proj[1, prow, G1_Z:G1_Z + SSD_WIDTH]
            yg = y_all * _silu(z)
            mix[rows, ATTN_WIDTH:ATTN_WIDTH + SSD_WIDTH] = (yg * _rms_scale(yg) * g_ssd_ref[...]).astype(BF16)

            xa = x_ahead_ref[0, rows, :]
            hb[rows, :] = (xa * _rms_scale(xa) * g_norm_ref[...]).astype(BF16)

        n_segments = 4
        gens = [block_body(blk) for blk in range(n_blocks)]
        for slot in range(2 * (n_blocks - 1) + n_segments):
            for blk in reversed(range(n_blocks)):
                if 0 <= slot - 2 * blk < n_segments:
                    next(gens[blk], None)

        @pl.when(t == n_tiles - 1)
        def _final():
            newk_ref[0] = kwin[T:T + WINDOW, :]
            newv_ref[0] = vwin[T:T + WINDOW, :]
            ssm_out_ref[0] = state_t[...].T
            r_lo = T + HEAD_ROWS - (CONV_WIDTH - 1)
            conv_out_ref[0, :, 0:GROUP_W] = proj[2, r_lo:T + HEAD_ROWS, :]
            conv_out_ref[0, :, GROUP_W:CONV_CH] = proj[3, r_lo:T + HEAD_ROWS, 0:CONV_CH - GROUP_W]

    parity = s % 2

    @pl.when((parity == 0) & (s < total_steps))
    def _even():
        step(proj_a, hb_a, mix_a, proj_b, hb_b, mix_b)

    @pl.when((parity == 1) & (s < total_steps))
    def _odd():
        step(proj_b, hb_b, mix_b, proj_a, hb_a, mix_a)

    @pl.when(s == total_steps)
    def _flush():
        for pp in range(OUT_PIECES):
            out_piece(mix_b if total_steps % 2 == 0 else mix_a, pp)


def _constants(tile, n_sub):
    n_chunks = tile // CHUNK
    rows_per_block = n_sub * CHUNK
    idx = np.arange
    segq = (idx(2 * LANES)[:, None] // HEAD_DIM == idx(2 * LANES)[None, :] // HEAD_DIM)
    segk = (idx(KV_WIDTH)[:, None] // HEAD_DIM == idx(KV_WIDTH)[None, :] // HEAD_DIM)
    r = idx(rows_per_block)
    tri = (r[:, None] // CHUNK == r[None, :] // CHUNK) & (r[:, None] >= r[None, :])
    src = idx(rows_per_block)[:, None]
    dst = idx(n_sub * LANES)[None, :]
    duptri = (src // CHUNK == dst // LANES) & (src % CHUNK <= dst % CHUNK)
    return [jnp.asarray(m.astype(np.float32), dtype=BF16) for m in (segq, segk, tri, duptri)]


def _layer(x, positions, mem_k, mem_v, past, params, *, tile):
    b, seq, _ = x.shape
    n_t = seq // tile
    n_steps = b * n_t
    has_past = past is not None
    n_chunks = tile // CHUNK
    n_sub = 2 if n_chunks % 2 == 0 else 1
    assert seq % tile == 0 and tile % CHUNK == 0 and N_GROUPS % (n_chunks // n_sub) == 0
    assert n_t == 1 or tile >= WINDOW

    half = HEAD_DIM // 2
    freqs = ROPE_THETA ** (-jnp.arange(half, dtype=F32) / half)
    ang = positions.astype(F32)[:, None] * freqs[None, :]
    cos, sin = jnp.cos(ang), jnp.sin(ang)
    cos_t = jnp.tile(cos, (1, LANES // half))
    sin_t = jnp.tile(jnp.concatenate([-sin, sin], axis=1), (1, LANES // HEAD_DIM))

    consts = _constants(tile, n_sub)

    def full(a):
        nd = a.ndim
        return pl.BlockSpec(a.shape, lambda i, _n=nd: (0,) * _n, pipeline_mode=pl.Buffered(1))

    def row_of(i):
        return jnp.minimum(i, n_steps - 1) // n_t

    def per_row(a):
        nd = a.ndim
        return pl.BlockSpec((1,) + a.shape[1:], lambda i, _n=nd: (row_of(i),) + (0,) * (_n - 1))

    def tile_at(j):
        j = jnp.clip(j, 0, n_steps - 1)
        return (j // n_t, j % n_t, 0)

    vec_params = [params[k] for k in ("g_norm", "gq", "gk")]
    tail_params = [params[k] for k in ("conv_w", "conv_b", "dtb_row", "dtb_col", "alog_row", "alog_col",
                                       "d_skip", "g_ssd", "g_mq")]
    past_inputs = list(past) if has_past else []
    inputs = [x, x, x, x, cos_t, sin_t, mem_k, mem_v, *past_inputs,
              params["w_in"], params["w_dt"], params["w_out"],
              *vec_params, params["sinks"], *tail_params, *consts]
    in_specs = [
        pl.BlockSpec((1, tile, D_MODEL), lambda i: tile_at(0)),
        pl.BlockSpec((1, tile, D_MODEL), lambda i: tile_at(1)),
        pl.BlockSpec((1, tile, D_MODEL), lambda i: tile_at(i + 2)),
        pl.BlockSpec((1, tile, D_MODEL), lambda i: tile_at(i - 1)),
        pl.BlockSpec((tile, LANES), lambda i: (i % n_t, 0)),
        pl.BlockSpec((tile, LANES), lambda i: (i % n_t, 0)),
        per_row(mem_k), per_row(mem_v),
        *[per_row(p) for p in past_inputs],
        full(params["w_in"]), full(params["w_dt"]), full(params["w_out"]),
        *[full(p) for p in vec_params],
        pl.BlockSpec(memory_space=pltpu.SMEM),
        *[full(p) for p in tail_params],
        *[full(c) for c in consts],
    ]
    out_shape = [
        jax.ShapeDtypeStruct((b, seq, D_MODEL), F32),
        jax.ShapeDtypeStruct((b, WINDOW, KV_WIDTH), F32),
        jax.ShapeDtypeStruct((b, WINDOW, KV_WIDTH), F32),
        jax.ShapeDtypeStruct((b, SSD_WIDTH, SSD_STATE), F32),
        jax.ShapeDtypeStruct((b, CONV_WIDTH - 1, CONV_CH), F32),
    ]
    out_specs = [
        pl.BlockSpec((1, tile, D_MODEL), lambda i: tile_at(i - 1)),
        pl.BlockSpec((1, WINDOW, KV_WIDTH), lambda i: (row_of(i), 0, 0)),
        pl.BlockSpec((1, WINDOW, KV_WIDTH), lambda i: (row_of(i), 0, 0)),
        pl.BlockSpec((1, SSD_WIDTH, SSD_STATE), lambda i: (row_of(i), 0, 0)),
        pl.BlockSpec((1, CONV_WIDTH - 1, CONV_CH), lambda i: (row_of(i), 0, 0)),
    ]
    proj_shape = (N_GROUPS, HEAD_ROWS + tile, GROUP_W)
    scratch = [
        pltpu.VMEM(proj_shape, F32), pltpu.VMEM(proj_shape, F32),
        pltpu.VMEM((tile, D_MODEL), BF16), pltpu.VMEM((tile, D_MODEL), BF16),
        pltpu.VMEM((tile, MIX_WIDTH), BF16), pltpu.VMEM((tile, MIX_WIDTH), BF16),
        pltpu.VMEM((tile, ATTN_WIDTH), BF16),
        pltpu.VMEM((WINDOW + tile, KV_WIDTH), F32),
        pltpu.VMEM((WINDOW + tile, KV_WIDTH), F32),
        pltpu.VMEM((4, WINDOW + tile, KV_WIDTH), BF16),
        pltpu.VMEM((4, WINDOW + tile, KV_WIDTH), BF16),
        pltpu.VMEM((SSD_STATE, SSD_WIDTH), F32),
        pltpu.VMEM((MEM_TOKENS, MEM_WIDTH), BF16),
        pltpu.VMEM((MEM_TOKENS, MEM_WIDTH), BF16),
    ]
    return pl.pallas_call(
        functools.partial(_layer_kernel, tile=tile, n_tiles=n_t, total_steps=n_steps, n_sub=n_sub,
                          has_past=has_past),
        grid=(n_steps + 1,),
        in_specs=in_specs,
        out_specs=out_specs,
        out_shape=out_shape,
        scratch_shapes=scratch,
        compiler_params=pltpu.CompilerParams(
            dimension_semantics=("arbitrary",),
            vmem_limit_bytes=VMEM_LIMIT_BYTES),
        name="layer_sample" if has_past else "layer_prompt",
    )(*inputs)


def _prep_params(g_norm, w_in, g_q, g_k, sinks, conv_w, conv_b, dt_bias, a_log, d_skip, g_ssd, g_mq, w_out):
    o_z = ATTN_WIDTH + 2 * KV_WIDTH + ATTN_WIDTH
    o_x = o_z + SSD_WIDTH
    o_b = o_x + SSD_WIDTH
    o_c = o_b + SSD_GROUPS * SSD_STATE
    o_dt = o_c + SSD_GROUPS * SSD_STATE
    o_mq = o_dt + SSD_HEADS
    w_dt = w_in[:, o_dt:o_mq]
    zeros = lambda n: jnp.zeros((D_MODEL, n), w_in.dtype)
    groups = [
        w_in[:, 0:o_z],
        jnp.concatenate([w_in[:, o_z:o_x], w_dt, zeros(GROUP_W - SSD_WIDTH - SSD_HEADS)], axis=1),
        w_in[:, o_x:o_c],
        jnp.concatenate([w_in[:, o_c:o_dt], w_in[:, o_mq:]], axis=1),
    ]
    pad = LANES - SSD_HEADS
    return {
        "w_in": jnp.stack(groups).astype(BF16),
        "w_dt": w_dt.T.astype(BF16),
        "w_out": w_out.astype(BF16),
        "g_norm": g_norm.reshape(1, D_MODEL),
        "gq": (jnp.tile(g_q, N_Q_HEADS) * (LOG2E / math.sqrt(HEAD_DIM))).reshape(1, ATTN_WIDTH),
        "gk": jnp.tile(g_k, N_KV_HEADS).reshape(1, KV_WIDTH),
        "sinks": sinks,
        "conv_w": conv_w,
        "conv_b": conv_b.reshape(1, CONV_CH),
        "dtb_row": jnp.pad(dt_bias, (0, pad)).reshape(1, LANES),
        "dtb_col": dt_bias.reshape(SSD_HEADS, 1),
        "alog_row": jnp.pad(a_log, (0, pad)).reshape(1, LANES),
        "alog_col": a_log.reshape(SSD_HEADS, 1),
        "d_skip": jnp.repeat(d_skip, SSD_HEAD_DIM).reshape(1, SSD_WIDTH),
        "g_ssd": g_ssd.reshape(1, SSD_WIDTH),
        "g_mq": g_mq.reshape(1, MEM_HEAD_DIM),
    }


PROMPT_TILE = 256


def kernel(x_prompt, x_sample, cache_win_k, cache_win_v, state_ssm, state_conv, cache_mem_k, cache_mem_v,
           mem_prompt, g_norm, w_in, g_q, g_k, sinks, conv_w, conv_b, dt_bias, a_log, d_skip, g_ssd,
           g_mem, w_mem_k, w_mem_v, g_mk, g_mq, w_out):
    depth = w_in.shape[0]
    assert depth == 1
    l = 0
    bp, t_p, _ = x_prompt.shape
    bs, t_s, _ = x_sample.shape
    params = _prep_params(g_norm[l], w_in[l], g_q[l], g_k[l], sinks[l], conv_w[l], conv_b[l], dt_bias[l],
                          a_log[l], d_skip[l], g_ssd[l], g_mq[l], w_out[l])

    mk_p, mv_p = _memory_kv(mem_prompt, g_mem[l], w_mem_k[l], w_mem_v[l], g_mk[l])

    yp, kp, vp, sp, cp = _layer(
        x_prompt, jnp.arange(t_p, dtype=jnp.int32), mk_p, mv_p, None, params, tile=min(PROMPT_TILE, t_p))

    past = (cache_win_k[l].reshape(bs, WINDOW, KV_WIDTH), cache_win_v[l].reshape(bs, WINDOW, KV_WIDTH),
            state_ssm[l].reshape(bs, SSD_WIDTH, SSD_STATE), state_conv[l])
    ys, ks, vs, ss, cs = _layer(
        x_sample, PAST_LEN + jnp.arange(t_s, dtype=jnp.int32),
        cache_mem_k[l], cache_mem_v[l],
        past, params, tile=t_s)

    kv5 = lambda a, b: a.reshape(1, b, WINDOW, N_KV_HEADS, HEAD_DIM)
    ssm5 = lambda a, b: a.reshape(1, b, SSD_HEADS, SSD_HEAD_DIM, SSD_STATE)
    mem5 = lambda a: a[None]
    return (yp, ys,
            kv5(kp, bp), kv5(vp, bp), ssm5(sp, bp), cp[None],
            mem5(mk_p), mem5(mv_p),
            kv5(ks, bs), kv5(vs, bs), ssm5(ss, bs), cs[None])
```

```python
import functools
import math

import numpy as np
import jax
import jax.numpy as jnp
from jax import lax
from jax.experimental import pallas as pl
from jax.experimental.pallas import tpu as pltpu

D_MODEL = 1024
CHUNK = 64
HEAD_DIM = 64
N_Q_HEADS = 8
N_KV_HEADS = 2
ATTN_WIDTH = N_Q_HEADS * HEAD_DIM
KV_WIDTH = N_KV_HEADS * HEAD_DIM
WINDOW = 128
ROPE_THETA = 10000.0
SSD_WIDTH = D_MODEL
SSD_HEAD_DIM = 64
SSD_HEADS = SSD_WIDTH // SSD_HEAD_DIM
SSD_GROUPS = 2
SSD_STATE = 128
CONV_WIDTH = 4
CONV_CH = SSD_WIDTH + 2 * SSD_GROUPS * SSD_STATE
MEM_TOKENS = 256
MEM_HEADS = 4
MEM_HEAD_DIM = 128
MEM_WIDTH = MEM_HEADS * MEM_HEAD_DIM
MIX_WIDTH = 2 * D_MODEL
EPS = 1e-6
PAST_LEN = 2048

LANES = 128
SUBLANES = 8
VMEM_LIMIT_BYTES = 56 * 1024 * 1024

N_GROUPS = 4
GROUP_W = 1280
G0_Q, G0_K, G0_V, G0_GA = 0, ATTN_WIDTH, ATTN_WIDTH + KV_WIDTH, ATTN_WIDTH + 2 * KV_WIDTH
G1_Z, G1_DT = 0, SSD_WIDTH
G2_X, G2_B = 0, SSD_WIDTH
G3_C, G3_MQ, G3_GM = 0, SSD_GROUPS * SSD_STATE, SSD_GROUPS * SSD_STATE + MEM_WIDTH
PIECE_W = 256
N_PIECES = GROUP_W // PIECE_W
OUT_PIECES = D_MODEL // PIECE_W
HEAD_ROWS = SUBLANES

F32 = jnp.float32
BF16 = jnp.bfloat16
NEG_BIG = -1e30
LOG2E = math.log2(math.e)


def _dot(a, b):
    return jnp.dot(a, b, preferred_element_type=F32)


def _dot_nt(a, b):
    return lax.dot_general(a, b, (((1,), (1,)), ((), ())), preferred_element_type=F32)


def _dot_tn(a, b):
    return lax.dot_general(a, b, (((0,), (0,)), ((), ())), preferred_element_type=F32)


def _split3(x):
    h1 = x.astype(BF16)
    r1 = x - h1.astype(F32)
    h2 = r1.astype(BF16)
    r2 = r1 - h2.astype(F32)
    return h1, h2, r2.astype(BF16)


def _dot_sel_rhs(x, sel):
    a, b, c = _split3(x)
    return (_dot(c, sel) + _dot(b, sel)) + _dot(a, sel)


def _dot_sel_lhs(sel, x):
    a, b, c = _split3(x)
    return (_dot(sel, c) + _dot(sel, b)) + _dot(sel, a)


def _seg_sumsq(x, seg):
    return _dot((x * x).astype(BF16), seg)


def _silu(x):
    h = 0.5 * x
    return h + h * jnp.tanh(h)


def _softplus(x):
    return jnp.maximum(x, 0.0) + jnp.log(1.0 + jnp.exp(-jnp.abs(x)))


def _rms_scale(x):
    return lax.rsqrt(jnp.mean(x * x, axis=-1, keepdims=True) + EPS)


def _softmax_rows_log2(sc, extra=None):
    m = jnp.max(sc, axis=-1, keepdims=True)
    if extra is not None:
        m = jnp.maximum(m, extra)
    e = jnp.exp2(sc - m)
    den = jnp.sum(e, axis=-1, keepdims=True)
    if extra is not None:
        den = den + jnp.exp2(extra - m)
    return (e * (1.0 / den)).astype(BF16)


def _rope(x, cos, sin_signed, first_half):
    partner = jnp.where(first_half, pltpu.roll(x, LANES - HEAD_DIM // 2, 1), pltpu.roll(x, HEAD_DIM // 2, 1))
    return x * cos + partner * sin_signed


def _memory_kv_kernel(mem_ref, g_mem_ref, wk_ref, wv_ref, g_mk_ref, k_out, v_out):
    x = mem_ref[0]
    hb = (x * _rms_scale(x) * g_mem_ref[...]).astype(BF16)
    k = _dot(hb, wk_ref[...])
    v = _dot(hb, wv_ref[...])
    for hh in range(MEM_HEADS):
        sl = slice(hh * MEM_HEAD_DIM, (hh + 1) * MEM_HEAD_DIM)
        kh = k[:, sl]
        k_out[0, :, hh, :] = kh * _rms_scale(kh) * g_mk_ref[...]
        v_out[0, :, hh, :] = v[:, sl]


def _memory_kv(mem, g_mem, w_k, w_v, g_mk):
    b = mem.shape[0]
    full = lambda shape: pl.BlockSpec(shape, lambda i: (0,) * len(shape))
    return pl.pallas_call(
        _memory_kv_kernel,
        grid=(b,),
        in_specs=[
            pl.BlockSpec((1, MEM_TOKENS, D_MODEL), lambda i: (i, 0, 0)),
            full((1, D_MODEL)),
            full((D_MODEL, MEM_WIDTH)),
            full((D_MODEL, MEM_WIDTH)),
            full((1, MEM_HEAD_DIM)),
        ],
        out_specs=[
            pl.BlockSpec((1, MEM_TOKENS, MEM_HEADS, MEM_HEAD_DIM), lambda i: (i, 0, 0, 0)),
            pl.BlockSpec((1, MEM_TOKENS, MEM_HEADS, MEM_HEAD_DIM), lambda i: (i, 0, 0, 0)),
        ],
        out_shape=[jax.ShapeDtypeStruct((b, MEM_TOKENS, MEM_HEADS, MEM_HEAD_DIM), F32)] * 2,
        compiler_params=pltpu.CompilerParams(dimension_semantics=("arbitrary",)),
        name="memory_kv",
    )(mem, g_mem.reshape(1, D_MODEL), w_k.astype(BF16), w_v.astype(BF16), g_mk.reshape(1, MEM_HEAD_DIM))


def _layer_kernel(*refs, tile, n_tiles, total_steps, n_sub, has_past):
    x_first_ref, x_ahead_ref, x_prev_ref, cos_ref, sin_ref, memk_ref, memv_ref, *rest = refs
    if has_past:
        wink_ref, winv_ref, ssm_ref, conv_ref, *rest = rest
    else:
        wink_ref = winv_ref = ssm_ref = conv_ref = None
    (w_in_ref, wdt_ref, w_out_ref,
     g_norm_ref, gq_ref, gk_ref, sinks_ref, convw_ref, convb_ref,
     dtb_row_ref, dtb_col_ref, alog_row_ref, alog_col_ref, dskip_ref, g_ssd_ref, g_mq_ref,
     segq_ref, segk_ref, tri_ref, duptri_ref,
     y_ref, newk_ref, newv_ref, ssm_out_ref, conv_out_ref,
     proj_a, proj_b, hb_a, hb_b, mix_a, mix_b, mixo_a, mixo_b,
     qs, kwin, vwin, kr, vr, state_t, memk_bf, memv_bf) = rest

    T = tile
    n_chunks = T // CHUNK
    n_blocks = n_chunks // n_sub
    R = n_sub * CHUNK
    groups_per_block = N_GROUPS // n_blocks
    pair = pl.program_id(0)
    total_pairs = total_steps // 2
    rows_per_pair = 2 if n_tiles == 1 else 1
    body_rows = pl.ds(HEAD_ROWS, T)

    def normed_input(x, hb):
        hb[...] = (x * _rms_scale(x) * g_norm_ref[...]).astype(BF16)

    def project_piece(hb, proj, g, i):
        cols = slice(i * PIECE_W, (i + 1) * PIECE_W)
        proj[g, body_rows, cols] = _dot(hb[...], w_in_ref[g, :, cols])

    @pl.when(pair == 0)
    def _prologue():
        normed_input(x_first_ref[0], hb_a)
        normed_input(x_first_ref[1], hb_b)
        mix_a[...] = jnp.zeros((T, MIX_WIDTH), BF16)
        mix_b[...] = jnp.zeros((T, MIX_WIDTH), BF16)
        for g in range(N_GROUPS):
            for i in range(N_PIECES):
                project_piece(hb_a, proj_a, g, i)

    lane = lax.broadcasted_iota(jnp.int32, (1, LANES), 1)
    first_half = (lane % HEAD_DIM) < (HEAD_DIM // 2)
    lo_half = lane < HEAD_DIM
    row64 = lax.broadcasted_iota(jnp.int32, (CHUNK, LANES), 0)
    lane64 = lax.broadcasted_iota(jnp.int32, (CHUNK, LANES), 1)
    causal_dup = row64 >= (lane64 % CHUNK)
    lo_half64 = lane64 < CHUNK
    row128 = lax.broadcasted_iota(jnp.int32, (2 * CHUNK, 1), 0)
    key_idx = lax.broadcasted_iota(jnp.int32, (1, WINDOW + CHUNK), 1)
    gw = SSD_WIDTH // SSD_GROUPS
    n_pairs = SSD_HEADS // 2
    pairs_per_group = n_pairs // SSD_GROUPS
    mem_scale = LOG2E / math.sqrt(MEM_HEAD_DIM)
    bc_w = SSD_GROUPS * SSD_STATE

    def expand_heads(x):
        tiles = []
        for pair in range(n_pairs):
            even = jnp.broadcast_to(x[:, 2 * pair:2 * pair + 1], (x.shape[0], LANES))
            odd = jnp.broadcast_to(x[:, 2 * pair + 1:2 * pair + 2], (x.shape[0], LANES))
            tiles.append(jnp.where(lo_half, even, odd))
        return jnp.concatenate(tiles, axis=1)

    def build_masked(src, dst, rows):
        full = src[rows, :]
        swapped = pltpu.roll(full, HEAD_DIM, 1)
        zero = jnp.zeros_like(full)
        dst[0, rows, :] = jnp.where(lo_half, full, zero).astype(BF16)
        dst[1, rows, :] = jnp.where(lo_half, zero, swapped).astype(BF16)
        dst[2, rows, :] = jnp.where(lo_half, swapped, zero).astype(BF16)
        dst[3, rows, :] = jnp.where(lo_half, zero, full).astype(BF16)

    def out_piece(mix_prev, k, pp):
        cols = slice(pp * PIECE_W, (pp + 1) * PIECE_W)
        y_ref[k, :, cols] = x_prev_ref[k, :, cols] + _dot(mix_prev[...], w_out_ref[:, cols])

    def step(tk, proj, hb, mix, mix_prev, proj_nxt, hb_nxt):
        t = (2 * pair + tk) % n_tiles
        rk = tk if rows_per_pair == 2 else 0
        mix_prev[...] = mix[...]

        @pl.when(t == 0)
        def _init():
            zero_head = jnp.zeros((HEAD_ROWS, GROUP_W), F32)
            proj[2, 0:HEAD_ROWS, :] = zero_head
            proj[3, 0:HEAD_ROWS, :] = zero_head
            if has_past:
                kwin[0:WINDOW, :] = wink_ref[rk]
                vwin[0:WINDOW, :] = winv_ref[rk]
                state_t[...] = ssm_ref[rk].T
                tail = conv_ref[rk]
                r_lo = HEAD_ROWS - (CONV_WIDTH - 1)
                proj[2, r_lo:HEAD_ROWS, :] = tail[:, 0:GROUP_W]
                proj[3, r_lo:HEAD_ROWS, 0:CONV_CH - GROUP_W] = tail[:, GROUP_W:CONV_CH]
                for src, dst in ((kwin, kr), (vwin, vr)):
                    build_masked(src, dst, slice(0, WINDOW))
            else:
                state_t[...] = jnp.zeros((SSD_STATE, SSD_WIDTH), F32)
                kwin[0:WINDOW, :] = jnp.zeros((WINDOW, KV_WIDTH), F32)
                vwin[0:WINDOW, :] = jnp.zeros((WINDOW, KV_WIDTH), F32)
                for v in range(4):
                    kr[v, 0:WINDOW, :] = jnp.zeros((WINDOW, KV_WIDTH), BF16)
                    vr[v, 0:WINDOW, :] = jnp.zeros((WINDOW, KV_WIDTH), BF16)
            for hh in range(MEM_HEADS):
                sl = slice(hh * MEM_HEAD_DIM, (hh + 1) * MEM_HEAD_DIM)
                memk_bf[:, sl] = memk_ref[rk, :, hh, :].astype(BF16)
                memv_bf[:, sl] = memv_ref[rk, :, hh, :].astype(BF16)

        if n_tiles > 1:
            @pl.when(t > 0)
            def _shift():
                kwin[0:WINDOW, :] = kwin[T:T + WINDOW, :]
                vwin[0:WINDOW, :] = vwin[T:T + WINDOW, :]
                for v in range(4):
                    kr[v, 0:WINDOW, :] = kr[v, T:T + WINDOW, :]
                    vr[v, 0:WINDOW, :] = vr[v, T:T + WINDOW, :]
                proj[2, 0:HEAD_ROWS, :] = proj_nxt[2, T:T + HEAD_ROWS, :]
                proj[3, 0:HEAD_ROWS, :] = proj_nxt[3, T:T + HEAD_ROWS, :]

        a_col = -jnp.exp(alog_col_ref[...]) * LOG2E
        a_row = -jnp.exp(alog_row_ref[...]) * LOG2E
        tri = tri_ref[...]
        d_skip = dskip_ref[...]
        subs = range(n_sub)

        out_per_block = OUT_PIECES // n_blocks

        def block_body(i):
            r0 = i * R
            rows = pl.ds(r0, R)
            prow = pl.ds(r0 + HEAD_ROWS, R)
            wrow = pl.ds(r0 + WINDOW, R)
            sub_rows = [slice(u * CHUNK, (u + 1) * CHUNK) for u in subs]
            krows = [pl.ds(r0 + u * CHUNK, WINDOW + CHUNK) for u in subs]

            pieces = [(i * groups_per_block + gg, pp) for gg in range(groups_per_block) for pp in range(N_PIECES)]
            gap_share = (1, 3, 5, 1)
            per_gap = [len(pieces) * sh // sum(gap_share) for sh in gap_share]

            def issue(n):
                for _ in range(n):
                    g, pp = pieces.pop(0)
                    project_piece(hb_nxt, proj_nxt, g, pp)

            issue(per_gap[0])

            q = proj[0, prow, G0_Q:G0_Q + ATTN_WIDTH]
            k = proj[0, prow, G0_K:G0_K + KV_WIDTH]
            dt_row = _softplus(proj[1, prow, G1_DT:G1_DT + LANES] + dtb_row_ref[...])
            mem_q = []
            for hh in range(MEM_HEADS):
                mqh = proj[3, prow, G3_MQ + hh * MEM_HEAD_DIM:G3_MQ + (hh + 1) * MEM_HEAD_DIM]
                mem_q.append((mqh * _rms_scale(mqh) * g_mq_ref[...]).astype(BF16))

            seg = segq_ref[...]
            q_blocks = [q[:, j * 2 * LANES:(j + 1) * 2 * LANES] for j in range(ATTN_WIDTH // (2 * LANES))]
            q_ss = [_seg_sumsq(qj, seg) for qj in q_blocks]
            k_ss = _seg_sumsq(k, segk_ref[...])
            dtx = expand_heads(dt_row)
            acum_n = _dot_sel_lhs(tri, dt_row * a_row)
            mem_sc = [_dot_nt(mem_q[hh], memk_bf[:, hh * MEM_HEAD_DIM:(hh + 1) * MEM_HEAD_DIM])
                      for hh in range(MEM_HEADS)]
            dt_t = _dot_nt(wdt_ref[...], hb[rows, :])

            issue(per_gap[1])
            yield

            def conv(group, c0, width, ch0):
                acc = convb_ref[:, ch0:ch0 + width]
                blk = proj[group, pl.ds(r0, HEAD_ROWS + R), c0:c0 + width]
                for j in range(CONV_WIDTH):
                    back = CONV_WIDTH - 1 - j
                    tap = blk if back == 0 else pltpu.roll(blk, back, 0)
                    acc = acc + tap[HEAD_ROWS:HEAD_ROWS + R, :] * convw_ref[j:j + 1, ch0:ch0 + width]
                return _silu(acc)

            b_all = conv(2, G2_B, bc_w, SSD_WIDTH).astype(BF16)
            c_all = conv(3, G3_C, bc_w, SSD_WIDTH + bc_w).astype(BF16)
            xs = conv(2, G2_X, SSD_WIDTH, 0)

            cos = cos_ref[tk, rows, :]
            sin_s = sin_ref[tk, rows, :]
            for j, (qj, ss) in enumerate(zip(q_blocks, q_ss)):
                qn = qj * lax.rsqrt(ss * (1.0 / HEAD_DIM) + EPS) * gq_ref[:, j * 2 * LANES:(j + 1) * 2 * LANES]
                for jj in range(2):
                    c0 = j * 2 * LANES + jj * LANES
                    qs[rows, c0:c0 + LANES] = _rope(qn[:, jj * LANES:(jj + 1) * LANES], cos, sin_s,
                                                    first_half).astype(BF16)
            kn = k * lax.rsqrt(k_ss * (1.0 / HEAD_DIM) + EPS) * gk_ref[...]
            kwin[wrow, :] = _rope(kn, cos, sin_s, first_half)
            vwin[wrow, :] = proj[0, prow, G0_V:G0_V + KV_WIDTH]
            build_masked(kwin, kr, wrow)
            build_masked(vwin, vr, wrow)

            mem_p = [_softmax_rows_log2(mem_sc[hh] * mem_scale) for hh in range(MEM_HEADS)]
            xdt = xs * dtx
            da_t3 = _split3(_softplus(dt_t + dtb_col_ref[...]) * a_col)

            bgs = [[b_all[sub_rows[u], g * SSD_STATE:(g + 1) * SSD_STATE] for g in range(SSD_GROUPS)] for u in subs]
            cgs = [[c_all[sub_rows[u], g * SSD_STATE:(g + 1) * SSD_STATE] for g in range(SSD_GROUPS)] for u in subs]
            cbs = [[_dot_nt(cgs[u][g], jnp.concatenate([bgs[u][g], bgs[u][g]], axis=0))
                    for g in range(SSD_GROUPS)] for u in subs]
            cross_cb = [_dot_nt(cgs[1][g], bgs[0][g]) for g in range(SSD_GROUPS)] if n_sub == 2 else None
            y_in = [_dot(c_all[:, g * SSD_STATE:(g + 1) * SSD_STATE], state_t[:, g * gw:(g + 1) * gw].astype(BF16))
                    for g in range(SSD_GROUPS)]
            acum = expand_heads(acum_n)
            dup = duptri_ref[...]
            acum_t = (_dot(da_t3[2], dup) + _dot(da_t3[1], dup)) + _dot(da_t3[0], dup)
            att_sc = []
            for u in subs:
                for g in range(N_KV_HEADS):
                    base = g * 2 * LANES
                    qrows = pl.ds(r0 + u * CHUNK, CHUNK)
                    qg = jnp.concatenate([qs[qrows, base:base + LANES], qs[qrows, base + LANES:base + 2 * LANES]],
                                         axis=0)
                    for pos in range(2):
                        att_sc.append(_dot_nt(qg, kr[2 * g + pos, krows[u], :]))
            mem_o = [_dot(mem_p[hh], memv_bf[:, hh * MEM_HEAD_DIM:(hh + 1) * MEM_HEAD_DIM])
                     for hh in range(MEM_HEADS)]

            issue(per_gap[2])
            for pp in range(i * out_per_block, i * out_per_block + out_per_block // 2):
                out_piece(mix_prev, tk, pp)
            yield

            att_p = []
            for u in subs:
                needs_mask = (not has_past) and (i * R + u * CHUNK < WINDOW)
                if needs_mask:
                    first_valid = jnp.where(t == 0, WINDOW - (r0 + u * CHUNK), 0)
                    key_ok = key_idx >= first_valid
                for g in range(N_KV_HEADS):
                    for pos in range(2):
                        hd0 = 4 * g + pos
                        sink = jnp.where(row128 < CHUNK, sinks_ref[hd0], sinks_ref[hd0 + 2]) * LOG2E
                        sc = att_sc[(u * N_KV_HEADS + g) * 2 + pos]
                        if needs_mask:
                            sc = jnp.where(key_ok, sc, NEG_BIG)
                        att_p.append(_softmax_rows_log2(sc, sink))

            exp_acum = jnp.exp2(acum)
            a_last = [acum[u * CHUNK + CHUNK - 1:(u + 1) * CHUNK, :] for u in subs]
            dec_end = [jnp.exp2(a_last[u]) for u in subs]
            x_end_f = [xdt[sub_rows[u], :] * jnp.exp2(a_last[u] - acum[sub_rows[u], :]) for u in subs]
            x_end = [xe.astype(BF16) for xe in x_end_f]
            if n_sub == 2:
                x_end_stack = jnp.concatenate([(x_end_f[0] * dec_end[1]).astype(BF16), x_end[1]], axis=0)
            ws, xbds = [], []
            zero_bd = jnp.zeros((2 * CHUNK, LANES), BF16)
            for u in subs:
                act = acum_t[:, u * LANES:(u + 1) * LANES]
                for quad in range(n_pairs // 2):
                    w_parts, bd_parts = [], []
                    for pair in (2 * quad, 2 * quad + 1):
                        g = pair // pairs_per_group
                        cols = slice(pair * LANES, (pair + 1) * LANES)
                        rowb = jnp.where(lane < CHUNK, act[2 * pair:2 * pair + 1, :], act[2 * pair + 1:2 * pair + 2, :])
                        dec = jnp.exp2(jnp.where(causal_dup, acum[sub_rows[u], cols] - rowb, -jnp.inf))
                        w_parts.append((cbs[u][g] * dec).astype(BF16))
                        xp = xdt[sub_rows[u], cols]
                        zero = jnp.zeros_like(xp)
                        bd_parts.append(jnp.concatenate(
                            [jnp.where(lo_half64, xp, zero), jnp.where(lo_half64, zero, xp)], axis=0).astype(BF16))
                    ws.append(jnp.concatenate(w_parts, axis=1))
                    xbds.append(jnp.concatenate(
                        [jnp.concatenate([bd_parts[0], zero_bd], axis=1),
                         jnp.concatenate([zero_bd, bd_parts[1]], axis=1)], axis=0))

            for hh in range(MEM_HEADS):
                gm = proj[3, prow, G3_GM + hh * MEM_HEAD_DIM:G3_GM + (hh + 1) * MEM_HEAD_DIM]
                c0 = ATTN_WIDTH + SSD_WIDTH + hh * MEM_HEAD_DIM
                mix[rows, c0:c0 + MEM_HEAD_DIM] = (mem_o[hh] * _silu(gm)).astype(BF16)

            att_o = []
            for u in subs:
                for g in range(N_KV_HEADS):
                    j0 = (u * N_KV_HEADS + g) * 2
                    att_o.append(_dot(att_p[j0], vr[2 * g, krows[u], :]) + _dot(att_p[j0 + 1], vr[2 * g + 1, krows[u], :]))
            for g in range(SSD_GROUPS):
                gcols = slice(g * gw, (g + 1) * gw)
                st = state_t[:, gcols]
                if n_sub == 1:
                    st = st * dec_end[0][:, gcols] + _dot_tn(bgs[0][g], x_end[0][:, gcols])
                else:
                    st = st * (dec_end[0][:, gcols] * dec_end[1][:, gcols]) + _dot_tn(
                        b_all[:, g * SSD_STATE:(g + 1) * SSD_STATE], x_end_stack[:, gcols])
                state_t[:, gcols] = st
            cross = ([_dot(cross_cb[g].astype(BF16), x_end[0][:, g * gw:(g + 1) * gw]) for g in range(SSD_GROUPS)]
                     if n_sub == 2 else None)
            y_diag = [_dot(w, xbd) for w, xbd in zip(ws, xbds)]

            issue(len(pieces))
            for pp in range(i * out_per_block + out_per_block // 2, (i + 1) * out_per_block):
                out_piece(mix_prev, tk, pp)
            yield

            for u in subs:
                for g in range(N_KV_HEADS):
                    base = g * 2 * LANES
                    orow = pl.ds(r0 + u * CHUNK, CHUNK)
                    ga = proj[0, pl.ds(r0 + u * CHUNK + HEAD_ROWS, CHUNK), G0_GA + base:G0_GA + base + 2 * LANES]
                    o = att_o[u * N_KV_HEADS + g]
                    mix[orow, base:base + LANES] = (o[0:CHUNK] * _silu(ga[:, 0:LANES])).astype(BF16)
                    mix[orow, base + LANES:base + 2 * LANES] = (o[CHUNK:] * _silu(ga[:, LANES:])).astype(BF16)

            y_rows = []
            for u in subs:
                ys = []
                for pair in range(n_pairs):
                    g, hp = divmod(pair, pairs_per_group)
                    cols = slice(pair * LANES, (pair + 1) * LANES)
                    lc = slice(hp * LANES, (hp + 1) * LANES)
                    y_off = y_in[g][sub_rows[u], lc]
                    if u == 1:
                        y_off = y_off * dec_end[0][:, cols] + cross[g][:, lc]
                    yd = y_diag[(u * n_pairs + pair) // 2][:, (pair % 2) * LANES:(pair % 2 + 1) * LANES]
                    ys.append(yd + y_off * exp_acum[sub_rows[u], cols] + d_skip[:, cols] * xs[sub_rows[u], cols])
                y_rows.append(jnp.concatenate(ys, axis=1))
            y_all = y_rows[0] if n_sub == 1 else jnp.concatenate(y_rows, axis=0)
            z = proj[1, prow, G1_Z:G1_Z + SSD_WIDTH]
            yg = y_all * _silu(z)
            mix[rows, ATTN_WIDTH:ATTN_WIDTH + SSD_WIDTH] = (yg * _rms_scale(yg) * g_ssd_ref[...]).astype(BF16)

            xa = x_ahead_ref[tk, rows, :]
            hb[rows, :] = (xa * _rms_scale(xa) * g_norm_ref[...]).astype(BF16)

        n_segments = 4
        gens = [block_body(blk) for blk in range(n_blocks)]
        for slot in range(2 * (n_blocks - 1) + n_segments):
            for blk in reversed(range(n_blocks)):
                if 0 <= slot - 2 * blk < n_segments:
                    next(gens[blk], None)

        @pl.when(t == n_tiles - 1)
        def _final():
            newk_ref[rk] = kwin[T:T + WINDOW, :]
            newv_ref[rk] = vwin[T:T + WINDOW, :]
            ssm_out_ref[rk] = state_t[...].T
            r_lo = T + HEAD_ROWS - (CONV_WIDTH - 1)
            conv_out_ref[rk, :, 0:GROUP_W] = proj[2, r_lo:T + HEAD_ROWS, :]
            conv_out_ref[rk, :, GROUP_W:CONV_CH] = proj[3, r_lo:T + HEAD_ROWS, 0:CONV_CH - GROUP_W]

    @pl.when(pair < total_pairs)
    def _pair():
        step(0, proj_a, hb_a, mix_a, mixo_a, proj_b, hb_b)
        step(1, proj_b, hb_b, mix_b, mixo_b, proj_a, hb_a)

    @pl.when(pair == total_pairs)
    def _flush():
        for k, mix_k in enumerate((mix_a, mix_b)):
            for pp in range(OUT_PIECES):
                out_piece(mix_k, k, pp)


def _constants(tile, n_sub):
    n_chunks = tile // CHUNK
    rows_per_block = n_sub * CHUNK
    idx = np.arange
    segq = (idx(2 * LANES)[:, None] // HEAD_DIM == idx(2 * LANES)[None, :] // HEAD_DIM)
    segk = (idx(KV_WIDTH)[:, None] // HEAD_DIM == idx(KV_WIDTH)[None, :] // HEAD_DIM)
    r = idx(rows_per_block)
    tri = (r[:, None] // CHUNK == r[None, :] // CHUNK) & (r[:, None] >= r[None, :])
    src = idx(rows_per_block)[:, None]
    dst = idx(n_sub * LANES)[None, :]
    duptri = (src // CHUNK == dst // LANES) & (src % CHUNK <= dst % CHUNK)
    return [jnp.asarray(m.astype(np.float32), dtype=BF16) for m in (segq, segk, tri, duptri)]


def _layer(x, positions, mem_k, mem_v, past, params, *, tile):
    b, seq, _ = x.shape
    n_t = seq // tile
    n_steps = b * n_t
    has_past = past is not None
    n_chunks = tile // CHUNK
    n_sub = 2 if n_chunks % 2 == 0 else 1
    assert seq % tile == 0 and tile % CHUNK == 0 and N_GROUPS % (n_chunks // n_sub) == 0
    assert n_t == 1 or tile >= WINDOW

    half = HEAD_DIM // 2
    freqs = ROPE_THETA ** (-jnp.arange(half, dtype=F32) / half)
    ang = positions.astype(F32)[:, None] * freqs[None, :]
    cos, sin = jnp.cos(ang), jnp.sin(ang)
    cos_t = jnp.tile(cos, (1, LANES // half)).reshape(n_t, tile, LANES)
    sin_t = jnp.tile(jnp.concatenate([-sin, sin], axis=1), (1, LANES // HEAD_DIM)).reshape(n_t, tile, LANES)
    if n_t == 1:
        cos_t, sin_t = jnp.tile(cos_t, (2, 1, 1)), jnp.tile(sin_t, (2, 1, 1))

    consts = _constants(tile, n_sub)

    assert n_steps % 2 == 0 and (n_t == 1 or n_t % 2 == 0)
    n_pairs = n_steps // 2
    rows_per_pair = 2 if n_t == 1 else 1
    x_tiles = x.reshape(n_steps, tile, D_MODEL)

    def full(a):
        nd = a.ndim
        return pl.BlockSpec(a.shape, lambda i, _n=nd: (0,) * _n, pipeline_mode=pl.Buffered(1))

    def pair_at(j):
        return (jnp.clip(j, 0, n_pairs - 1), 0, 0)

    def per_row(a):
        nd = a.ndim
        return pl.BlockSpec((rows_per_pair,) + a.shape[1:],
                            lambda i, _n=nd: ((2 * jnp.minimum(i, n_pairs - 1)) // n_t // rows_per_pair,)
                            + (0,) * (_n - 1))

    vec_params = [params[k] for k in ("g_norm", "gq", "gk")]
    tail_params = [params[k] for k in ("conv_w", "conv_b", "dtb_row", "dtb_col", "alog_row", "alog_col",
                                       "d_skip", "g_ssd", "g_mq")]
    past_inputs = list(past) if has_past else []
    inputs = [x_tiles, x_tiles, x_tiles, cos_t, sin_t, mem_k, mem_v, *past_inputs,
              params["w_in"], params["w_dt"], params["w_out"],
              *vec_params, params["sinks"], *tail_params, *consts]
    cos_pairs = cos_t.shape[0] // 2
    in_specs = [
        pl.BlockSpec((2, tile, D_MODEL), lambda i: pair_at(0), pipeline_mode=pl.Buffered(1)),
        pl.BlockSpec((2, tile, D_MODEL), lambda i: pair_at(i + 1)),
        pl.BlockSpec((2, tile, D_MODEL), lambda i: pair_at(i - 1)),
        pl.BlockSpec((2, tile, LANES), lambda i: (i % cos_pairs, 0, 0)),
        pl.BlockSpec((2, tile, LANES), lambda i: (i % cos_pairs, 0, 0)),
        per_row(mem_k), per_row(mem_v),
        *[per_row(p) for p in past_inputs],
        full(params["w_in"]), full(params["w_dt"]), full(params["w_out"]),
        *[full(p) for p in vec_params],
        pl.BlockSpec(memory_space=pltpu.SMEM),
        *[full(p) for p in tail_params],
        *[full(c) for c in consts],
    ]
    out_shape = [
        jax.ShapeDtypeStruct((n_steps, tile, D_MODEL), F32),
        jax.ShapeDtypeStruct((b, WINDOW, KV_WIDTH), F32),
        jax.ShapeDtypeStruct((b, WINDOW, KV_WIDTH), F32),
        jax.ShapeDtypeStruct((b, SSD_WIDTH, SSD_STATE), F32),
        jax.ShapeDtypeStruct((b, CONV_WIDTH - 1, CONV_CH), F32),
    ]
    out_specs = [
        pl.BlockSpec((2, tile, D_MODEL), lambda i: pair_at(i - 1)),
        *[per_row(o) for o in out_shape[1:]],
    ]
    proj_shape = (N_GROUPS, HEAD_ROWS + tile, GROUP_W)
    scratch = [
        pltpu.VMEM(proj_shape, F32), pltpu.VMEM(proj_shape, F32),
        pltpu.VMEM((tile, D_MODEL), BF16), pltpu.VMEM((tile, D_MODEL), BF16),
        pltpu.VMEM((tile, MIX_WIDTH), BF16), pltpu.VMEM((tile, MIX_WIDTH), BF16),
        pltpu.VMEM((tile, MIX_WIDTH), BF16), pltpu.VMEM((tile, MIX_WIDTH), BF16),
        pltpu.VMEM((tile, ATTN_WIDTH), BF16),
        pltpu.VMEM((WINDOW + tile, KV_WIDTH), F32),
        pltpu.VMEM((WINDOW + tile, KV_WIDTH), F32),
        pltpu.VMEM((4, WINDOW + tile, KV_WIDTH), BF16),
        pltpu.VMEM((4, WINDOW + tile, KV_WIDTH), BF16),
        pltpu.VMEM((SSD_STATE, SSD_WIDTH), F32),
        pltpu.VMEM((MEM_TOKENS, MEM_WIDTH), BF16),
        pltpu.VMEM((MEM_TOKENS, MEM_WIDTH), BF16),
    ]
    outs = pl.pallas_call(
        functools.partial(_layer_kernel, tile=tile, n_tiles=n_t, total_steps=n_steps, n_sub=n_sub,
                          has_past=has_past),
        grid=(n_pairs + 1,),
        in_specs=in_specs,
        out_specs=out_specs,
        out_shape=out_shape,
        scratch_shapes=scratch,
        compiler_params=pltpu.CompilerParams(
            dimension_semantics=("arbitrary",),
            vmem_limit_bytes=VMEM_LIMIT_BYTES),
        name="layer_sample" if has_past else "layer_prompt",
    )(*inputs)
    return (outs[0].reshape(b, seq, D_MODEL), *outs[1:])


def _prep_params(g_norm, w_in, g_q, g_k, sinks, conv_w, conv_b, dt_bias, a_log, d_skip, g_ssd, g_mq, w_out):
    o_z = ATTN_WIDTH + 2 * KV_WIDTH + ATTN_WIDTH
    o_x = o_z + SSD_WIDTH
    o_b = o_x + SSD_WIDTH
    o_c = o_b + SSD_GROUPS * SSD_STATE
    o_dt = o_c + SSD_GROUPS * SSD_STATE
    o_mq = o_dt + SSD_HEADS
    w_dt = w_in[:, o_dt:o_mq]
    zeros = lambda n: jnp.zeros((D_MODEL, n), w_in.dtype)
    groups = [
        w_in[:, 0:o_z],
        jnp.concatenate([w_in[:, o_z:o_x], w_dt, zeros(GROUP_W - SSD_WIDTH - SSD_HEADS)], axis=1),
        w_in[:, o_x:o_c],
        jnp.concatenate([w_in[:, o_c:o_dt], w_in[:, o_mq:]], axis=1),
    ]
    pad = LANES - SSD_HEADS
    return {
        "w_in": jnp.stack(groups).astype(BF16),
        "w_dt": w_dt.T.astype(BF16),
        "w_out": w_out.astype(BF16),
        "g_norm": g_norm.reshape(1, D_MODEL),
        "gq": (jnp.tile(g_q, N_Q_HEADS) * (LOG2E / math.sqrt(HEAD_DIM))).reshape(1, ATTN_WIDTH),
        "gk": jnp.tile(g_k, N_KV_HEADS).reshape(1, KV_WIDTH),
        "sinks": sinks,
        "conv_w": conv_w,
        "conv_b": conv_b.reshape(1, CONV_CH),
        "dtb_row": jnp.pad(dt_bias, (0, pad)).reshape(1, LANES),
        "dtb_col": dt_bias.reshape(SSD_HEADS, 1),
        "alog_row": jnp.pad(a_log, (0, pad)).reshape(1, LANES),
        "alog_col": a_log.reshape(SSD_HEADS, 1),
        "d_skip": jnp.repeat(d_skip, SSD_HEAD_DIM).reshape(1, SSD_WIDTH),
        "g_ssd": g_ssd.reshape(1, SSD_WIDTH),
        "g_mq": g_mq.reshape(1, MEM_HEAD_DIM),
    }


PROMPT_TILE = 256


def kernel(x_prompt, x_sample, cache_win_k, cache_win_v, state_ssm, state_conv, cache_mem_k, cache_mem_v,
           mem_prompt, g_norm, w_in, g_q, g_k, sinks, conv_w, conv_b, dt_bias, a_log, d_skip, g_ssd,
           g_mem, w_mem_k, w_mem_v, g_mk, g_mq, w_out):
    depth = w_in.shape[0]
    assert depth == 1
    l = 0
    bp, t_p, _ = x_prompt.shape
    bs, t_s, _ = x_sample.shape
    params = _prep_params(g_norm[l], w_in[l], g_q[l], g_k[l], sinks[l], conv_w[l], conv_b[l], dt_bias[l],
                          a_log[l], d_skip[l], g_ssd[l], g_mq[l], w_out[l])

    mk_p, mv_p = _memory_kv(mem_prompt, g_mem[l], w_mem_k[l], w_mem_v[l], g_mk[l])

    yp, kp, vp, sp, cp = _layer(
        x_prompt, jnp.arange(t_p, dtype=jnp.int32), mk_p, mv_p, None, params, tile=min(PROMPT_TILE, t_p))

    past = (cache_win_k[l].reshape(bs, WINDOW, KV_WIDTH), cache_win_v[l].reshape(bs, WINDOW, KV_WIDTH),
            state_ssm[l].reshape(bs, SSD_WIDTH, SSD_STATE), state_conv[l])
    ys, ks, vs, ss, cs = _layer(
        x_sample, PAST_LEN + jnp.arange(t_s, dtype=jnp.int32),
        cache_mem_k[l], cache_mem_v[l],
        past, params, tile=t_s)

    kv5 = lambda a, b: a.reshape(1, b, WINDOW, N_KV_HEADS, HEAD_DIM)
    ssm5 = lambda a, b: a.reshape(1, b, SSD_HEADS, SSD_HEAD_DIM, SSD_STATE)
    mem5 = lambda a: a[None]
    return (yp, ys,
            kv5(kp, bp), kv5(vp, bp), ssm5(sp, bp), cp[None],
            mem5(mk_p), mem5(mv_p),
            kv5(ks, bs), kv5(vs, bs), ssm5(ss, bs), cs[None])
```

```python
import functools
import math

import numpy as np
import jax
import jax.numpy as jnp
from jax import lax
from jax.experimental import pallas as pl
from jax.experimental.pallas import tpu as pltpu

D_MODEL = 1024
CHUNK = 64
HEAD_DIM = 64
N_Q_HEADS = 8
N_KV_HEADS = 2
ATTN_WIDTH = N_Q_HEADS * HEAD_DIM
KV_WIDTH = N_KV_HEADS * HEAD_DIM
WINDOW = 128
ROPE_THETA = 10000.0
SSD_WIDTH = D_MODEL
SSD_HEAD_DIM = 64
SSD_HEADS = SSD_WIDTH // SSD_HEAD_DIM
SSD_GROUPS = 2
SSD_STATE = 128
CONV_WIDTH = 4
CONV_CH = SSD_WIDTH + 2 * SSD_GROUPS * SSD_STATE
MEM_TOKENS = 256
MEM_HEADS = 4
MEM_HEAD_DIM = 128
MEM_WIDTH = MEM_HEADS * MEM_HEAD_DIM
MIX_WIDTH = 2 * D_MODEL
EPS = 1e-6
PAST_LEN = 2048

LANES = 128
SUBLANES = 8
VMEM_LIMIT_BYTES = 56 * 1024 * 1024

N_GROUPS = 4
GROUP_W = 1280
G0_Q, G0_K, G0_V, G0_GA = 0, ATTN_WIDTH, ATTN_WIDTH + KV_WIDTH, ATTN_WIDTH + 2 * KV_WIDTH
G1_Z, G1_DT = 0, SSD_WIDTH
G2_X, G2_B = 0, SSD_WIDTH
G3_C, G3_MQ, G3_GM = 0, SSD_GROUPS * SSD_STATE, SSD_GROUPS * SSD_STATE + MEM_WIDTH
PIECE_W = 256
N_PIECES = GROUP_W // PIECE_W
OUT_PIECES = D_MODEL // PIECE_W
HEAD_ROWS = SUBLANES

F32 = jnp.float32
BF16 = jnp.bfloat16
NEG_BIG = -1e30
LOG2E = math.log2(math.e)


def _dot(a, b):
    return jnp.dot(a, b, preferred_element_type=F32)


def _dot_nt(a, b):
    return lax.dot_general(a, b, (((1,), (1,)), ((), ())), preferred_element_type=F32)


def _dot_tn(a, b):
    return lax.dot_general(a, b, (((0,), (0,)), ((), ())), preferred_element_type=F32)


def _split3(x):
    h1 = x.astype(BF16)
    r1 = x - h1.astype(F32)
    h2 = r1.astype(BF16)
    r2 = r1 - h2.astype(F32)
    return h1, h2, r2.astype(BF16)


def _dot_sel_rhs(x, sel):
    a, b, c = _split3(x)
    return (_dot(c, sel) + _dot(b, sel)) + _dot(a, sel)


def _dot_sel_lhs(sel, x):
    a, b, c = _split3(x)
    return (_dot(sel, c) + _dot(sel, b)) + _dot(sel, a)


def _seg_sumsq(x, seg):
    return _dot((x * x).astype(BF16), seg)


def _silu(x):
    h = 0.5 * x
    return h + h * jnp.tanh(h)


def _softplus(x):
    return jnp.maximum(x, 0.0) + jnp.log(1.0 + jnp.exp(-jnp.abs(x)))


def _rms_scale(x):
    return lax.rsqrt(jnp.mean(x * x, axis=-1, keepdims=True) + EPS)


def _softmax_rows_log2(sc, extra=None):
    m = jnp.max(sc, axis=-1, keepdims=True)
    if extra is not None:
        m = jnp.maximum(m, extra)
    e = jnp.exp2(sc - m)
    den = jnp.sum(e, axis=-1, keepdims=True)
    if extra is not None:
        den = den + jnp.exp2(extra - m)
    return (e * (1.0 / den)).astype(BF16)


def _rope(x, cos, sin_signed, first_half):
    partner = jnp.where(first_half, pltpu.roll(x, LANES - HEAD_DIM // 2, 1), pltpu.roll(x, HEAD_DIM // 2, 1))
    return x * cos + partner * sin_signed


def _memory_kv_kernel(mem_ref, g_mem_ref, wk_ref, wv_ref, g_mk_ref, k_out, v_out):
    x = mem_ref[0]
    hb = (x * _rms_scale(x) * g_mem_ref[...]).astype(BF16)
    k = _dot(hb, wk_ref[...])
    v = _dot(hb, wv_ref[...])
    for hh in range(MEM_HEADS):
        sl = slice(hh * MEM_HEAD_DIM, (hh + 1) * MEM_HEAD_DIM)
        kh = k[:, sl]
        k_out[0, :, hh, :] = kh * _rms_scale(kh) * g_mk_ref[...]
        v_out[0, :, hh, :] = v[:, sl]


def _memory_kv(mem, g_mem, w_k, w_v, g_mk):
    b = mem.shape[0]
    full = lambda shape: pl.BlockSpec(shape, lambda i: (0,) * len(shape))
    return pl.pallas_call(
        _memory_kv_kernel,
        grid=(b,),
        in_specs=[
            pl.BlockSpec((1, MEM_TOKENS, D_MODEL), lambda i: (i, 0, 0)),
            full((1, D_MODEL)),
            full((D_MODEL, MEM_WIDTH)),
            full((D_MODEL, MEM_WIDTH)),
            full((1, MEM_HEAD_DIM)),
        ],
        out_specs=[
            pl.BlockSpec((1, MEM_TOKENS, MEM_HEADS, MEM_HEAD_DIM), lambda i: (i, 0, 0, 0)),
            pl.BlockSpec((1, MEM_TOKENS, MEM_HEADS, MEM_HEAD_DIM), lambda i: (i, 0, 0, 0)),
        ],
        out_shape=[jax.ShapeDtypeStruct((b, MEM_TOKENS, MEM_HEADS, MEM_HEAD_DIM), F32)] * 2,
        compiler_params=pltpu.CompilerParams(dimension_semantics=("arbitrary",)),
        name="memory_kv",
    )(mem, g_mem.reshape(1, D_MODEL), w_k.astype(BF16), w_v.astype(BF16), g_mk.reshape(1, MEM_HEAD_DIM))


def _layer_kernel(*refs, tile, n_tiles, total_steps, n_sub, has_past):
    x_first_ref, x_second_ref, x_ahead_ref, x_prev_ref, cos_ref, sin_ref, memk_ref, memv_ref, *rest = refs
    if has_past:
        wink_ref, winv_ref, ssm_ref, conv_ref, *rest = rest
    else:
        wink_ref = winv_ref = ssm_ref = conv_ref = None
    (w_in_ref, wdt_ref, w_out_ref,
     g_norm_ref, gq_ref, gk_ref, sinks_ref, convw_ref, convb_ref,
     dtb_row_ref, dtb_col_ref, alog_row_ref, alog_col_ref, dskip_ref, g_ssd_ref, g_mq_ref,
     segq_ref, segk_ref, tri_ref, duptri_ref,
     y_ref, newk_ref, newv_ref, ssm_out_ref, conv_out_ref,
     proj_a, proj_b, hb_a, hb_b, mix_a, mix_b,
     qs, kwin, vwin, kr, vr, state_t, memk_bf, memv_bf) = rest

    T = tile
    n_chunks = T // CHUNK
    n_blocks = n_chunks // n_sub
    R = n_sub * CHUNK
    groups_per_block = N_GROUPS // n_blocks
    s = pl.program_id(0)
    t = s % n_tiles
    body_rows = pl.ds(HEAD_ROWS, T)

    def normed_input(x_ref, hb):
        x = x_ref[0]
        hb[...] = (x * _rms_scale(x) * g_norm_ref[...]).astype(BF16)

    def project_piece(hb, proj, g, i):
        cols = slice(i * PIECE_W, (i + 1) * PIECE_W)
        proj[g, body_rows, cols] = _dot(hb[...], w_in_ref[g, :, cols])

    @pl.when(s == 0)
    def _prologue():
        normed_input(x_first_ref, hb_a)
        normed_input(x_second_ref, hb_b)
        mix_b[...] = jnp.zeros((T, MIX_WIDTH), BF16)
        for g in range(N_GROUPS):
            for i in range(N_PIECES):
                project_piece(hb_a, proj_a, g, i)

    lane = lax.broadcasted_iota(jnp.int32, (1, LANES), 1)
    first_half = (lane % HEAD_DIM) < (HEAD_DIM // 2)
    lo_half = lane < HEAD_DIM
    row64 = lax.broadcasted_iota(jnp.int32, (CHUNK, LANES), 0)
    lane64 = lax.broadcasted_iota(jnp.int32, (CHUNK, LANES), 1)
    causal_dup = row64 >= (lane64 % CHUNK)
    lo_half64 = lane64 < CHUNK
    row128 = lax.broadcasted_iota(jnp.int32, (2 * CHUNK, 1), 0)
    key_idx = lax.broadcasted_iota(jnp.int32, (1, WINDOW + CHUNK), 1)
    gw = SSD_WIDTH // SSD_GROUPS
    n_pairs = SSD_HEADS // 2
    pairs_per_group = n_pairs // SSD_GROUPS
    mem_scale = LOG2E / math.sqrt(MEM_HEAD_DIM)
    bc_w = SSD_GROUPS * SSD_STATE

    def expand_heads(x):
        tiles = []
        for pair in range(n_pairs):
            even = jnp.broadcast_to(x[:, 2 * pair:2 * pair + 1], (x.shape[0], LANES))
            odd = jnp.broadcast_to(x[:, 2 * pair + 1:2 * pair + 2], (x.shape[0], LANES))
            tiles.append(jnp.where(lo_half, even, odd))
        return jnp.concatenate(tiles, axis=1)

    def build_masked(src, dst, rows):
        full = src[rows, :]
        swapped = pltpu.roll(full, HEAD_DIM, 1)
        zero = jnp.zeros_like(full)
        dst[0, rows, :] = jnp.where(lo_half, full, zero).astype(BF16)
        dst[1, rows, :] = jnp.where(lo_half, zero, swapped).astype(BF16)
        dst[2, rows, :] = jnp.where(lo_half, swapped, zero).astype(BF16)
        dst[3, rows, :] = jnp.where(lo_half, zero, full).astype(BF16)

    def out_piece(mix_prev, pp):
        cols = slice(pp * PIECE_W, (pp + 1) * PIECE_W)
        y_ref[0, :, cols] = x_prev_ref[0, :, cols] + _dot(mix_prev[...], w_out_ref[:, cols])

    def step(proj, hb, mix, proj_nxt, hb_nxt, mix_prev):
        @pl.when(t == 0)
        def _init():
            zero_head = jnp.zeros((HEAD_ROWS, GROUP_W), F32)
            proj[2, 0:HEAD_ROWS, :] = zero_head
            proj[3, 0:HEAD_ROWS, :] = zero_head
            if has_past:
                kwin[0:WINDOW, :] = wink_ref[0]
                vwin[0:WINDOW, :] = winv_ref[0]
                state_t[...] = ssm_ref[0].T
                tail = conv_ref[0]
                r_lo = HEAD_ROWS - (CONV_WIDTH - 1)
                proj[2, r_lo:HEAD_ROWS, :] = tail[:, 0:GROUP_W]
                proj[3, r_lo:HEAD_ROWS, 0:CONV_CH - GROUP_W] = tail[:, GROUP_W:CONV_CH]
                for src, dst in ((kwin, kr), (vwin, vr)):
                    build_masked(src, dst, slice(0, WINDOW))
            else:
                state_t[...] = jnp.zeros((SSD_STATE, SSD_WIDTH), F32)
                kwin[0:WINDOW, :] = jnp.zeros((WINDOW, KV_WIDTH), F32)
                vwin[0:WINDOW, :] = jnp.zeros((WINDOW, KV_WIDTH), F32)
                for v in range(4):
                    kr[v, 0:WINDOW, :] = jnp.zeros((WINDOW, KV_WIDTH), BF16)
                    vr[v, 0:WINDOW, :] = jnp.zeros((WINDOW, KV_WIDTH), BF16)
            for hh in range(MEM_HEADS):
                sl = slice(hh * MEM_HEAD_DIM, (hh + 1) * MEM_HEAD_DIM)
                head_rows = pl.ds(hh, MEM_TOKENS, stride=MEM_HEADS)
                memk_bf[:, sl] = memk_ref[0, head_rows, :].astype(BF16)
                memv_bf[:, sl] = memv_ref[0, head_rows, :].astype(BF16)

        if n_tiles > 1:
            @pl.when(t > 0)
            def _shift():
                kwin[0:WINDOW, :] = kwin[T:T + WINDOW, :]
                vwin[0:WINDOW, :] = vwin[T:T + WINDOW, :]
                for v in range(4):
                    kr[v, 0:WINDOW, :] = kr[v, T:T + WINDOW, :]
                    vr[v, 0:WINDOW, :] = vr[v, T:T + WINDOW, :]
                proj[2, 0:HEAD_ROWS, :] = proj_nxt[2, T:T + HEAD_ROWS, :]
                proj[3, 0:HEAD_ROWS, :] = proj_nxt[3, T:T + HEAD_ROWS, :]

        a_col = -jnp.exp(alog_col_ref[...]) * LOG2E
        a_row = -jnp.exp(alog_row_ref[...]) * LOG2E
        tri = tri_ref[...]
        d_skip = dskip_ref[...]
        subs = range(n_sub)

        out_per_block = OUT_PIECES // n_blocks

        def block_body(i):
            r0 = i * R
            rows = pl.ds(r0, R)
            prow = pl.ds(r0 + HEAD_ROWS, R)
            wrow = pl.ds(r0 + WINDOW, R)
            sub_rows = [slice(u * CHUNK, (u + 1) * CHUNK) for u in subs]
            krows = [pl.ds(r0 + u * CHUNK, WINDOW + CHUNK) for u in subs]

            pieces = [(i * groups_per_block + gg, pp) for gg in range(groups_per_block) for pp in range(N_PIECES)]
            gap_share = (1, 3, 5, 1)
            per_gap = [len(pieces) * sh // sum(gap_share) for sh in gap_share]

            def issue(n):
                for _ in range(n):
                    g, pp = pieces.pop(0)
                    project_piece(hb_nxt, proj_nxt, g, pp)

            issue(per_gap[0])

            q = proj[0, prow, G0_Q:G0_Q + ATTN_WIDTH]
            k = proj[0, prow, G0_K:G0_K + KV_WIDTH]
            dt_row = _softplus(proj[1, prow, G1_DT:G1_DT + LANES] + dtb_row_ref[...])
            mem_q = []
            for hh in range(MEM_HEADS):
                mqh = proj[3, prow, G3_MQ + hh * MEM_HEAD_DIM:G3_MQ + (hh + 1) * MEM_HEAD_DIM]
                mem_q.append((mqh * _rms_scale(mqh) * g_mq_ref[...]).astype(BF16))

            seg = segq_ref[...]
            q_blocks = [q[:, j * 2 * LANES:(j + 1) * 2 * LANES] for j in range(ATTN_WIDTH // (2 * LANES))]
            q_ss = [_seg_sumsq(qj, seg) for qj in q_blocks]
            k_ss = _seg_sumsq(k, segk_ref[...])
            dtx = expand_heads(dt_row)
            acum_n = _dot_sel_lhs(tri, dt_row * a_row)
            mem_sc = [_dot_nt(mem_q[hh], memk_bf[:, hh * MEM_HEAD_DIM:(hh + 1) * MEM_HEAD_DIM])
                      for hh in range(MEM_HEADS)]
            dt_t = _dot_nt(wdt_ref[...], hb[rows, :])

            issue(per_gap[1])
            yield

            def conv(group, c0, width, ch0):
                acc = convb_ref[:, ch0:ch0 + width]
                blk = proj[group, pl.ds(r0, HEAD_ROWS + R), c0:c0 + width]
                for j in range(CONV_WIDTH):
                    back = CONV_WIDTH - 1 - j
                    tap = blk if back == 0 else pltpu.roll(blk, back, 0)
                    acc = acc + tap[HEAD_ROWS:HEAD_ROWS + R, :] * convw_ref[j:j + 1, ch0:ch0 + width]
                return _silu(acc)

            b_all = conv(2, G2_B, bc_w, SSD_WIDTH).astype(BF16)
            c_all = conv(3, G3_C, bc_w, SSD_WIDTH + bc_w).astype(BF16)
            xs = conv(2, G2_X, SSD_WIDTH, 0)

            cos = cos_ref[rows, :]
            sin_s = sin_ref[rows, :]
            for j, (qj, ss) in enumerate(zip(q_blocks, q_ss)):
                qn = qj * lax.rsqrt(ss * (1.0 / HEAD_DIM) + EPS) * gq_ref[:, j * 2 * LANES:(j + 1) * 2 * LANES]
                for jj in range(2):
                    c0 = j * 2 * LANES + jj * LANES
                    qs[rows, c0:c0 + LANES] = _rope(qn[:, jj * LANES:(jj + 1) * LANES], cos, sin_s,
                                                    first_half).astype(BF16)
            kn = k * lax.rsqrt(k_ss * (1.0 / HEAD_DIM) + EPS) * gk_ref[...]
            kwin[wrow, :] = _rope(kn, cos, sin_s, first_half)
            vwin[wrow, :] = proj[0, prow, G0_V:G0_V + KV_WIDTH]
            build_masked(kwin, kr, wrow)
            build_masked(vwin, vr, wrow)

            mem_p = [_softmax_rows_log2(mem_sc[hh] * mem_scale) for hh in range(MEM_HEADS)]
            xdt = xs * dtx
            da_t3 = _split3(_softplus(dt_t + dtb_col_ref[...]) * a_col)

            bgs = [[b_all[sub_rows[u], g * SSD_STATE:(g + 1) * SSD_STATE] for g in range(SSD_GROUPS)] for u in subs]
            cgs = [[c_all[sub_rows[u], g * SSD_STATE:(g + 1) * SSD_STATE] for g in range(SSD_GROUPS)] for u in subs]
            cbs = [[_dot_nt(cgs[u][g], jnp.concatenate([bgs[u][g], bgs[u][g]], axis=0))
                    for g in range(SSD_GROUPS)] for u in subs]
            cross_cb = [_dot_nt(cgs[1][g], bgs[0][g]) for g in range(SSD_GROUPS)] if n_sub == 2 else None
            y_in = [_dot(c_all[:, g * SSD_STATE:(g + 1) * SSD_STATE], state_t[:, g * gw:(g + 1) * gw].astype(BF16))
                    for g in range(SSD_GROUPS)]
            acum = expand_heads(acum_n)
            dup = duptri_ref[...]
            acum_t = (_dot(da_t3[2], dup) + _dot(da_t3[1], dup)) + _dot(da_t3[0], dup)
            att_sc = []
            for u in subs:
                for g in range(N_KV_HEADS):
                    base = g * 2 * LANES
                    qrows = pl.ds(r0 + u * CHUNK, CHUNK)
                    qg = jnp.concatenate([qs[qrows, base:base + LANES], qs[qrows, base + LANES:base + 2 * LANES]],
                                         axis=0)
                    for pos in range(2):
                        att_sc.append(_dot_nt(qg, kr[2 * g + pos, krows[u], :]))
            mem_o = [_dot(mem_p[hh], memv_bf[:, hh * MEM_HEAD_DIM:(hh + 1) * MEM_HEAD_DIM])
                     for hh in range(MEM_HEADS)]

            issue(per_gap[2])
            for pp in range(i * out_per_block, i * out_per_block + out_per_block // 2):
                out_piece(mix_prev, pp)
            yield

            att_p = []
            for u in subs:
                needs_mask = (not has_past) and (i * R + u * CHUNK < WINDOW)
                if needs_mask:
                    first_valid = jnp.where(t == 0, WINDOW - (r0 + u * CHUNK), 0)
                    key_ok = key_idx >= first_valid
                for g in range(N_KV_HEADS):
                    for pos in range(2):
                        hd0 = 4 * g + pos
                        sink = jnp.where(row128 < CHUNK, sinks_ref[hd0], sinks_ref[hd0 + 2]) * LOG2E
                        sc = att_sc[(u * N_KV_HEADS + g) * 2 + pos]
                        if needs_mask:
                            sc = jnp.where(key_ok, sc, NEG_BIG)
                        att_p.append(_softmax_rows_log2(sc, sink))

            exp_acum = jnp.exp2(acum)
            a_last = [acum[u * CHUNK + CHUNK - 1:(u + 1) * CHUNK, :] for u in subs]
            dec_end = [jnp.exp2(a_last[u]) for u in subs]
            x_end_f = [xdt[sub_rows[u], :] * jnp.exp2(a_last[u] - acum[sub_rows[u], :]) for u in subs]
            x_end = [xe.astype(BF16) for xe in x_end_f]
            if n_sub == 2:
                x_end_stack = jnp.concatenate([(x_end_f[0] * dec_end[1]).astype(BF16), x_end[1]], axis=0)
            ws, xbds = [], []
            zero_bd = jnp.zeros((2 * CHUNK, LANES), BF16)
            for u in subs:
                act = acum_t[:, u * LANES:(u + 1) * LANES]
                for quad in range(n_pairs // 2):
                    w_parts, bd_parts = [], []
                    for pair in (2 * quad, 2 * quad + 1):
                        g = pair // pairs_per_group
                        cols = slice(pair * LANES, (pair + 1) * LANES)
                        rowb = jnp.where(lane < CHUNK, act[2 * pair:2 * pair + 1, :], act[2 * pair + 1:2 * pair + 2, :])
                        dec = jnp.exp2(jnp.where(causal_dup, acum[sub_rows[u], cols] - rowb, -jnp.inf))
                        w_parts.append((cbs[u][g] * dec).astype(BF16))
                        xp = xdt[sub_rows[u], cols]
                        zero = jnp.zeros_like(xp)
                        bd_parts.append(jnp.concatenate(
                            [jnp.where(lo_half64, xp, zero), jnp.where(lo_half64, zero, xp)], axis=0).astype(BF16))
                    ws.append(jnp.concatenate(w_parts, axis=1))
                    xbds.append(jnp.concatenate(
                        [jnp.concatenate([bd_parts[0], zero_bd], axis=1),
                         jnp.concatenate([zero_bd, bd_parts[1]], axis=1)], axis=0))

            for hh in range(MEM_HEADS):
                gm = proj[3, prow, G3_GM + hh * MEM_HEAD_DIM:G3_GM + (hh + 1) * MEM_HEAD_DIM]
                c0 = ATTN_WIDTH + SSD_WIDTH + hh * MEM_HEAD_DIM
                mix[rows, c0:c0 + MEM_HEAD_DIM] = (mem_o[hh] * _silu(gm)).astype(BF16)

            att_o = []
            for u in subs:
                for g in range(N_KV_HEADS):
                    j0 = (u * N_KV_HEADS + g) * 2
                    att_o.append(_dot(att_p[j0], vr[2 * g, krows[u], :]) + _dot(att_p[j0 + 1], vr[2 * g + 1, krows[u], :]))
            for g in range(SSD_GROUPS):
                gcols = slice(g * gw, (g + 1) * gw)
                st = state_t[:, gcols]
                if n_sub == 1:
                    st = st * dec_end[0][:, gcols] + _dot_tn(bgs[0][g], x_end[0][:, gcols])
                else:
                    st = st * (dec_end[0][:, gcols] * dec_end[1][:, gcols]) + _dot_tn(
                        b_all[:, g * SSD_STATE:(g + 1) * SSD_STATE], x_end_stack[:, gcols])
                state_t[:, gcols] = st
            cross = ([_dot(cross_cb[g].astype(BF16), x_end[0][:, g * gw:(g + 1) * gw]) for g in range(SSD_GROUPS)]
                     if n_sub == 2 else None)
            y_diag = [_dot(w, xbd) for w, xbd in zip(ws, xbds)]

            issue(len(pieces))
            for pp in range(i * out_per_block + out_per_block // 2, (i + 1) * out_per_block):
                out_piece(mix_prev, pp)
            yield

            for u in subs:
                for g in range(N_KV_HEADS):
                    base = g * 2 * LANES
                    orow = pl.ds(r0 + u * CHUNK, CHUNK)
                    ga = proj[0, pl.ds(r0 + u * CHUNK + HEAD_ROWS, CHUNK), G0_GA + base:G0_GA + base + 2 * LANES]
                    o = att_o[u * N_KV_HEADS + g]
                    mix[orow, base:base + LANES] = (o[0:CHUNK] * _silu(ga[:, 0:LANES])).astype(BF16)
                    mix[orow, base + LANES:base + 2 * LANES] = (o[CHUNK:] * _silu(ga[:, LANES:])).astype(BF16)

            y_rows = []
            for u in subs:
                ys = []
                for pair in range(n_pairs):
                    g, hp = divmod(pair, pairs_per_group)
                    cols = slice(pair * LANES, (pair + 1) * LANES)
                    lc = slice(hp * LANES, (hp + 1) * LANES)
                    y_off = y_in[g][sub_rows[u], lc]
                    if u == 1:
                        y_off = y_off * dec_end[0][:, cols] + cross[g][:, lc]
                    yd = y_diag[(u * n_pairs + pair) // 2][:, (pair % 2) * LANES:(pair % 2 + 1) * LANES]
                    ys.append(yd + y_off * exp_acum[sub_rows[u], cols] + d_skip[:, cols] * xs[sub_rows[u], cols])
                y_rows.append(jnp.concatenate(ys, axis=1))
            y_all = y_rows[0] if n_sub == 1 else jnp.concatenate(y_rows, axis=0)
            z = proj[1, prow, G1_Z:G1_Z + SSD_WIDTH]
            yg = y_all * _silu(z)
            mix[rows, ATTN_WIDTH:ATTN_WIDTH + SSD_WIDTH] = (yg * _rms_scale(yg) * g_ssd_ref[...]).astype(BF16)

            xa = x_ahead_ref[0, rows, :]
            hb[rows, :] = (xa * _rms_scale(xa) * g_norm_ref[...]).astype(BF16)

        n_segments = 4
        gens = [block_body(blk) for blk in range(n_blocks)]
        for slot in range(2 * (n_blocks - 1) + n_segments):
            for blk in reversed(range(n_blocks)):
                if 0 <= slot - 2 * blk < n_segments:
                    next(gens[blk], None)

        @pl.when(t == n_tiles - 1)
        def _final():
            newk_ref[0] = kwin[T:T + WINDOW, :]
            newv_ref[0] = vwin[T:T + WINDOW, :]
            ssm_out_ref[0] = state_t[...].T
            r_lo = T + HEAD_ROWS - (CONV_WIDTH - 1)
            conv_out_ref[0, :, 0:GROUP_W] = proj[2, r_lo:T + HEAD_ROWS, :]
            conv_out_ref[0, :, GROUP_W:CONV_CH] = proj[3, r_lo:T + HEAD_ROWS, 0:CONV_CH - GROUP_W]

    parity = s % 2

    @pl.when((parity == 0) & (s < total_steps))
    def _even():
        step(proj_a, hb_a, mix_a, proj_b, hb_b, mix_b)

    @pl.when((parity == 1) & (s < total_steps))
    def _odd():
        step(proj_b, hb_b, mix_b, proj_a, hb_a, mix_a)

    @pl.when(s == total_steps)
    def _flush():
        for pp in range(OUT_PIECES):
            out_piece(mix_b if total_steps % 2 == 0 else mix_a, pp)


def _constants(tile, n_sub):
    n_chunks = tile // CHUNK
    rows_per_block = n_sub * CHUNK
    idx = np.arange
    segq = (idx(2 * LANES)[:, None] // HEAD_DIM == idx(2 * LANES)[None, :] // HEAD_DIM)
    segk = (idx(KV_WIDTH)[:, None] // HEAD_DIM == idx(KV_WIDTH)[None, :] // HEAD_DIM)
    r = idx(rows_per_block)
    tri = (r[:, None] // CHUNK == r[None, :] // CHUNK) & (r[:, None] >= r[None, :])
    src = idx(rows_per_block)[:, None]
    dst = idx(n_sub * LANES)[None, :]
    duptri = (src // CHUNK == dst // LANES) & (src % CHUNK <= dst % CHUNK)
    return [jnp.asarray(m.astype(np.float32), dtype=BF16) for m in (segq, segk, tri, duptri)]


def _layer(x, positions, mem_k, mem_v, past, params, *, tile):
    b, seq, _ = x.shape
    mem_k = mem_k.reshape(b, MEM_TOKENS * MEM_HEADS, MEM_HEAD_DIM)
    mem_v = mem_v.reshape(b, MEM_TOKENS * MEM_HEADS, MEM_HEAD_DIM)
    n_t = seq // tile
    n_steps = b * n_t
    has_past = past is not None
    n_chunks = tile // CHUNK
    n_sub = 2 if n_chunks % 2 == 0 else 1
    assert seq % tile == 0 and tile % CHUNK == 0 and N_GROUPS % (n_chunks // n_sub) == 0
    assert n_t == 1 or tile >= WINDOW

    half = HEAD_DIM // 2
    freqs = ROPE_THETA ** (-jnp.arange(half, dtype=F32) / half)
    ang = positions.astype(F32)[:, None] * freqs[None, :]
    cos, sin = jnp.cos(ang), jnp.sin(ang)
    cos_t = jnp.tile(cos, (1, LANES // half))
    sin_t = jnp.tile(jnp.concatenate([-sin, sin], axis=1), (1, LANES // HEAD_DIM))

    consts = _constants(tile, n_sub)

    def full(a):
        nd = a.ndim
        return pl.BlockSpec(a.shape, lambda i, _n=nd: (0,) * _n, pipeline_mode=pl.Buffered(1))

    def row_of(i):
        return jnp.minimum(i, n_steps - 1) // n_t

    def per_row(a):
        nd = a.ndim
        return pl.BlockSpec((1,) + a.shape[1:], lambda i, _n=nd: (row_of(i),) + (0,) * (_n - 1))

    def tile_at(j):
        j = jnp.clip(j, 0, n_steps - 1)
        return (j // n_t, j % n_t, 0)

    vec_params = [params[k] for k in ("g_norm", "gq", "gk")]
    tail_params = [params[k] for k in ("conv_w", "conv_b", "dtb_row", "dtb_col", "alog_row", "alog_col",
                                       "d_skip", "g_ssd", "g_mq")]
    past_inputs = list(past) if has_past else []
    inputs = [x, x, x, x, cos_t, sin_t, mem_k, mem_v, *past_inputs,
              params["w_in"], params["w_dt"], params["w_out"],
              *vec_params, params["sinks"], *tail_params, *consts]
    in_specs = [
        pl.BlockSpec((1, tile, D_MODEL), lambda i: tile_at(0)),
        pl.BlockSpec((1, tile, D_MODEL), lambda i: tile_at(1)),
        pl.BlockSpec((1, tile, D_MODEL), lambda i: tile_at(i + 2)),
        pl.BlockSpec((1, tile, D_MODEL), lambda i: tile_at(i - 1)),
        pl.BlockSpec((tile, LANES), lambda i: (i % n_t, 0)),
        pl.BlockSpec((tile, LANES), lambda i: (i % n_t, 0)),
        per_row(mem_k), per_row(mem_v),
        *[per_row(p) for p in past_inputs],
        full(params["w_in"]), full(params["w_dt"]), full(params["w_out"]),
        *[full(p) for p in vec_params],
        pl.BlockSpec(memory_space=pltpu.SMEM),
        *[full(p) for p in tail_params],
        *[full(c) for c in consts],
    ]
    out_shape = [
        jax.ShapeDtypeStruct((b, seq, D_MODEL), F32),
        jax.ShapeDtypeStruct((b, WINDOW, KV_WIDTH), F32),
        jax.ShapeDtypeStruct((b, WINDOW, KV_WIDTH), F32),
        jax.ShapeDtypeStruct((b, SSD_WIDTH, SSD_STATE), F32),
        jax.ShapeDtypeStruct((b, CONV_WIDTH - 1, CONV_CH), F32),
    ]
    out_specs = [
        pl.BlockSpec((1, tile, D_MODEL), lambda i: tile_at(i - 1)),
        pl.BlockSpec((1, WINDOW, KV_WIDTH), lambda i: (row_of(i), 0, 0)),
        pl.BlockSpec((1, WINDOW, KV_WIDTH), lambda i: (row_of(i), 0, 0)),
        pl.BlockSpec((1, SSD_WIDTH, SSD_STATE), lambda i: (row_of(i), 0, 0)),
        pl.BlockSpec((1, CONV_WIDTH - 1, CONV_CH), lambda i: (row_of(i), 0, 0)),
    ]
    proj_shape = (N_GROUPS, HEAD_ROWS + tile, GROUP_W)
    scratch = [
        pltpu.VMEM(proj_shape, F32), pltpu.VMEM(proj_shape, F32),
        pltpu.VMEM((tile, D_MODEL), BF16), pltpu.VMEM((tile, D_MODEL), BF16),
        pltpu.VMEM((tile, MIX_WIDTH), BF16), pltpu.VMEM((tile, MIX_WIDTH), BF16),
        pltpu.VMEM((tile, ATTN_WIDTH), BF16),
        pltpu.VMEM((WINDOW + tile, KV_WIDTH), F32),
        pltpu.VMEM((WINDOW + tile, KV_WIDTH), F32),
        pltpu.VMEM((4, WINDOW + tile, KV_WIDTH), BF16),
        pltpu.VMEM((4, WINDOW + tile, KV_WIDTH), BF16),
        pltpu.VMEM((SSD_STATE, SSD_WIDTH), F32),
        pltpu.VMEM((MEM_TOKENS, MEM_WIDTH), BF16),
        pltpu.VMEM((MEM_TOKENS, MEM_WIDTH), BF16),
    ]
    return pl.pallas_call(
        functools.partial(_layer_kernel, tile=tile, n_tiles=n_t, total_steps=n_steps, n_sub=n_sub,
                          has_past=has_past),
        grid=(n_steps + 1,),
        in_specs=in_specs,
        out_specs=out_specs,
        out_shape=out_shape,
        scratch_shapes=scratch,
        compiler_params=pltpu.CompilerParams(
            dimension_semantics=("arbitrary",),
            vmem_limit_bytes=VMEM_LIMIT_BYTES),
        name="layer_sample" if has_past else "layer_prompt",
    )(*inputs)


def _prep_params(g_norm, w_in, g_q, g_k, sinks, conv_w, conv_b, dt_bias, a_log, d_skip, g_ssd, g_mq, w_out):
    o_z = ATTN_WIDTH + 2 * KV_WIDTH + ATTN_WIDTH
    o_x = o_z + SSD_WIDTH
    o_b = o_x + SSD_WIDTH
    o_c = o_b + SSD_GROUPS * SSD_STATE
    o_dt = o_c + SSD_GROUPS * SSD_STATE
    o_mq = o_dt + SSD_HEADS
    w_in = w_in.astype(BF16)
    w_dt = w_in[:, o_dt:o_mq]
    zeros = lambda n: jnp.zeros((D_MODEL, n), w_in.dtype)
    groups = [
        w_in[:, 0:o_z],
        jnp.concatenate([w_in[:, o_z:o_x], w_dt, zeros(GROUP_W - SSD_WIDTH - SSD_HEADS)], axis=1),
        w_in[:, o_x:o_c],
        jnp.concatenate([w_in[:, o_c:o_dt], w_in[:, o_mq:]], axis=1),
    ]
    pad = LANES - SSD_HEADS
    return {
        "w_in": jnp.stack(groups),
        "w_dt": w_dt.T,
        "w_out": w_out.astype(BF16),
        "g_norm": g_norm.reshape(1, D_MODEL),
        "gq": (jnp.tile(g_q, N_Q_HEADS) * (LOG2E / math.sqrt(HEAD_DIM))).reshape(1, ATTN_WIDTH),
        "gk": jnp.tile(g_k, N_KV_HEADS).reshape(1, KV_WIDTH),
        "sinks": sinks,
        "conv_w": conv_w,
        "conv_b": conv_b.reshape(1, CONV_CH),
        "dtb_row": jnp.pad(dt_bias, (0, pad)).reshape(1, LANES),
        "dtb_col": dt_bias.reshape(SSD_HEADS, 1),
        "alog_row": jnp.pad(a_log, (0, pad)).reshape(1, LANES),
        "alog_col": a_log.reshape(SSD_HEADS, 1),
        "d_skip": jnp.repeat(d_skip, SSD_HEAD_DIM).reshape(1, SSD_WIDTH),
        "g_ssd": g_ssd.reshape(1, SSD_WIDTH),
        "g_mq": g_mq.reshape(1, MEM_HEAD_DIM),
    }


PROMPT_TILE = 256


def kernel(x_prompt, x_sample, cache_win_k, cache_win_v, state_ssm, state_conv, cache_mem_k, cache_mem_v,
           mem_prompt, g_norm, w_in, g_q, g_k, sinks, conv_w, conv_b, dt_bias, a_log, d_skip, g_ssd,
           g_mem, w_mem_k, w_mem_v, g_mk, g_mq, w_out):
    depth = w_in.shape[0]
    assert depth == 1
    l = 0
    bp, t_p, _ = x_prompt.shape
    bs, t_s, _ = x_sample.shape
    params = _prep_params(g_norm[l], w_in[l], g_q[l], g_k[l], sinks[l], conv_w[l], conv_b[l], dt_bias[l],
                          a_log[l], d_skip[l], g_ssd[l], g_mq[l], w_out[l])

    mk_p, mv_p = _memory_kv(mem_prompt, g_mem[l], w_mem_k[l], w_mem_v[l], g_mk[l])

    yp, kp, vp, sp, cp = _layer(
        x_prompt, jnp.arange(t_p, dtype=jnp.int32), mk_p, mv_p, None, params, tile=min(PROMPT_TILE, t_p))

    past = (cache_win_k[l].reshape(bs, WINDOW, KV_WIDTH), cache_win_v[l].reshape(bs, WINDOW, KV_WIDTH),
            state_ssm[l].reshape(bs, SSD_WIDTH, SSD_STATE), state_conv[l])
    ys, ks, vs, ss, cs = _layer(
        x_sample, PAST_LEN + jnp.arange(t_s, dtype=jnp.int32),
        cache_mem_k[l], cache_mem_v[l],
        past, params, tile=t_s)

    kv5 = lambda a, b: a.reshape(1, b, WINDOW, N_KV_HEADS, HEAD_DIM)
    ssm5 = lambda a, b: a.reshape(1, b, SSD_HEADS, SSD_HEAD_DIM, SSD_STATE)
    mem5 = lambda a: a[None]
    return (yp, ys,
            kv5(kp, bp), kv5(vp, bp), ssm5(sp, bp), cp[None],
            mem5(mk_p), mem5(mv_p),
            kv5(ks, bs), kv5(vs, bs), ssm5(ss, bs), cs[None])
```

```python
import functools
import math

import numpy as np
import jax
import jax.numpy as jnp
from jax import lax
from jax.experimental import pallas as pl
from jax.experimental.pallas import tpu as pltpu

D_MODEL = 1024
CHUNK = 64
HEAD_DIM = 64
N_Q_HEADS = 8
N_KV_HEADS = 2
ATTN_WIDTH = N_Q_HEADS * HEAD_DIM
KV_WIDTH = N_KV_HEADS * HEAD_DIM
WINDOW = 128
ROPE_THETA = 10000.0
SSD_WIDTH = D_MODEL
SSD_HEAD_DIM = 64
SSD_HEADS = SSD_WIDTH // SSD_HEAD_DIM
SSD_GROUPS = 2
SSD_STATE = 128
CONV_WIDTH = 4
CONV_CH = SSD_WIDTH + 2 * SSD_GROUPS * SSD_STATE
MEM_TOKENS = 256
MEM_HEADS = 4
MEM_HEAD_DIM = 128
MEM_WIDTH = MEM_HEADS * MEM_HEAD_DIM
MIX_WIDTH = 2 * D_MODEL
EPS = 1e-6
PAST_LEN = 2048

LANES = 128
SUBLANES = 8
VMEM_LIMIT_BYTES = 56 * 1024 * 1024

N_GROUPS = 4
GROUP_W = 1280
G0_Q, G0_K, G0_V, G0_GA = 0, ATTN_WIDTH, ATTN_WIDTH + KV_WIDTH, ATTN_WIDTH + 2 * KV_WIDTH
G1_Z, G1_DT = 0, SSD_WIDTH
G2_X, G2_B = 0, SSD_WIDTH
G3_C, G3_MQ, G3_GM = 0, SSD_GROUPS * SSD_STATE, SSD_GROUPS * SSD_STATE + MEM_WIDTH
PIECE_W = 256
N_PIECES = GROUP_W // PIECE_W
OUT_PIECES = D_MODEL // PIECE_W
HEAD_ROWS = SUBLANES

F32 = jnp.float32
BF16 = jnp.bfloat16
NEG_BIG = -1e30
LOG2E = math.log2(math.e)


def _dot(a, b):
    return jnp.dot(a, b, preferred_element_type=F32)


def _dot_nt(a, b):
    return lax.dot_general(a, b, (((1,), (1,)), ((), ())), preferred_element_type=F32)


def _dot_tn(a, b):
    return lax.dot_general(a, b, (((0,), (0,)), ((), ())), preferred_element_type=F32)


def _split3(x):
    h1 = x.astype(BF16)
    r1 = x - h1.astype(F32)
    h2 = r1.astype(BF16)
    r2 = r1 - h2.astype(F32)
    return h1, h2, r2.astype(BF16)


def _dot_sel_rhs(x, sel):
    a, b, c = _split3(x)
    return (_dot(c, sel) + _dot(b, sel)) + _dot(a, sel)


def _dot_sel_lhs(sel3, x):
    a, b, c = _split3(x)
    return _dot(sel3, jnp.concatenate([c, b, a], axis=0))


def _seg_sumsq(x, seg):
    return _dot((x * x).astype(BF16), seg)


def _silu(x):
    h = 0.5 * x
    return h + h * jnp.tanh(h)


def _softplus(x):
    return jnp.maximum(x, 0.0) + jnp.log(1.0 + jnp.exp(-jnp.abs(x)))


def _rms_scale(x):
    return lax.rsqrt(jnp.mean(x * x, axis=-1, keepdims=True) + EPS)


def _softmax_rows_log2(sc, extra=None):
    m = jnp.max(sc, axis=-1, keepdims=True)
    if extra is not None:
        m = jnp.maximum(m, extra)
    e = jnp.exp2(sc - m)
    den = jnp.sum(e, axis=-1, keepdims=True)
    if extra is not None:
        den = den + jnp.exp2(extra - m)
    return (e * (1.0 / den)).astype(BF16)


def _rope(x, cos, sin_signed, first_half):
    partner = jnp.where(first_half, pltpu.roll(x, LANES - HEAD_DIM // 2, 1), pltpu.roll(x, HEAD_DIM // 2, 1))
    return x * cos + partner * sin_signed


def _memory_kv_kernel(mem_ref, g_mem_ref, wk_ref, wv_ref, g_mk_ref, k_out, v_out):
    x = mem_ref[0]
    hb = (x * _rms_scale(x) * g_mem_ref[...]).astype(BF16)
    k = _dot(hb, wk_ref[...])
    v = _dot(hb, wv_ref[...])
    for hh in range(MEM_HEADS):
        sl = slice(hh * MEM_HEAD_DIM, (hh + 1) * MEM_HEAD_DIM)
        kh = k[:, sl]
        k_out[0, :, hh, :] = kh * _rms_scale(kh) * g_mk_ref[...]
        v_out[0, :, hh, :] = v[:, sl]


def _memory_kv(mem, g_mem, w_k, w_v, g_mk):
    b = mem.shape[0]
    full = lambda shape: pl.BlockSpec(shape, lambda i: (0,) * len(shape))
    return pl.pallas_call(
        _memory_kv_kernel,
        grid=(b,),
        in_specs=[
            pl.BlockSpec((1, MEM_TOKENS, D_MODEL), lambda i: (i, 0, 0)),
            full((1, D_MODEL)),
            full((D_MODEL, MEM_WIDTH)),
            full((D_MODEL, MEM_WIDTH)),
            full((1, MEM_HEAD_DIM)),
        ],
        out_specs=[
            pl.BlockSpec((1, MEM_TOKENS, MEM_HEADS, MEM_HEAD_DIM), lambda i: (i, 0, 0, 0)),
            pl.BlockSpec((1, MEM_TOKENS, MEM_HEADS, MEM_HEAD_DIM), lambda i: (i, 0, 0, 0)),
        ],
        out_shape=[jax.ShapeDtypeStruct((b, MEM_TOKENS, MEM_HEADS, MEM_HEAD_DIM), F32)] * 2,
        compiler_params=pltpu.CompilerParams(dimension_semantics=("arbitrary",)),
        name="memory_kv",
    )(mem, g_mem.reshape(1, D_MODEL), w_k.astype(BF16), w_v.astype(BF16), g_mk.reshape(1, MEM_HEAD_DIM))


def _layer_kernel(*refs, tile, n_tiles, total_steps, n_sub, has_past):
    x_first_ref, x_second_ref, x_ahead_ref, x_prev_ref, cos_ref, sin_ref, memk_ref, memv_ref, *rest = refs
    if has_past:
        wink_ref, winv_ref, ssm_ref, conv_ref, *rest = rest
    else:
        wink_ref = winv_ref = ssm_ref = conv_ref = None
    (w_in_ref, wdt_ref, w_out_ref,
     g_norm_ref, gq_ref, gk_ref, sinks_ref, convw_ref, convb_ref,
     dtb_row_ref, dtb_col_ref, alog_row_ref, alog_col_ref, dskip_ref, g_ssd_ref, g_mq_ref,
     segq_ref, segk_ref, tri_ref, duptri_ref,
     y_ref, newk_ref, newv_ref, ssm_out_ref, conv_out_ref,
     proj_a, proj_b, hb_a, hb_b, mix_a, mix_b,
     qs, kwin, vwin, kr, vr, state_t, memk_bf, memv_bf) = rest

    T = tile
    n_chunks = T // CHUNK
    n_blocks = n_chunks // n_sub
    R = n_sub * CHUNK
    groups_per_block = N_GROUPS // n_blocks
    s = pl.program_id(0)
    t = s % n_tiles
    body_rows = pl.ds(HEAD_ROWS, T)

    def normed_input(x_ref, hb):
        x = x_ref[0]
        hb[...] = (x * _rms_scale(x) * g_norm_ref[...]).astype(BF16)

    def project_piece(hb, proj, g, i):
        cols = slice(i * PIECE_W, (i + 1) * PIECE_W)
        proj[g, body_rows, cols] = _dot(hb[...], w_in_ref[g, :, cols])

    @pl.when(s == 0)
    def _prologue():
        normed_input(x_first_ref, hb_a)
        normed_input(x_second_ref, hb_b)
        mix_b[...] = jnp.zeros((T, MIX_WIDTH), BF16)
        if n_tiles > 1:
            kwin[...] = jnp.zeros((WINDOW + T, KV_WIDTH), F32)
            vwin[...] = jnp.zeros((WINDOW + T, KV_WIDTH), F32)
            kr[...] = jnp.zeros((4, WINDOW + T, KV_WIDTH), BF16)
            vr[...] = jnp.zeros((4, WINDOW + T, KV_WIDTH), BF16)
            proj_b[2, T:T + HEAD_ROWS, :] = jnp.zeros((HEAD_ROWS, GROUP_W), F32)
            proj_b[3, T:T + HEAD_ROWS, :] = jnp.zeros((HEAD_ROWS, GROUP_W), F32)
        for g in range(N_GROUPS):
            for i in range(N_PIECES):
                project_piece(hb_a, proj_a, g, i)

    lane = lax.broadcasted_iota(jnp.int32, (1, LANES), 1)
    first_half = (lane % HEAD_DIM) < (HEAD_DIM // 2)
    lo_half = lane < HEAD_DIM
    row64 = lax.broadcasted_iota(jnp.int32, (CHUNK, LANES), 0)
    lane64 = lax.broadcasted_iota(jnp.int32, (CHUNK, LANES), 1)
    causal_dup = row64 >= (lane64 % CHUNK)
    lo_half64 = lane64 < CHUNK
    row128 = lax.broadcasted_iota(jnp.int32, (2 * CHUNK, 1), 0)
    key_idx = lax.broadcasted_iota(jnp.int32, (1, WINDOW + CHUNK), 1)
    gw = SSD_WIDTH // SSD_GROUPS
    n_pairs = SSD_HEADS // 2
    pairs_per_group = n_pairs // SSD_GROUPS
    mem_scale = LOG2E / math.sqrt(MEM_HEAD_DIM)
    bc_w = SSD_GROUPS * SSD_STATE

    def expand_heads(x):
        tiles = []
        for pair in range(n_pairs):
            even = jnp.broadcast_to(x[:, 2 * pair:2 * pair + 1], (x.shape[0], LANES))
            odd = jnp.broadcast_to(x[:, 2 * pair + 1:2 * pair + 2], (x.shape[0], LANES))
            tiles.append(jnp.where(lo_half, even, odd))
        return jnp.concatenate(tiles, axis=1)

    def build_masked(src, dst, rows):
        full = src[rows, :]
        swapped = pltpu.roll(full, HEAD_DIM, 1)
        zero = jnp.zeros_like(full)
        dst[0, rows, :] = jnp.where(lo_half, full, zero).astype(BF16)
        dst[1, rows, :] = jnp.where(lo_half, zero, swapped).astype(BF16)
        dst[2, rows, :] = jnp.where(lo_half, swapped, zero).astype(BF16)
        dst[3, rows, :] = jnp.where(lo_half, zero, full).astype(BF16)

    def out_piece(mix_prev, pp):
        cols = slice(pp * PIECE_W, (pp + 1) * PIECE_W)
        y_ref[0, :, cols] = x_prev_ref[0, :, cols] + _dot(mix_prev[...], w_out_ref[:, cols])

    def step(proj, hb, mix, proj_nxt, hb_nxt, mix_prev):
        if n_tiles > 1:
            kwin[0:WINDOW, :] = kwin[T:T + WINDOW, :]
            vwin[0:WINDOW, :] = vwin[T:T + WINDOW, :]
            for v in range(4):
                kr[v, 0:WINDOW, :] = kr[v, T:T + WINDOW, :]
                vr[v, 0:WINDOW, :] = vr[v, T:T + WINDOW, :]
            proj[2, 0:HEAD_ROWS, :] = proj_nxt[2, T:T + HEAD_ROWS, :]
            proj[3, 0:HEAD_ROWS, :] = proj_nxt[3, T:T + HEAD_ROWS, :]

        @pl.when(t == 0)
        def _init():
            zero_head = jnp.zeros((HEAD_ROWS, GROUP_W), F32)
            proj[2, 0:HEAD_ROWS, :] = zero_head
            proj[3, 0:HEAD_ROWS, :] = zero_head
            if has_past:
                kwin[0:WINDOW, :] = wink_ref[0]
                vwin[0:WINDOW, :] = winv_ref[0]
                state_t[...] = ssm_ref[0].T
                tail = conv_ref[0]
                r_lo = HEAD_ROWS - (CONV_WIDTH - 1)
                proj[2, r_lo:HEAD_ROWS, :] = tail[:, 0:GROUP_W]
                proj[3, r_lo:HEAD_ROWS, 0:CONV_CH - GROUP_W] = tail[:, GROUP_W:CONV_CH]
                for src, dst in ((kwin, kr), (vwin, vr)):
                    build_masked(src, dst, slice(0, WINDOW))
            else:
                state_t[...] = jnp.zeros((SSD_STATE, SSD_WIDTH), F32)
                kwin[0:WINDOW, :] = jnp.zeros((WINDOW, KV_WIDTH), F32)
                vwin[0:WINDOW, :] = jnp.zeros((WINDOW, KV_WIDTH), F32)
                for v in range(4):
                    kr[v, 0:WINDOW, :] = jnp.zeros((WINDOW, KV_WIDTH), BF16)
                    vr[v, 0:WINDOW, :] = jnp.zeros((WINDOW, KV_WIDTH), BF16)
            for hh in range(MEM_HEADS):
                sl = slice(hh * MEM_HEAD_DIM, (hh + 1) * MEM_HEAD_DIM)
                head_rows = pl.ds(hh, MEM_TOKENS, stride=MEM_HEADS)
                memk_bf[:, sl] = memk_ref[0, head_rows, :].astype(BF16)
                memv_bf[:, sl] = memv_ref[0, head_rows, :].astype(BF16)

        a_col = -jnp.exp(alog_col_ref[...]) * LOG2E
        a_row = -jnp.exp(alog_row_ref[...]) * LOG2E
        tri = tri_ref[...]
        d_skip = dskip_ref[...]
        subs = range(n_sub)

        out_per_block = OUT_PIECES // n_blocks

        def block_body(i):
            r0 = i * R
            rows = pl.ds(r0, R)
            prow = pl.ds(r0 + HEAD_ROWS, R)
            wrow = pl.ds(r0 + WINDOW, R)
            sub_rows = [slice(u * CHUNK, (u + 1) * CHUNK) for u in subs]
            krows = [pl.ds(r0 + u * CHUNK, WINDOW + CHUNK) for u in subs]

            pieces = [(i * groups_per_block + gg, pp) for gg in range(groups_per_block) for pp in range(N_PIECES)]
            gap_share = (1, 3, 5, 1)
            per_gap = [len(pieces) * sh // sum(gap_share) for sh in gap_share]

            def issue(n):
                for _ in range(n):
                    g, pp = pieces.pop(0)
                    project_piece(hb_nxt, proj_nxt, g, pp)

            issue(per_gap[0])

            q = proj[0, prow, G0_Q:G0_Q + ATTN_WIDTH]
            k = proj[0, prow, G0_K:G0_K + KV_WIDTH]
            dt_row = _softplus(proj[1, prow, G1_DT:G1_DT + LANES] + dtb_row_ref[...])
            mem_q = []
            for hh in range(MEM_HEADS):
                mqh = proj[3, prow, G3_MQ + hh * MEM_HEAD_DIM:G3_MQ + (hh + 1) * MEM_HEAD_DIM]
                mem_q.append((mqh * _rms_scale(mqh) * g_mq_ref[...]).astype(BF16))

            seg = segq_ref[...]
            q_blocks = [q[:, j * 2 * LANES:(j + 1) * 2 * LANES] for j in range(ATTN_WIDTH // (2 * LANES))]
            q_ss = [_seg_sumsq(qj, seg) for qj in q_blocks]
            k_ss = _seg_sumsq(k, segk_ref[...])
            dtx = expand_heads(dt_row)
            acum_n = _dot_sel_lhs(tri, dt_row * a_row)
            mem_sc = [_dot_nt(mem_q[hh], memk_bf[:, hh * MEM_HEAD_DIM:(hh + 1) * MEM_HEAD_DIM])
                      for hh in range(MEM_HEADS)]
            dt_t = _dot_nt(wdt_ref[...], hb[rows, :])

            issue(per_gap[1])
            yield

            def conv(group, c0, width, ch0):
                acc = convb_ref[:, ch0:ch0 + width]
                blk = proj[group, pl.ds(r0, HEAD_ROWS + R), c0:c0 + width]
                for j in range(CONV_WIDTH):
                    back = CONV_WIDTH - 1 - j
                    tap = blk if back == 0 else pltpu.roll(blk, back, 0)
                    acc = acc + tap[HEAD_ROWS:HEAD_ROWS + R, :] * convw_ref[j:j + 1, ch0:ch0 + width]
                return _silu(acc)

            b_all = conv(2, G2_B, bc_w, SSD_WIDTH).astype(BF16)
            c_all = conv(3, G3_C, bc_w, SSD_WIDTH + bc_w).astype(BF16)
            xs = conv(2, G2_X, SSD_WIDTH, 0)

            cos = cos_ref[rows, :]
            sin_s = sin_ref[rows, :]
            for j, (qj, ss) in enumerate(zip(q_blocks, q_ss)):
                qn = qj * lax.rsqrt(ss * (1.0 / HEAD_DIM) + EPS) * gq_ref[:, j * 2 * LANES:(j + 1) * 2 * LANES]
                for jj in range(2):
                    c0 = j * 2 * LANES + jj * LANES
                    qs[rows, c0:c0 + LANES] = _rope(qn[:, jj * LANES:(jj + 1) * LANES], cos, sin_s,
                                                    first_half).astype(BF16)
            kn = k * lax.rsqrt(k_ss * (1.0 / HEAD_DIM) + EPS) * gk_ref[...]
            kwin[wrow, :] = _rope(kn, cos, sin_s, first_half)
            vwin[wrow, :] = proj[0, prow, G0_V:G0_V + KV_WIDTH]
            build_masked(kwin, kr, wrow)
            build_masked(vwin, vr, wrow)

            mem_p = [_softmax_rows_log2(mem_sc[hh] * mem_scale) for hh in range(MEM_HEADS)]
            xdt = xs * dtx
            da_t3 = _split3(_softplus(dt_t + dtb_col_ref[...]) * a_col)

            bgs = [[b_all[sub_rows[u], g * SSD_STATE:(g + 1) * SSD_STATE] for g in range(SSD_GROUPS)] for u in subs]
            cgs = [[c_all[sub_rows[u], g * SSD_STATE:(g + 1) * SSD_STATE] for g in range(SSD_GROUPS)] for u in subs]
            cbs = [[_dot_nt(cgs[u][g], jnp.concatenate([bgs[u][g], bgs[u][g]], axis=0))
                    for g in range(SSD_GROUPS)] for u in subs]
            cross_cb = [_dot_nt(cgs[1][g], bgs[0][g]) for g in range(SSD_GROUPS)] if n_sub == 2 else None
            y_in = [_dot(c_all[:, g * SSD_STATE:(g + 1) * SSD_STATE], state_t[:, g * gw:(g + 1) * gw].astype(BF16))
                    for g in range(SSD_GROUPS)]
            acum = expand_heads(acum_n)
            dup = duptri_ref[...]
            acum_t = (_dot(da_t3[2], dup) + _dot(da_t3[1], dup)) + _dot(da_t3[0], dup)
            att_sc = []
            for u in subs:
                for g in range(N_KV_HEADS):
                    base = g * 2 * LANES
                    qrows = pl.ds(r0 + u * CHUNK, CHUNK)
                    qg = jnp.concatenate([qs[qrows, base:base + LANES], qs[qrows, base + LANES:base + 2 * LANES]],
                                         axis=0)
                    for pos in range(2):
                        att_sc.append(_dot_nt(qg, kr[2 * g + pos, krows[u], :]))
            mem_o = [_dot(mem_p[hh], memv_bf[:, hh * MEM_HEAD_DIM:(hh + 1) * MEM_HEAD_DIM])
                     for hh in range(MEM_HEADS)]

            issue(per_gap[2])
            for pp in range(i * out_per_block, i * out_per_block + out_per_block // 2):
                out_piece(mix_prev, pp)
            yield

            att_p = []
            for u in subs:
                needs_mask = (not has_past) and (i * R + u * CHUNK < WINDOW)
                if needs_mask:
                    first_valid = jnp.where(t == 0, WINDOW - (r0 + u * CHUNK), 0)
                    key_ok = key_idx >= first_valid
                for g in range(N_KV_HEADS):
                    for pos in range(2):
                        hd0 = 4 * g + pos
                        sink = jnp.where(row128 < CHUNK, sinks_ref[hd0], sinks_ref[hd0 + 2]) * LOG2E
                        sc = att_sc[(u * N_KV_HEADS + g) * 2 + pos]
                        if needs_mask:
                            sc = jnp.where(key_ok, sc, NEG_BIG)
                        att_p.append(_softmax_rows_log2(sc, sink))

            exp_acum = jnp.exp2(acum)
            a_last = [acum[u * CHUNK + CHUNK - 1:(u + 1) * CHUNK, :] for u in subs]
            dec_end = [jnp.exp2(a_last[u]) for u in subs]
            x_end_f = [xdt[sub_rows[u], :] * jnp.exp2(a_last[u] - acum[sub_rows[u], :]) for u in subs]
            x_end = [xe.astype(BF16) for xe in x_end_f]
            if n_sub == 2:
                x_end_stack = jnp.concatenate([(x_end_f[0] * dec_end[1]).astype(BF16), x_end[1]], axis=0)
            ws, xbds = [], []
            zero_bd = jnp.zeros((2 * CHUNK, LANES), BF16)
            for u in subs:
                act = acum_t[:, u * LANES:(u + 1) * LANES]
                for quad in range(n_pairs // 2):
                    w_parts, bd_parts = [], []
                    for pair in (2 * quad, 2 * quad + 1):
                        g = pair // pairs_per_group
                        cols = slice(pair * LANES, (pair + 1) * LANES)
                        rowb = jnp.where(lane < CHUNK, act[2 * pair:2 * pair + 1, :], act[2 * pair + 1:2 * pair + 2, :])
                        dec = jnp.exp2(jnp.where(causal_dup, acum[sub_rows[u], cols] - rowb, -jnp.inf))
                        w_parts.append((cbs[u][g] * dec).astype(BF16))
                        xp = xdt[sub_rows[u], cols]
                        zero = jnp.zeros_like(xp)
                        bd_parts.append(jnp.concatenate(
                            [jnp.where(lo_half64, xp, zero), jnp.where(lo_half64, zero, xp)], axis=0).astype(BF16))
                    ws.append(jnp.concatenate(w_parts, axis=1))
                    xbds.append(jnp.concatenate(
                        [jnp.concatenate([bd_parts[0], zero_bd], axis=1),
                         jnp.concatenate([zero_bd, bd_parts[1]], axis=1)], axis=0))

            for hh in range(MEM_HEADS):
                gm = proj[3, prow, G3_GM + hh * MEM_HEAD_DIM:G3_GM + (hh + 1) * MEM_HEAD_DIM]
                c0 = ATTN_WIDTH + SSD_WIDTH + hh * MEM_HEAD_DIM
                mix[rows, c0:c0 + MEM_HEAD_DIM] = (mem_o[hh] * _silu(gm)).astype(BF16)

            att_o = []
            for u in subs:
                for g in range(N_KV_HEADS):
                    j0 = (u * N_KV_HEADS + g) * 2
                    att_o.append(_dot(att_p[j0], vr[2 * g, krows[u], :]) + _dot(att_p[j0 + 1], vr[2 * g + 1, krows[u], :]))
            for g in range(SSD_GROUPS):
                gcols = slice(g * gw, (g + 1) * gw)
                st = state_t[:, gcols]
                if n_sub == 1:
                    st = st * dec_end[0][:, gcols] + _dot_tn(bgs[0][g], x_end[0][:, gcols])
                else:
                    st = st * (dec_end[0][:, gcols] * dec_end[1][:, gcols]) + _dot_tn(
                        b_all[:, g * SSD_STATE:(g + 1) * SSD_STATE], x_end_stack[:, gcols])
                state_t[:, gcols] = st
            cross = ([_dot(cross_cb[g].astype(BF16), x_end[0][:, g * gw:(g + 1) * gw]) for g in range(SSD_GROUPS)]
                     if n_sub == 2 else None)
            y_diag = [_dot(w, xbd) for w, xbd in zip(ws, xbds)]

            issue(len(pieces))
            for pp in range(i * out_per_block + out_per_block // 2, (i + 1) * out_per_block):
                out_piece(mix_prev, pp)
            yield

            for u in subs:
                for g in range(N_KV_HEADS):
                    base = g * 2 * LANES
                    orow = pl.ds(r0 + u * CHUNK, CHUNK)
                    ga = proj[0, pl.ds(r0 + u * CHUNK + HEAD_ROWS, CHUNK), G0_GA + base:G0_GA + base + 2 * LANES]
                    o = att_o[u * N_KV_HEADS + g]
                    mix[orow, base:base + LANES] = (o[0:CHUNK] * _silu(ga[:, 0:LANES])).astype(BF16)
                    mix[orow, base + LANES:base + 2 * LANES] = (o[CHUNK:] * _silu(ga[:, LANES:])).astype(BF16)

            y_rows = []
            for u in subs:
                ys = []
                for pair in range(n_pairs):
                    g, hp = divmod(pair, pairs_per_group)
                    cols = slice(pair * LANES, (pair + 1) * LANES)
                    lc = slice(hp * LANES, (hp + 1) * LANES)
                    y_off = y_in[g][sub_rows[u], lc]
                    if u == 1:
                        y_off = y_off * dec_end[0][:, cols] + cross[g][:, lc]
                    yd = y_diag[(u * n_pairs + pair) // 2][:, (pair % 2) * LANES:(pair % 2 + 1) * LANES]
                    ys.append(yd + y_off * exp_acum[sub_rows[u], cols] + d_skip[:, cols] * xs[sub_rows[u], cols])
                y_rows.append(jnp.concatenate(ys, axis=1))
            y_all = y_rows[0] if n_sub == 1 else jnp.concatenate(y_rows, axis=0)
            z = proj[1, prow, G1_Z:G1_Z + SSD_WIDTH]
            yg = y_all * _silu(z)
            mix[rows, ATTN_WIDTH:ATTN_WIDTH + SSD_WIDTH] = (yg * _rms_scale(yg) * g_ssd_ref[...]).astype(BF16)

            xa = x_ahead_ref[0, rows, :]
            hb[rows, :] = (xa * _rms_scale(xa) * g_norm_ref[...]).astype(BF16)

        n_segments = 4
        gens = [block_body(blk) for blk in range(n_blocks)]
        for slot in range(2 * (n_blocks - 1) + n_segments):
            for blk in reversed(range(n_blocks)):
                if 0 <= slot - 2 * blk < n_segments:
                    next(gens[blk], None)

        @pl.when(t == n_tiles - 1)
        def _final():
            newk_ref[0] = kwin[T:T + WINDOW, :]
            newv_ref[0] = vwin[T:T + WINDOW, :]
            ssm_out_ref[0] = state_t[...].T
            r_lo = T + HEAD_ROWS - (CONV_WIDTH - 1)
            conv_out_ref[0, :, 0:GROUP_W] = proj[2, r_lo:T + HEAD_ROWS, :]
            conv_out_ref[0, :, GROUP_W:CONV_CH] = proj[3, r_lo:T + HEAD_ROWS, 0:CONV_CH - GROUP_W]

    parity = s % 2

    @pl.when((parity == 0) & (s < total_steps))
    def _even():
        step(proj_a, hb_a, mix_a, proj_b, hb_b, mix_b)

    @pl.when((parity == 1) & (s < total_steps))
    def _odd():
        step(proj_b, hb_b, mix_b, proj_a, hb_a, mix_a)

    @pl.when(s == total_steps)
    def _flush():
        for pp in range(OUT_PIECES):
            out_piece(mix_b if total_steps % 2 == 0 else mix_a, pp)


def _constants(tile, n_sub):
    n_chunks = tile // CHUNK
    rows_per_block = n_sub * CHUNK
    idx = np.arange
    segq = (idx(2 * LANES)[:, None] // HEAD_DIM == idx(2 * LANES)[None, :] // HEAD_DIM)
    segk = (idx(KV_WIDTH)[:, None] // HEAD_DIM == idx(KV_WIDTH)[None, :] // HEAD_DIM)
    r = idx(rows_per_block)
    tri = (r[:, None] // CHUNK == r[None, :] // CHUNK) & (r[:, None] >= r[None, :])
    tri = np.concatenate([tri, tri, tri], axis=1)
    src = idx(rows_per_block)[:, None]
    dst = idx(n_sub * LANES)[None, :]
    duptri = (src // CHUNK == dst // LANES) & (src % CHUNK <= dst % CHUNK)
    return [jnp.asarray(m.astype(np.float32), dtype=BF16) for m in (segq, segk, tri, duptri)]


def _layer(x, positions, mem_k, mem_v, past, params, *, tile):
    b, seq, _ = x.shape
    mem_k = mem_k.reshape(b, MEM_TOKENS * MEM_HEADS, MEM_HEAD_DIM)
    mem_v = mem_v.reshape(b, MEM_TOKENS * MEM_HEADS, MEM_HEAD_DIM)
    n_t = seq // tile
    n_steps = b * n_t
    has_past = past is not None
    n_chunks = tile // CHUNK
    n_sub = 2 if n_chunks % 2 == 0 else 1
    assert seq % tile == 0 and tile % CHUNK == 0 and N_GROUPS % (n_chunks // n_sub) == 0
    assert n_t == 1 or tile >= WINDOW

    half = HEAD_DIM // 2
    freqs = ROPE_THETA ** (-jnp.arange(half, dtype=F32) / half)
    ang = positions.astype(F32)[:, None] * freqs[None, :]
    cos, sin = jnp.cos(ang), jnp.sin(ang)
    cos_t = jnp.tile(cos, (1, LANES // half))
    sin_t = jnp.tile(jnp.concatenate([-sin, sin], axis=1), (1, LANES // HEAD_DIM))

    consts = _constants(tile, n_sub)

    def full(a):
        nd = a.ndim
        return pl.BlockSpec(a.shape, lambda i, _n=nd: (0,) * _n, pipeline_mode=pl.Buffered(1))

    def row_of(i):
        return jnp.minimum(i, n_steps - 1) // n_t

    def per_row(a):
        nd = a.ndim
        return pl.BlockSpec((1,) + a.shape[1:], lambda i, _n=nd: (row_of(i),) + (0,) * (_n - 1))

    def tile_at(j):
        j = jnp.clip(j, 0, n_steps - 1)
        return (j // n_t, j % n_t, 0)

    vec_params = [params[k] for k in ("g_norm", "gq", "gk")]
    tail_params = [params[k] for k in ("conv_w", "conv_b", "dtb_row", "dtb_col", "alog_row", "alog_col",
                                       "d_skip", "g_ssd", "g_mq")]
    past_inputs = list(past) if has_past else []
    inputs = [x, x, x, x, cos_t, sin_t, mem_k, mem_v, *past_inputs,
              params["w_in"], params["w_dt"], params["w_out"],
              *vec_params, params["sinks"], *tail_params, *consts]
    in_specs = [
        pl.BlockSpec((1, tile, D_MODEL), lambda i: tile_at(0)),
        pl.BlockSpec((1, tile, D_MODEL), lambda i: tile_at(1)),
        pl.BlockSpec((1, tile, D_MODEL), lambda i: tile_at(i + 2)),
        pl.BlockSpec((1, tile, D_MODEL), lambda i: tile_at(i - 1)),
        pl.BlockSpec((tile, LANES), lambda i: (i % n_t, 0)),
        pl.BlockSpec((tile, LANES), lambda i: (i % n_t, 0)),
        per_row(mem_k), per_row(mem_v),
        *[per_row(p) for p in past_inputs],
        full(params["w_in"]), full(params["w_dt"]), full(params["w_out"]),
        *[full(p) for p in vec_params],
        pl.BlockSpec(memory_space=pltpu.SMEM),
        *[full(p) for p in tail_params],
        *[full(c) for c in consts],
    ]
    out_shape = [
        jax.ShapeDtypeStruct((b, seq, D_MODEL), F32),
        jax.ShapeDtypeStruct((b, WINDOW, KV_WIDTH), F32),
        jax.ShapeDtypeStruct((b, WINDOW, KV_WIDTH), F32),
        jax.ShapeDtypeStruct((b, SSD_WIDTH, SSD_STATE), F32),
        jax.ShapeDtypeStruct((b, CONV_WIDTH - 1, CONV_CH), F32),
    ]
    out_specs = [
        pl.BlockSpec((1, tile, D_MODEL), lambda i: tile_at(i - 1)),
        pl.BlockSpec((1, WINDOW, KV_WIDTH), lambda i: (row_of(i), 0, 0)),
        pl.BlockSpec((1, WINDOW, KV_WIDTH), lambda i: (row_of(i), 0, 0)),
        pl.BlockSpec((1, SSD_WIDTH, SSD_STATE), lambda i: (row_of(i), 0, 0)),
        pl.BlockSpec((1, CONV_WIDTH - 1, CONV_CH), lambda i: (row_of(i), 0, 0)),
    ]
    proj_shape = (N_GROUPS, HEAD_ROWS + tile, GROUP_W)
    scratch = [
        pltpu.VMEM(proj_shape, F32), pltpu.VMEM(proj_shape, F32),
        pltpu.VMEM((tile, D_MODEL), BF16), pltpu.VMEM((tile, D_MODEL), BF16),
        pltpu.VMEM((tile, MIX_WIDTH), BF16), pltpu.VMEM((tile, MIX_WIDTH), BF16),
        pltpu.VMEM((tile, ATTN_WIDTH), BF16),
        pltpu.VMEM((WINDOW + tile, KV_WIDTH), F32),
        pltpu.VMEM((WINDOW + tile, KV_WIDTH), F32),
        pltpu.VMEM((4, WINDOW + tile, KV_WIDTH), BF16),
        pltpu.VMEM((4, WINDOW + tile, KV_WIDTH), BF16),
        pltpu.VMEM((SSD_STATE, SSD_WIDTH), F32),
        pltpu.VMEM((MEM_TOKENS, MEM_WIDTH), BF16),
        pltpu.VMEM((MEM_TOKENS, MEM_WIDTH), BF16),
    ]
    return pl.pallas_call(
        functools.partial(_layer_kernel, tile=tile, n_tiles=n_t, total_steps=n_steps, n_sub=n_sub,
                          has_past=has_past),
        grid=(n_steps + 1,),
        in_specs=in_specs,
        out_specs=out_specs,
        out_shape=out_shape,
        scratch_shapes=scratch,
        compiler_params=pltpu.CompilerParams(
            dimension_semantics=("arbitrary",),
            vmem_limit_bytes=VMEM_LIMIT_BYTES),
        name="layer_sample" if has_past else "layer_prompt",
    )(*inputs)


def _prep_params(g_norm, w_in, g_q, g_k, sinks, conv_w, conv_b, dt_bias, a_log, d_skip, g_ssd, g_mq, w_out):
    o_z = ATTN_WIDTH + 2 * KV_WIDTH + ATTN_WIDTH
    o_x = o_z + SSD_WIDTH
    o_b = o_x + SSD_WIDTH
    o_c = o_b + SSD_GROUPS * SSD_STATE
    o_dt = o_c + SSD_GROUPS * SSD_STATE
    o_mq = o_dt + SSD_HEADS
    w_in = w_in.astype(BF16)
    w_dt = w_in[:, o_dt:o_mq]
    zeros = lambda n: jnp.zeros((D_MODEL, n), w_in.dtype)
    groups = [
        w_in[:, 0:o_z],
        jnp.concatenate([w_in[:, o_z:o_x], w_dt, zeros(GROUP_W - SSD_WIDTH - SSD_HEADS)], axis=1),
        w_in[:, o_x:o_c],
        jnp.concatenate([w_in[:, o_c:o_dt], w_in[:, o_mq:]], axis=1),
    ]
    pad = LANES - SSD_HEADS
    return {
        "w_in": jnp.stack(groups),
        "w_dt": w_dt.T,
        "w_out": w_out.astype(BF16),
        "g_norm": g_norm.reshape(1, D_MODEL),
        "gq": (jnp.tile(g_q, N_Q_HEADS) * (LOG2E / math.sqrt(HEAD_DIM))).reshape(1, ATTN_WIDTH),
        "gk": jnp.tile(g_k, N_KV_HEADS).reshape(1, KV_WIDTH),
        "sinks": sinks,
        "conv_w": conv_w,
        "conv_b": conv_b.reshape(1, CONV_CH),
        "dtb_row": jnp.pad(dt_bias, (0, pad)).reshape(1, LANES),
        "dtb_col": dt_bias.reshape(SSD_HEADS, 1),
        "alog_row": jnp.pad(a_log, (0, pad)).reshape(1, LANES),
        "alog_col": a_log.reshape(SSD_HEADS, 1),
        "d_skip": jnp.repeat(d_skip, SSD_HEAD_DIM).reshape(1, SSD_WIDTH),
        "g_ssd": g_ssd.reshape(1, SSD_WIDTH),
        "g_mq": g_mq.reshape(1, MEM_HEAD_DIM),
    }


PROMPT_TILE = 256


def kernel(x_prompt, x_sample, cache_win_k, cache_win_v, state_ssm, state_conv, cache_mem_k, cache_mem_v,
           mem_prompt, g_norm, w_in, g_q, g_k, sinks, conv_w, conv_b, dt_bias, a_log, d_skip, g_ssd,
           g_mem, w_mem_k, w_mem_v, g_mk, g_mq, w_out):
    depth = w_in.shape[0]
    assert depth == 1
    l = 0
    bp, t_p, _ = x_prompt.shape
    bs, t_s, _ = x_sample.shape
    params = _prep_params(g_norm[l], w_in[l], g_q[l], g_k[l], sinks[l], conv_w[l], conv_b[l], dt_bias[l],
                          a_log[l], d_skip[l], g_ssd[l], g_mq[l], w_out[l])

    mk_p, mv_p = _memory_kv(mem_prompt, g_mem[l], w_mem_k[l], w_mem_v[l], g_mk[l])

    yp, kp, vp, sp, cp = _layer(
        x_prompt, jnp.arange(t_p, dtype=jnp.int32), mk_p, mv_p, None, params, tile=min(PROMPT_TILE, t_p))

    past = (cache_win_k[l].reshape(bs, WINDOW, KV_WIDTH), cache_win_v[l].reshape(bs, WINDOW, KV_WIDTH),
            state_ssm[l].reshape(bs, SSD_WIDTH, SSD_STATE), state_conv[l])
    ys, ks, vs, ss, cs = _layer(
        x_sample, PAST_LEN + jnp.arange(t_s, dtype=jnp.int32),
        cache_mem_k[l], cache_mem_v[l],
        past, params, tile=t_s)

    kv5 = lambda a, b: a.reshape(1, b, WINDOW, N_KV_HEADS, HEAD_DIM)
    ssm5 = lambda a, b: a.reshape(1, b, SSD_HEADS, SSD_HEAD_DIM, SSD_STATE)
    mem5 = lambda a: a[None]
    return (yp, ys,
            kv5(kp, bp), kv5(vp, bp), ssm5(sp, bp), cp[None],
            mem5(mk_p), mem5(mv_p),
            kv5(ks, bs), kv5(vs, bs), ssm5(ss, bs), cs[None])
```

```python
import functools
import math

import numpy as np
import jax
import jax.numpy as jnp
from jax import lax
from jax.experimental import pallas as pl
from jax.experimental.pallas import tpu as pltpu

D_MODEL = 1024
CHUNK = 64
HEAD_DIM = 64
N_Q_HEADS = 8
N_KV_HEADS = 2
ATTN_WIDTH = N_Q_HEADS * HEAD_DIM
KV_WIDTH = N_KV_HEADS * HEAD_DIM
WINDOW = 128
ROPE_THETA = 10000.0
SSD_WIDTH = D_MODEL
SSD_HEAD_DIM = 64
SSD_HEADS = SSD_WIDTH // SSD_HEAD_DIM
SSD_GROUPS = 2
SSD_STATE = 128
CONV_WIDTH = 4
CONV_CH = SSD_WIDTH + 2 * SSD_GROUPS * SSD_STATE
MEM_TOKENS = 256
MEM_HEADS = 4
MEM_HEAD_DIM = 128
MEM_WIDTH = MEM_HEADS * MEM_HEAD_DIM
MIX_WIDTH = 2 * D_MODEL
EPS = 1e-6
PAST_LEN = 2048

LANES = 128
SUBLANES = 8
VMEM_LIMIT_BYTES = 56 * 1024 * 1024

N_GROUPS = 4
GROUP_W = 1280
G0_Q, G0_K, G0_V, G0_GA = 0, ATTN_WIDTH, ATTN_WIDTH + KV_WIDTH, ATTN_WIDTH + 2 * KV_WIDTH
G1_Z, G1_DT = 0, SSD_WIDTH
G2_X, G2_B = 0, SSD_WIDTH
G3_C, G3_MQ, G3_GM = 0, SSD_GROUPS * SSD_STATE, SSD_GROUPS * SSD_STATE + MEM_WIDTH
PIECE_W = 256
N_PIECES = GROUP_W // PIECE_W
OUT_PIECES = D_MODEL // PIECE_W
HEAD_ROWS = SUBLANES

F32 = jnp.float32
BF16 = jnp.bfloat16
NEG_BIG = -1e30
LOG2E = math.log2(math.e)


def _dot(a, b):
    return jnp.dot(a, b, preferred_element_type=F32)


def _dot_nt(a, b):
    return lax.dot_general(a, b, (((1,), (1,)), ((), ())), preferred_element_type=F32)


def _dot_tn(a, b):
    return lax.dot_general(a, b, (((0,), (0,)), ((), ())), preferred_element_type=F32)


def _split3(x):
    h1 = x.astype(BF16)
    r1 = x - h1.astype(F32)
    h2 = r1.astype(BF16)
    r2 = r1 - h2.astype(F32)
    return h1, h2, r2.astype(BF16)


def _dot_sel_rhs(x, sel):
    a, b, c = _split3(x)
    return (_dot(c, sel) + _dot(b, sel)) + _dot(a, sel)


def _dot_sel_lhs(sel, x):
    a, b, c = _split3(x)
    return (_dot(sel, c) + _dot(sel, b)) + _dot(sel, a)


def _seg_sumsq(x, seg):
    return _dot((x * x).astype(BF16), seg)


def _silu(x):
    h = 0.5 * x
    return h + h * jnp.tanh(h)


def _softplus(x):
    return jnp.maximum(x, 0.0) + jnp.log(1.0 + jnp.exp(-jnp.abs(x)))


def _rms_scale(x):
    return lax.rsqrt(jnp.mean(x * x, axis=-1, keepdims=True) + EPS)


def _softmax_rows_log2(sc, extra=None):
    m = jnp.max(sc, axis=-1, keepdims=True)
    if extra is not None:
        m = jnp.maximum(m, extra)
    e = jnp.exp2(sc - m)
    den = jnp.sum(e, axis=-1, keepdims=True)
    if extra is not None:
        den = den + jnp.exp2(extra - m)
    return (e * (1.0 / den)).astype(BF16)


def _rope(x, cos, sin_signed, first_half):
    partner = jnp.where(first_half, pltpu.roll(x, LANES - HEAD_DIM // 2, 1), pltpu.roll(x, HEAD_DIM // 2, 1))
    return x * cos + partner * sin_signed


def _memory_kv_kernel(mem_ref, g_mem_ref, wk_ref, wv_ref, g_mk_ref, k_out, v_out):
    x = mem_ref[0]
    hb = (x * _rms_scale(x) * g_mem_ref[...]).astype(BF16)
    k = _dot(hb, wk_ref[...])
    v = _dot(hb, wv_ref[...])
    for hh in range(MEM_HEADS):
        sl = slice(hh * MEM_HEAD_DIM, (hh + 1) * MEM_HEAD_DIM)
        kh = k[:, sl]
        k_out[0, :, hh, :] = kh * _rms_scale(kh) * g_mk_ref[...]
        v_out[0, :, hh, :] = v[:, sl]


def _memory_kv(mem, g_mem, w_k, w_v, g_mk):
    b = mem.shape[0]
    full = lambda shape: pl.BlockSpec(shape, lambda i: (0,) * len(shape))
    return pl.pallas_call(
        _memory_kv_kernel,
        grid=(b,),
        in_specs=[
            pl.BlockSpec((1, MEM_TOKENS, D_MODEL), lambda i: (i, 0, 0)),
            full((1, D_MODEL)),
            full((D_MODEL, MEM_WIDTH)),
            full((D_MODEL, MEM_WIDTH)),
            full((1, MEM_HEAD_DIM)),
        ],
        out_specs=[
            pl.BlockSpec((1, MEM_TOKENS, MEM_HEADS, MEM_HEAD_DIM), lambda i: (i, 0, 0, 0)),
            pl.BlockSpec((1, MEM_TOKENS, MEM_HEADS, MEM_HEAD_DIM), lambda i: (i, 0, 0, 0)),
        ],
        out_shape=[jax.ShapeDtypeStruct((b, MEM_TOKENS, MEM_HEADS, MEM_HEAD_DIM), F32)] * 2,
        compiler_params=pltpu.CompilerParams(dimension_semantics=("arbitrary",)),
        name="memory_kv",
    )(mem, g_mem.reshape(1, D_MODEL), w_k.astype(BF16), w_v.astype(BF16), g_mk.reshape(1, MEM_HEAD_DIM))


def _layer_kernel(*refs, tile, n_tiles, total_steps, n_sub, has_past):
    x_first_ref, x_second_ref, x_ahead_ref, x_prev_ref, cos_ref, sin_ref, memk_ref, memv_ref, *rest = refs
    if has_past:
        wink_ref, winv_ref, ssm_ref, conv_ref, *rest = rest
    else:
        wink_ref = winv_ref = ssm_ref = conv_ref = None
    (w_in_ref, wdt_ref, w_out_ref,
     g_norm_ref, gq_ref, gk_ref, sinks_ref, convw_ref, convb_ref,
     dtb_row_ref, dtb_col_ref, alog_row_ref, alog_col_ref, dskip_ref, g_ssd_ref, g_mq_ref,
     segq_ref, segk_ref, tri_ref, duptri_ref,
     y_ref, newk_ref, newv_ref, ssm_out_ref, conv_out_ref,
     proj_a, proj_b, hb_a, hb_b, mix_a, mix_b,
     qs, kwin, vwin, kr, vr, state_t, memk_bf, memv_bf) = rest

    T = tile
    n_chunks = T // CHUNK
    n_blocks = n_chunks // n_sub
    R = n_sub * CHUNK
    groups_per_block = N_GROUPS // n_blocks
    s = pl.program_id(0)
    t = s % n_tiles
    body_rows = pl.ds(HEAD_ROWS, T)

    def normed_input(x_ref, hb):
        x = x_ref[0]
        hb[...] = (x * _rms_scale(x) * g_norm_ref[...]).astype(BF16)

    def project_piece(hb, proj, g, i):
        cols = slice(i * PIECE_W, (i + 1) * PIECE_W)
        proj[g, body_rows, cols] = _dot(hb[...], w_in_ref[g, :, cols])

    @pl.when(s == 0)
    def _prologue():
        normed_input(x_first_ref, hb_a)
        normed_input(x_second_ref, hb_b)
        mix_b[...] = jnp.zeros((T, MIX_WIDTH), BF16)
        for g in range(N_GROUPS):
            for i in range(N_PIECES):
                project_piece(hb_a, proj_a, g, i)

    lane = lax.broadcasted_iota(jnp.int32, (1, LANES), 1)
    first_half = (lane % HEAD_DIM) < (HEAD_DIM // 2)
    lo_half = lane < HEAD_DIM
    row64 = lax.broadcasted_iota(jnp.int32, (CHUNK, LANES), 0)
    lane64 = lax.broadcasted_iota(jnp.int32, (CHUNK, LANES), 1)
    causal_dup = row64 >= (lane64 % CHUNK)
    lo_half64 = lane64 < CHUNK
    row128 = lax.broadcasted_iota(jnp.int32, (2 * CHUNK, 1), 0)
    key_idx = lax.broadcasted_iota(jnp.int32, (1, WINDOW + CHUNK), 1)
    gw = SSD_WIDTH // SSD_GROUPS
    n_pairs = SSD_HEADS // 2
    pairs_per_group = n_pairs // SSD_GROUPS
    mem_scale = LOG2E / math.sqrt(MEM_HEAD_DIM)
    bc_w = SSD_GROUPS * SSD_STATE

    def expand_heads(x):
        tiles = []
        for pair in range(n_pairs):
            even = jnp.broadcast_to(x[:, 2 * pair:2 * pair + 1], (x.shape[0], LANES))
            odd = jnp.broadcast_to(x[:, 2 * pair + 1:2 * pair + 2], (x.shape[0], LANES))
            tiles.append(jnp.where(lo_half, even, odd))
        return jnp.concatenate(tiles, axis=1)

    def build_masked(src, dst, rows):
        full = src[rows, :]
        swapped = pltpu.roll(full, HEAD_DIM, 1)
        zero = jnp.zeros_like(full)
        dst[0, rows, :] = jnp.where(lo_half, full, zero).astype(BF16)
        dst[1, rows, :] = jnp.where(lo_half, zero, swapped).astype(BF16)
        dst[2, rows, :] = jnp.where(lo_half, swapped, zero).astype(BF16)
        dst[3, rows, :] = jnp.where(lo_half, zero, full).astype(BF16)

    def out_piece(mix_prev, pp):
        cols = slice(pp * PIECE_W, (pp + 1) * PIECE_W)
        y_ref[0, :, cols] = x_prev_ref[0, :, cols] + _dot(mix_prev[...], w_out_ref[:, cols])

    def step(proj, hb, mix, proj_nxt, hb_nxt, mix_prev):
        @pl.when(t == 0)
        def _init():
            zero_head = jnp.zeros((HEAD_ROWS, GROUP_W), F32)
            proj[2, 0:HEAD_ROWS, :] = zero_head
            proj[3, 0:HEAD_ROWS, :] = zero_head
            if has_past:
                kwin[0:WINDOW, :] = wink_ref[0]
                vwin[0:WINDOW, :] = winv_ref[0]
                state_t[...] = ssm_ref[0].T
                tail = conv_ref[0]
                r_lo = HEAD_ROWS - (CONV_WIDTH - 1)
                proj[2, r_lo:HEAD_ROWS, :] = tail[:, 0:GROUP_W]
                proj[3, r_lo:HEAD_ROWS, 0:CONV_CH - GROUP_W] = tail[:, GROUP_W:CONV_CH]
                for src, dst in ((kwin, kr), (vwin, vr)):
                    build_masked(src, dst, slice(0, WINDOW))
            else:
                state_t[...] = jnp.zeros((SSD_STATE, SSD_WIDTH), F32)
                kwin[0:WINDOW, :] = jnp.zeros((WINDOW, KV_WIDTH), F32)
                vwin[0:WINDOW, :] = jnp.zeros((WINDOW, KV_WIDTH), F32)
                for v in range(4):
                    kr[v, 0:WINDOW, :] = jnp.zeros((WINDOW, KV_WIDTH), BF16)
                    vr[v, 0:WINDOW, :] = jnp.zeros((WINDOW, KV_WIDTH), BF16)
            for hh in range(MEM_HEADS):
                sl = slice(hh * MEM_HEAD_DIM, (hh + 1) * MEM_HEAD_DIM)
                head_rows = pl.ds(hh, MEM_TOKENS, stride=MEM_HEADS)
                memk_bf[:, sl] = memk_ref[0, head_rows, :].astype(BF16)
                memv_bf[:, sl] = memv_ref[0, head_rows, :].astype(BF16)

        if n_tiles > 1:
            @pl.when(t > 0)
            def _shift():
                kwin[0:WINDOW, :] = kwin[T:T + WINDOW, :]
                vwin[0:WINDOW, :] = vwin[T:T + WINDOW, :]
                for v in range(4):
                    kr[v, 0:WINDOW, :] = kr[v, T:T + WINDOW, :]
                    vr[v, 0:WINDOW, :] = vr[v, T:T + WINDOW, :]
                proj[2, 0:HEAD_ROWS, :] = proj_nxt[2, T:T + HEAD_ROWS, :]
                proj[3, 0:HEAD_ROWS, :] = proj_nxt[3, T:T + HEAD_ROWS, :]

        a_col = -jnp.exp(alog_col_ref[...]) * LOG2E
        a_row = -jnp.exp(alog_row_ref[...]) * LOG2E
        tri = tri_ref[...]
        d_skip = dskip_ref[...]
        subs = range(n_sub)

        out_per_block = OUT_PIECES // n_blocks

        def block_body(i):
            r0 = i * R
            rows = pl.ds(r0, R)
            prow = pl.ds(r0 + HEAD_ROWS, R)
            wrow = pl.ds(r0 + WINDOW, R)
            sub_rows = [slice(u * CHUNK, (u + 1) * CHUNK) for u in subs]
            krows = [pl.ds(r0 + u * CHUNK, WINDOW + CHUNK) for u in subs]

            pieces = [(i * groups_per_block + gg, pp) for gg in range(groups_per_block) for pp in range(N_PIECES)]
            gap_share = (1, 3, 5, 1)
            per_gap = [len(pieces) * sh // sum(gap_share) for sh in gap_share]

            def issue(n):
                for _ in range(n):
                    g, pp = pieces.pop(0)
                    project_piece(hb_nxt, proj_nxt, g, pp)

            issue(per_gap[0])

            q = proj[0, prow, G0_Q:G0_Q + ATTN_WIDTH]
            k = proj[0, prow, G0_K:G0_K + KV_WIDTH]
            dt_row = _softplus(proj[1, prow, G1_DT:G1_DT + LANES] + dtb_row_ref[...])
            mem_q = []
            for hh in range(MEM_HEADS):
                mqh = proj[3, prow, G3_MQ + hh * MEM_HEAD_DIM:G3_MQ + (hh + 1) * MEM_HEAD_DIM]
                mem_q.append((mqh * _rms_scale(mqh) * g_mq_ref[...]).astype(BF16))

            seg = segq_ref[...]
            q_blocks = [q[:, j * 2 * LANES:(j + 1) * 2 * LANES] for j in range(ATTN_WIDTH // (2 * LANES))]
            q_ss = [_seg_sumsq(qj, seg) for qj in q_blocks]
            k_ss = _seg_sumsq(k, segk_ref[...])
            dtx = expand_heads(dt_row)
            acum_n = _dot_sel_lhs(tri, dt_row * a_row)
            mem_sc = [_dot_nt(mem_q[hh], memk_bf[:, hh * MEM_HEAD_DIM:(hh + 1) * MEM_HEAD_DIM])
                      for hh in range(MEM_HEADS)]
            dt_t = _dot_nt(wdt_ref[...], hb[rows, :])

            issue(per_gap[1])
            yield

            def conv(group, c0, width, ch0):
                acc = convb_ref[:, ch0:ch0 + width]
                blk = proj[group, pl.ds(r0, HEAD_ROWS + R), c0:c0 + width]
                for j in range(CONV_WIDTH):
                    back = CONV_WIDTH - 1 - j
                    tap = blk if back == 0 else pltpu.roll(blk, back, 0)
                    acc = acc + tap[HEAD_ROWS:HEAD_ROWS + R, :] * convw_ref[j:j + 1, ch0:ch0 + width]
                return _silu(acc)

            b_all = conv(2, G2_B, bc_w, SSD_WIDTH).astype(BF16)
            c_all = conv(3, G3_C, bc_w, SSD_WIDTH + bc_w).astype(BF16)
            xs = conv(2, G2_X, SSD_WIDTH, 0)

            cos = cos_ref[rows, :]
            sin_s = sin_ref[rows, :]
            for j, (qj, ss) in enumerate(zip(q_blocks, q_ss)):
                qn = qj * lax.rsqrt(ss * (1.0 / HEAD_DIM) + EPS) * gq_ref[:, j * 2 * LANES:(j + 1) * 2 * LANES]
                for jj in range(2):
                    c0 = j * 2 * LANES + jj * LANES
                    qs[rows, c0:c0 + LANES] = _rope(qn[:, jj * LANES:(jj + 1) * LANES], cos, sin_s,
                                                    first_half).astype(BF16)
            kn = k * lax.rsqrt(k_ss * (1.0 / HEAD_DIM) + EPS) * gk_ref[...]
            kwin[wrow, :] = _rope(kn, cos, sin_s, first_half)
            vwin[wrow, :] = proj[0, prow, G0_V:G0_V + KV_WIDTH]
            build_masked(kwin, kr, wrow)
            build_masked(vwin, vr, wrow)

            mem_p = [_softmax_rows_log2(mem_sc[hh] * mem_scale) for hh in range(MEM_HEADS)]
            xdt = xs * dtx
            da_t3 = _split3(_softplus(dt_t + dtb_col_ref[...]) * a_col)

            bgs = [[b_all[sub_rows[u], g * SSD_STATE:(g + 1) * SSD_STATE] for g in range(SSD_GROUPS)] for u in subs]
            cgs = [[c_all[sub_rows[u], g * SSD_STATE:(g + 1) * SSD_STATE] for g in range(SSD_GROUPS)] for u in subs]
            cbs = [[_dot_nt(cgs[u][g], jnp.concatenate([bgs[u][g], bgs[u][g]], axis=0))
                    for g in range(SSD_GROUPS)] for u in subs]
            cross_cb = [_dot_nt(cgs[1][g], bgs[0][g]) for g in range(SSD_GROUPS)] if n_sub == 2 else None
            y_in = [_dot(c_all[:, g * SSD_STATE:(g + 1) * SSD_STATE], state_t[:, g * gw:(g + 1) * gw].astype(BF16))
                    for g in range(SSD_GROUPS)]
            acum = expand_heads(acum_n)
            dup = duptri_ref[...]
            acum_t = (_dot(da_t3[2], dup) + _dot(da_t3[1], dup)) + _dot(da_t3[0], dup)
            att_sc = []
            for u in subs:
                for g in range(N_KV_HEADS):
                    base = g * 2 * LANES
                    qrows = pl.ds(r0 + u * CHUNK, CHUNK)
                    qg = jnp.concatenate([qs[qrows, base:base + LANES], qs[qrows, base + LANES:base + 2 * LANES]],
                                         axis=0)
                    for pos in range(2):
                        att_sc.append(_dot_nt(qg, kr[2 * g + pos, krows[u], :]))
            mem_o = [_dot(mem_p[hh], memv_bf[:, hh * MEM_HEAD_DIM:(hh + 1) * MEM_HEAD_DIM])
                     for hh in range(MEM_HEADS)]

            issue(per_gap[2])
            for pp in range(i * out_per_block, i * out_per_block + out_per_block // 2):
                out_piece(mix_prev, pp)
            yield

            att_p = []
            for u in subs:
                needs_mask = (not has_past) and (i * R + u * CHUNK < WINDOW)
                if needs_mask:
                    first_valid = jnp.where(t == 0, WINDOW - (r0 + u * CHUNK), 0)
                    key_ok = key_idx >= first_valid
                for g in range(N_KV_HEADS):
                    for pos in range(2):
                        hd0 = 4 * g + pos
                        sink = jnp.where(row128 < CHUNK, sinks_ref[hd0], sinks_ref[hd0 + 2]) * LOG2E
                        sc = att_sc[(u * N_KV_HEADS + g) * 2 + pos]
                        if needs_mask:
                            sc = jnp.where(key_ok, sc, NEG_BIG)
                        att_p.append(_softmax_rows_log2(sc, sink))

            exp_acum = jnp.exp2(acum)
            a_last = [acum[u * CHUNK + CHUNK - 1:(u + 1) * CHUNK, :] for u in subs]
            dec_end = [jnp.exp2(a_last[u]) for u in subs]
            x_end_f = [xdt[sub_rows[u], :] * jnp.exp2(a_last[u] - acum[sub_rows[u], :]) for u in subs]
            x_end = [xe.astype(BF16) for xe in x_end_f]
            if n_sub == 2:
                x_end_stack = jnp.concatenate([(x_end_f[0] * dec_end[1]).astype(BF16), x_end[1]], axis=0)
            ws, xbds = [], []
            zero_bd = jnp.zeros((2 * CHUNK, LANES), BF16)
            for u in subs:
                act = acum_t[:, u * LANES:(u + 1) * LANES]
                for quad in range(n_pairs // 2):
                    w_parts, bd_parts = [], []
                    for pair in (2 * quad, 2 * quad + 1):
                        g = pair // pairs_per_group
                        cols = slice(pair * LANES, (pair + 1) * LANES)
                        rowb = jnp.where(lane < CHUNK, act[2 * pair:2 * pair + 1, :], act[2 * pair + 1:2 * pair + 2, :])
                        dec = jnp.exp2(jnp.where(causal_dup, acum[sub_rows[u], cols] - rowb, -jnp.inf))
                        w_parts.append((cbs[u][g] * dec).astype(BF16))
                        xp = xdt[sub_rows[u], cols]
                        zero = jnp.zeros_like(xp)
                        bd_parts.append(jnp.concatenate(
                            [jnp.where(lo_half64, xp, zero), jnp.where(lo_half64, zero, xp)], axis=0).astype(BF16))
                    ws.append(jnp.concatenate(w_parts, axis=1))
                    xbds.append(jnp.concatenate(
                        [jnp.concatenate([bd_parts[0], zero_bd], axis=1),
                         jnp.concatenate([zero_bd, bd_parts[1]], axis=1)], axis=0))

            for hh in range(MEM_HEADS):
                gm = proj[3, prow, G3_GM + hh * MEM_HEAD_DIM:G3_GM + (hh + 1) * MEM_HEAD_DIM]
                c0 = ATTN_WIDTH + SSD_WIDTH + hh * MEM_HEAD_DIM
                mix[rows, c0:c0 + MEM_HEAD_DIM] = (mem_o[hh] * _silu(gm)).astype(BF16)

            att_o = []
            for u in subs:
                for g in range(N_KV_HEADS):
                    j0 = (u * N_KV_HEADS + g) * 2
                    att_o.append(_dot(att_p[j0], vr[2 * g, krows[u], :]) + _dot(att_p[j0 + 1], vr[2 * g + 1, krows[u], :]))
            for g in range(SSD_GROUPS):
                gcols = slice(g * gw, (g + 1) * gw)
                st = state_t[:, gcols]
                if n_sub == 1:
                    st = st * dec_end[0][:, gcols] + _dot_tn(bgs[0][g], x_end[0][:, gcols])
                else:
                    st = st * (dec_end[0][:, gcols] * dec_end[1][:, gcols]) + _dot_tn(
                        b_all[:, g * SSD_STATE:(g + 1) * SSD_STATE], x_end_stack[:, gcols])
                state_t[:, gcols] = st
            cross = ([_dot(cross_cb[g].astype(BF16), x_end[0][:, g * gw:(g + 1) * gw]) for g in range(SSD_GROUPS)]
                     if n_sub == 2 else None)
            y_diag = [_dot(w, xbd) for w, xbd in zip(ws, xbds)]

            issue(len(pieces))
            for pp in range(i * out_per_block + out_per_block // 2, (i + 1) * out_per_block):
                out_piece(mix_prev, pp)
            yield

            for u in subs:
                for g in range(N_KV_HEADS):
                    base = g * 2 * LANES
                    orow = pl.ds(r0 + u * CHUNK, CHUNK)
                    ga = proj[0, pl.ds(r0 + u * CHUNK + HEAD_ROWS, CHUNK), G0_GA + base:G0_GA + base + 2 * LANES]
                    o = att_o[u * N_KV_HEADS + g]
                    mix[orow, base:base + LANES] = (o[0:CHUNK] * _silu(ga[:, 0:LANES])).astype(BF16)
                    mix[orow, base + LANES:base + 2 * LANES] = (o[CHUNK:] * _silu(ga[:, LANES:])).astype(BF16)

            y_rows = []
            for u in subs:
                ys = []
                for pair in range(n_pairs):
                    g, hp = divmod(pair, pairs_per_group)
                    cols = slice(pair * LANES, (pair + 1) * LANES)
                    lc = slice(hp * LANES, (hp + 1) * LANES)
                    y_off = y_in[g][sub_rows[u], lc]
                    if u == 1:
                        y_off = y_off * dec_end[0][:, cols] + cross[g][:, lc]
                    yd = y_diag[(u * n_pairs + pair) // 2][:, (pair % 2) * LANES:(pair % 2 + 1) * LANES]
                    ys.append(yd + y_off * exp_acum[sub_rows[u], cols] + d_skip[:, cols] * xs[sub_rows[u], cols])
                y_rows.append(jnp.concatenate(ys, axis=1))
            y_all = y_rows[0] if n_sub == 1 else jnp.concatenate(y_rows, axis=0)
            z = proj[1, prow, G1_Z:G1_Z + SSD_WIDTH]
            yg = y_all * _silu(z)
            mix[rows, ATTN_WIDTH:ATTN_WIDTH + SSD_WIDTH] = (yg * _rms_scale(yg) * g_ssd_ref[...]).astype(BF16)

            xa = x_ahead_ref[0, rows, :]
            hb[rows, :] = (xa * _rms_scale(xa) * g_norm_ref[...]).astype(BF16)

        n_segments = 4
        gens = [block_body(blk) for blk in range(n_blocks)]
        for slot in range(2 * (n_blocks - 1) + n_segments):
            for blk in reversed(range(n_blocks)):
                if 0 <= slot - 2 * blk < n_segments:
                    next(gens[blk], None)

        @pl.when(t == n_tiles - 1)
        def _final():
            newk_ref[0] = kwin[T:T + WINDOW, :]
            newv_ref[0] = vwin[T:T + WINDOW, :]
            ssm_out_ref[0] = state_t[...].T
            r_lo = T + HEAD_ROWS - (CONV_WIDTH - 1)
            conv_out_ref[0, :, 0:GROUP_W] = proj[2, r_lo:T + HEAD_ROWS, :]
            conv_out_ref[0, :, GROUP_W:CONV_CH] = proj[3, r_lo:T + HEAD_ROWS, 0:CONV_CH - GROUP_W]

    parity = s % 2

    @pl.when((parity == 0) & (s < total_steps))
    def _even():
        step(proj_a, hb_a, mix_a, proj_b, hb_b, mix_b)

    @pl.when((parity == 1) & (s < total_steps))
    def _odd():
        step(proj_b, hb_b, mix_b, proj_a, hb_a, mix_a)

    @pl.when(s == total_steps)
    def _flush():
        for pp in range(OUT_PIECES):
            out_piece(mix_b if total_steps % 2 == 0 else mix_a, pp)


def _constants(tile, n_sub):
    n_chunks = tile // CHUNK
    rows_per_block = n_sub * CHUNK
    idx = np.arange
    segq = (idx(2 * LANES)[:, None] // HEAD_DIM == idx(2 * LANES)[None, :] // HEAD_DIM)
    segk = (idx(KV_WIDTH)[:, None] // HEAD_DIM == idx(KV_WIDTH)[None, :] // HEAD_DIM)
    r = idx(rows_per_block)
    tri = (r[:, None] // CHUNK == r[None, :] // CHUNK) & (r[:, None] >= r[None, :])
    src = idx(rows_per_block)[:, None]
    dst = idx(n_sub * LANES)[None, :]
    duptri = (src // CHUNK == dst // LANES) & (src % CHUNK <= dst % CHUNK)
    return [jnp.asarray(m.astype(np.float32), dtype=BF16) for m in (segq, segk, tri, duptri)]


def _layer(x, positions, mem_k, mem_v, past, params, *, tile):
    b, seq, _ = x.shape
    mem_k = mem_k.reshape(b, MEM_TOKENS * MEM_HEADS, MEM_HEAD_DIM)
    mem_v = mem_v.reshape(b, MEM_TOKENS * MEM_HEADS, MEM_HEAD_DIM)
    n_t = seq // tile
    n_steps = b * n_t
    has_past = past is not None
    n_chunks = tile // CHUNK
    n_sub = 2 if n_chunks % 2 == 0 else 1
    assert seq % tile == 0 and tile % CHUNK == 0 and N_GROUPS % (n_chunks // n_sub) == 0
    assert n_t == 1 or tile >= WINDOW

    half = HEAD_DIM // 2
    freqs = ROPE_THETA ** (-jnp.arange(half, dtype=F32) / half)
    ang = positions.astype(F32)[:, None] * freqs[None, :]
    cos, sin = jnp.cos(ang), jnp.sin(ang)
    cos_t = jnp.tile(cos, (1, LANES // half))
    sin_t = jnp.tile(jnp.concatenate([-sin, sin], axis=1), (1, LANES // HEAD_DIM))

    consts = _constants(tile, n_sub)

    def full(a):
        nd = a.ndim
        return pl.BlockSpec(a.shape, lambda i, _n=nd: (0,) * _n, pipeline_mode=pl.Buffered(1))

    def row_of(i):
        return jnp.minimum(i, n_steps - 1) // n_t

    def per_row(a):
        nd = a.ndim
        return pl.BlockSpec((1,) + a.shape[1:], lambda i, _n=nd: (row_of(i),) + (0,) * (_n - 1))

    def tile_at(j):
        j = jnp.clip(j, 0, n_steps - 1)
        return (j // n_t, j % n_t, 0)

    vec_params = [params[k] for k in ("g_norm", "gq", "gk")]
    tail_params = [params[k] for k in ("conv_w", "conv_b", "dtb_row", "dtb_col", "alog_row", "alog_col",
                                       "d_skip", "g_ssd", "g_mq")]
    past_inputs = list(past) if has_past else []
    inputs = [x, x, x, x, cos_t, sin_t, mem_k, mem_v, *past_inputs,
              params["w_in"], params["w_dt"], params["w_out"],
              *vec_params, params["sinks"], *tail_params, *consts]
    in_specs = [
        pl.BlockSpec((1, tile, D_MODEL), lambda i: tile_at(0)),
        pl.BlockSpec((1, tile, D_MODEL), lambda i: tile_at(1)),
        pl.BlockSpec((1, tile, D_MODEL), lambda i: tile_at(i + 2)),
        pl.BlockSpec((1, tile, D_MODEL), lambda i: tile_at(i - 1)),
        pl.BlockSpec((tile, LANES), lambda i: (i % n_t, 0)),
        pl.BlockSpec((tile, LANES), lambda i: (i % n_t, 0)),
        per_row(mem_k), per_row(mem_v),
        *[per_row(p) for p in past_inputs],
        full(params["w_in"]), full(params["w_dt"]), full(params["w_out"]),
        *[full(p) for p in vec_params],
        pl.BlockSpec(memory_space=pltpu.SMEM),
        *[full(p) for p in tail_params],
        *[full(c) for c in consts],
    ]
    out_shape = [
        jax.ShapeDtypeStruct((b, seq, D_MODEL), F32),
        jax.ShapeDtypeStruct((b, WINDOW, KV_WIDTH), F32),
        jax.ShapeDtypeStruct((b, WINDOW, KV_WIDTH), F32),
        jax.ShapeDtypeStruct((b, SSD_WIDTH, SSD_STATE), F32),
        jax.ShapeDtypeStruct((b, CONV_WIDTH - 1, CONV_CH), F32),
    ]
    out_specs = [
        pl.BlockSpec((1, tile, D_MODEL), lambda i: tile_at(i - 1)),
        pl.BlockSpec((1, WINDOW, KV_WIDTH), lambda i: (row_of(i), 0, 0)),
        pl.BlockSpec((1, WINDOW, KV_WIDTH), lambda i: (row_of(i), 0, 0)),
        pl.BlockSpec((1, SSD_WIDTH, SSD_STATE), lambda i: (row_of(i), 0, 0)),
        pl.BlockSpec((1, CONV_WIDTH - 1, CONV_CH), lambda i: (row_of(i), 0, 0)),
    ]
    proj_shape = (N_GROUPS, HEAD_ROWS + tile, GROUP_W)
    scratch = [
        pltpu.VMEM(proj_shape, F32), pltpu.VMEM(proj_shape, F32),
        pltpu.VMEM((tile, D_MODEL), BF16), pltpu.VMEM((tile, D_MODEL), BF16),
        pltpu.VMEM((tile, MIX_WIDTH), BF16), pltpu.VMEM((tile, MIX_WIDTH), BF16),
        pltpu.VMEM((tile, ATTN_WIDTH), BF16),
        pltpu.VMEM((WINDOW + tile, KV_WIDTH), F32),
        pltpu.VMEM((WINDOW + tile, KV_WIDTH), F32),
        pltpu.VMEM((4, WINDOW + tile, KV_WIDTH), BF16),
        pltpu.VMEM((4, WINDOW + tile, KV_WIDTH), BF16),
        pltpu.VMEM((SSD_STATE, SSD_WIDTH), F32),
        pltpu.VMEM((MEM_TOKENS, MEM_WIDTH), BF16),
        pltpu.VMEM((MEM_TOKENS, MEM_WIDTH), BF16),
    ]
    return pl.pallas_call(
        functools.partial(_layer_kernel, tile=tile, n_tiles=n_t, total_steps=n_steps, n_sub=n_sub,
                          has_past=has_past),
        grid=(n_steps + 1,),
        in_specs=in_specs,
        out_specs=out_specs,
        out_shape=out_shape,
        scratch_shapes=scratch,
        compiler_params=pltpu.CompilerParams(
            dimension_semantics=("arbitrary",),
            vmem_limit_bytes=VMEM_LIMIT_BYTES),
        name="layer_sample" if has_past else "layer_prompt",
    )(*inputs)


def _prep_params(g_norm, w_in, g_q, g_k, sinks, conv_w, conv_b, dt_bias, a_log, d_skip, g_ssd, g_mq, w_out):
    o_z = ATTN_WIDTH + 2 * KV_WIDTH + ATTN_WIDTH
    o_x = o_z + SSD_WIDTH
    o_b = o_x + SSD_WIDTH
    o_c = o_b + SSD_GROUPS * SSD_STATE
    o_dt = o_c + SSD_GROUPS * SSD_STATE
    o_mq = o_dt + SSD_HEADS
    w_in = w_in.astype(BF16)
    w_dt = w_in[:, o_dt:o_mq]
    zeros = lambda n: jnp.zeros((D_MODEL, n), w_in.dtype)
    groups = [
        w_in[:, 0:o_z],
        jnp.concatenate([w_in[:, o_z:o_x], w_dt, zeros(GROUP_W - SSD_WIDTH - SSD_HEADS)], axis=1),
        w_in[:, o_x:o_c],
        jnp.concatenate([w_in[:, o_c:o_dt], w_in[:, o_mq:]], axis=1),
    ]
    pad = LANES - SSD_HEADS
    return {
        "w_in": jnp.stack(groups),
        "w_dt": w_dt.T,
        "w_out": w_out.astype(BF16),
        "g_norm": g_norm.reshape(1, D_MODEL),
        "gq": (jnp.tile(g_q, N_Q_HEADS) * (LOG2E / math.sqrt(HEAD_DIM))).reshape(1, ATTN_WIDTH),
        "gk": jnp.tile(g_k, N_KV_HEADS).reshape(1, KV_WIDTH),
        "sinks": sinks,
        "conv_w": conv_w,
        "conv_b": conv_b.reshape(1, CONV_CH),
        "dtb_row": jnp.pad(dt_bias, (0, pad)).reshape(1, LANES),
        "dtb_col": dt_bias.reshape(SSD_HEADS, 1),
        "alog_row": jnp.pad(a_log, (0, pad)).reshape(1, LANES),
        "alog_col": a_log.reshape(SSD_HEADS, 1),
        "d_skip": jnp.repeat(d_skip, SSD_HEAD_DIM).reshape(1, SSD_WIDTH),
        "g_ssd": g_ssd.reshape(1, SSD_WIDTH),
        "g_mq": g_mq.reshape(1, MEM_HEAD_DIM),
    }


PROMPT_TILE = 256


def kernel(x_prompt, x_sample, cache_win_k, cache_win_v, state_ssm, state_conv, cache_mem_k, cache_mem_v,
           mem_prompt, g_norm, w_in, g_q, g_k, sinks, conv_w, conv_b, dt_bias, a_log, d_skip, g_ssd,
           g_mem, w_mem_k, w_mem_v, g_mk, g_mq, w_out):
    depth = w_in.shape[0]
    assert depth == 1
    l = 0
    bp, t_p, _ = x_prompt.shape
    bs, t_s, _ = x_sample.shape
    params = _prep_params(g_norm[l], w_in[l], g_q[l], g_k[l], sinks[l], conv_w[l], conv_b[l], dt_bias[l],
                          a_log[l], d_skip[l], g_ssd[l], g_mq[l], w_out[l])

    mk_p, mv_p = _memory_kv(mem_prompt, g_mem[l], w_mem_k[l], w_mem_v[l], g_mk[l])

    yp, kp, vp, sp, cp = _layer(
        x_prompt, jnp.arange(t_p, dtype=jnp.int32), mk_p, mv_p, None, params, tile=min(PROMPT_TILE, t_p))

    past = (cache_win_k[l].reshape(bs, WINDOW, KV_WIDTH), cache_win_v[l].reshape(bs, WINDOW, KV_WIDTH),
            state_ssm[l].reshape(bs, SSD_WIDTH, SSD_STATE), state_conv[l])
    ys, ks, vs, ss, cs = _layer(
        x_sample, PAST_LEN + jnp.arange(t_s, dtype=jnp.int32),
        cache_mem_k[l], cache_mem_v[l],
        past, params, tile=t_s)

    kv5 = lambda a, b: a.reshape(1, b, WINDOW, N_KV_HEADS, HEAD_DIM)
    ssm5 = lambda a, b: a.reshape(1, b, SSD_HEADS, SSD_HEAD_DIM, SSD_STATE)
    mem5 = lambda a: a[None]
    return (yp, ys,
            kv5(kp, bp), kv5(vp, bp), ssm5(sp, bp), cp[None],
            mem5(mk_p), mem5(mv_p),
            kv5(ks, bs), kv5(vs, bs), ssm5(ss, bs), cs[None])
```
